```python
import jax, jax.numpy as jnp
from jax import lax
import numpy as np

D_MODEL = 1024
BATCH = 2
SEQ = 8192
DEPTH = 4

CHUNK = 64
Q_BLOCK = 2 * CHUNK
N_META = 16

HEAD_DIM = 64
ATT_W = D_MODEL // 2
ATT_HEADS = ATT_W // HEAD_DIM

CONV_W = D_MODEL // 4
CONV_K = 31

POOL_W = D_MODEL // 4
POOL_WINDOWS = (2, 4, 8, 16)
POOL_GROUPS = len(POOL_WINDOWS)
POOL_GROUP_W = POOL_W // POOL_GROUPS

N_BRANCH = 3
IN_W = 3 * ATT_W + ATT_HEADS + 2 * CONV_W + POOL_W + N_BRANCH * D_MODEL

D_FF = ((8 * D_MODEL // 3 + 127) // 128) * 128
N_EXPERTS = 8
TOP_K = 2
D_FF_EXPERT = 7 * D_MODEL // 2
N_DENSE = (DEPTH + 1) // 2
N_MOE = DEPTH // 2

NEG_INF = -1e30

kernel_name = "hybrid_fox_conformer_pool_moe_trunk"


def rms_norm(x, g, eps=1e-6):
    xf = x.astype(jnp.float32)
    y = xf * lax.rsqrt(jnp.mean(xf * xf, axis=-1, keepdims=True) + eps)
    return (y * g.astype(jnp.float32)).astype(x.dtype)


def layer_norm(x, g, b, eps=1e-5):
    xf = x.astype(jnp.float32)
    mu = jnp.mean(xf, axis=-1, keepdims=True)
    var = jnp.mean(jnp.square(xf - mu), axis=-1, keepdims=True)
    y = (xf - mu) * lax.rsqrt(var + eps)
    return (y * g.astype(jnp.float32) + b.astype(jnp.float32)).astype(x.dtype)


def swiglu(h, wg, wu, wd):
    return (jax.nn.silu(h @ wg) * (h @ wu)) @ wd


def forgetting_attention(q, k, v, F):
    B, H, L, hd = q.shape
    nb = -(-L // Q_BLOCK)
    Lp = nb * Q_BLOCK
    pad = Lp - L
    pad4 = ((0, 0), (0, 0), (0, pad), (0, 0))
    q, k, v = jnp.pad(q, pad4), jnp.pad(k, pad4), jnp.pad(v, pad4)
    F = jnp.pad(F, ((0, 0), (0, 0), (0, pad)))
    qb = q.reshape(B, H, nb, Q_BLOCK, hd).transpose(2, 0, 1, 3, 4)
    Fb = F.reshape(B, H, nb, Q_BLOCK).transpose(2, 0, 1, 3)
    kpos = jnp.arange(Lp)
    scale = HEAD_DIM ** -0.5

    def block(args):
        qi, Fi, i = args
        s = jnp.einsum('bhqd,bhkd->bhqk', qi, k, preferred_element_type=jnp.float32) * scale
        s = s + Fi[..., None] - F[:, :, None, :]
        qpos = i * Q_BLOCK + jnp.arange(Q_BLOCK)
        s = jnp.where(kpos[None, :] <= qpos[:, None], s, NEG_INF)
        p = jax.nn.softmax(s, axis=-1)
        return jnp.einsum('bhqk,bhkd->bhqd', p.astype(v.dtype), v)

    out = lax.map(block, (qb, Fb, jnp.arange(nb)))
    return out.transpose(1, 2, 0, 3, 4).reshape(B, H, Lp, hd)[:, :, :L]


def causal_depthwise_conv(u, w, b):
    y = lax.conv_general_dilated(
        u, w[:, None, :].astype(u.dtype), window_strides=(1,), padding=[(CONV_K - 1, 0)],
        dimension_numbers=('NWC', 'WIO', 'NWC'), feature_group_count=u.shape[-1])
    return y + b


def causal_multiscale_pool(p):
    B, L, C = p.shape
    cs = lax.cumsum(p.astype(jnp.float32), axis=1)
    cs0 = jnp.pad(cs, ((0, 0), (1, 0), (0, 0)))
    t = jnp.arange(L)
    outs = []
    for g, w in enumerate(POOL_WINDOWS):
        c = cs0[:, :, g * POOL_GROUP_W:(g + 1) * POOL_GROUP_W]
        upper = c[:, 1:]
        lower = jnp.pad(c[:, :L + 1 - w], ((0, 0), (w - 1, 0), (0, 0)))
        cnt = jnp.minimum(t + 1, w).astype(jnp.float32)
        outs.append((upper - lower) / cnt[None, :, None])
    return jnp.concatenate(outs, axis=-1).astype(p.dtype)


def mixer_block(hn, w_in, b_in, q_norm, k_norm, w_attn_o, conv_w, conv_b, conv_ln_g, conv_ln_b,
                w_conv_o, pool_w, pool_scale, w_pool_o, w_out):
    B, L, _ = hn.shape
    z = hn @ w_in + b_in
    sizes = [ATT_W, ATT_W, ATT_W, ATT_HEADS, 2 * CONV_W, POOL_W, N_BRANCH * D_MODEL]
    q, k, v, fgl, conv_in, pool_in, gate = jnp.split(z, np.cumsum(sizes)[:-1].tolist(), axis=-1)

    q = rms_norm(q.reshape(B, L, ATT_HEADS, HEAD_DIM), q_norm).transpose(0, 2, 1, 3)
    k = rms_norm(k.reshape(B, L, ATT_HEADS, HEAD_DIM), k_norm).transpose(0, 2, 1, 3)
    v = v.reshape(B, L, ATT_HEADS, HEAD_DIM).transpose(0, 2, 1, 3)
    F = lax.cumsum(jax.nn.log_sigmoid(fgl.astype(jnp.float32)), axis=1).transpose(0, 2, 1)
    att = forgetting_attention(q, k, v, F).transpose(0, 2, 1, 3).reshape(B, L, ATT_W)
    y_att = att @ w_attn_o

    a, g = jnp.split(conv_in, 2, axis=-1)
    u = a * jax.nn.sigmoid(g)
    u = causal_depthwise_conv(u, conv_w, conv_b)
    u = jax.nn.silu(layer_norm(u, conv_ln_g, conv_ln_b))
    y_conv = u @ w_conv_o

    pm = causal_multiscale_pool(pool_in) - pool_in
    pm = jnp.einsum('blgc,gcd->blgd', pm.reshape(B, L, POOL_GROUPS, POOL_GROUP_W), pool_w)
    pm = pm.reshape(B, L, POOL_W) * pool_scale
    y_pool = pm @ w_pool_o

    gts = jax.nn.sigmoid(gate).reshape(B, L, N_BRANCH, D_MODEL)
    m = gts[:, :, 0] * y_att + gts[:, :, 1] * y_conv + gts[:, :, 2] * y_pool
    return m @ w_out


def moe_swiglu(h, w_router, b_router, w_e_gate, w_e_up, w_e_down):
    B, L, D = h.shape
    hf = h.reshape(B * L, D)
    logits = (hf @ w_router).astype(jnp.float32) + b_router.astype(jnp.float32)
    top_v, top_i = lax.top_k(logits, TOP_K)
    gates = jax.nn.softmax(top_v, axis=-1)
    combine = jnp.sum(jax.nn.one_hot(top_i, N_EXPERTS, dtype=jnp.float32) * gates[..., None], axis=1)
    out = jnp.zeros((B * L, D), jnp.float32)
    for e in range(N_EXPERTS):
        y = swiglu(hf, w_e_gate[e], w_e_up[e], w_e_down[e])
        out = out + combine[:, e:e + 1] * y.astype(jnp.float32)
    return out.astype(h.dtype).reshape(B, L, D)


def setup_inputs(seed: int = 0) -> dict:
    key = jax.random.key(seed)
    ks = iter(jax.random.split(key, 32))
    nrm = lambda shape, s: jax.random.normal(next(ks), shape, jnp.float32) * s
    gain = lambda shape: 1.0 + nrm(shape, 0.02)
    return {
        "x": nrm((BATCH, SEQ, D_MODEL), 1.0),
        "meta": nrm((N_META, D_MODEL), 1.0),
        "norm_mix": gain((DEPTH, D_MODEL)),
        "w_in": nrm((DEPTH, D_MODEL, IN_W), D_MODEL ** -0.5),
        "b_in": nrm((DEPTH, IN_W), 0.02),
        "q_norm": gain((DEPTH, HEAD_DIM)),
        "k_norm": gain((DEPTH, HEAD_DIM)),
        "w_attn_o": nrm((DEPTH, ATT_W, D_MODEL), ATT_W ** -0.5),
        "conv_w": nrm((DEPTH, CONV_K, CONV_W), CONV_K ** -0.5),
        "conv_b": nrm((DEPTH, CONV_W), 0.02),
        "conv_ln_g": gain((DEPTH, CONV_W)),
        "conv_ln_b": nrm((DEPTH, CONV_W), 0.02),
        "w_conv_o": nrm((DEPTH, CONV_W, D_MODEL), CONV_W ** -0.5),
        "pool_w": nrm((DEPTH, POOL_GROUPS, POOL_GROUP_W, POOL_GROUP_W), POOL_GROUP_W ** -0.5),
        "pool_scale": gain((DEPTH, POOL_W)),
        "w_pool_o": nrm((DEPTH, POOL_W, D_MODEL), POOL_W ** -0.5),
        "w_out": nrm((DEPTH, D_MODEL, D_MODEL), D_MODEL ** -0.5),
        "norm_ffn": gain((DEPTH, D_MODEL)),
        "w_ff_gate": nrm((N_DENSE, D_MODEL, D_FF), D_MODEL ** -0.5),
        "w_ff_up": nrm((N_DENSE, D_MODEL, D_FF), D_MODEL ** -0.5),
        "w_ff_down": nrm((N_DENSE, D_FF, D_MODEL), D_FF ** -0.5),
        "w_router": nrm((N_MOE, D_MODEL, N_EXPERTS), D_MODEL ** -0.5),
        "b_router": nrm((N_MOE, N_EXPERTS), 0.01),
        "w_e_gate": nrm((N_MOE, N_EXPERTS, D_MODEL, D_FF_EXPERT), D_MODEL ** -0.5),
        "w_e_up": nrm((N_MOE, N_EXPERTS, D_MODEL, D_FF_EXPERT), D_MODEL ** -0.5),
        "w_e_down": nrm((N_MOE, N_EXPERTS, D_FF_EXPERT, D_MODEL), D_FF_EXPERT ** -0.5),
    }


def reference(x, meta, norm_mix, w_in, b_in, q_norm, k_norm, w_attn_o, conv_w, conv_b, conv_ln_g,
              conv_ln_b, w_conv_o, pool_w, pool_scale, w_pool_o, w_out, norm_ffn, w_ff_gate, w_ff_up,
              w_ff_down, w_router, b_router, w_e_gate, w_e_up, w_e_down):
    B = x.shape[0]
    h = jnp.concatenate([jnp.broadcast_to(meta[None].astype(x.dtype), (B, N_META, D_MODEL)), x], axis=1)
    for l in range(DEPTH):
        hn = rms_norm(h, norm_mix[l])
        h = h + mixer_block(hn, w_in[l], b_in[l], q_norm[l], k_norm[l], w_attn_o[l], conv_w[l], conv_b[l],
                            conv_ln_g[l], conv_ln_b[l], w_conv_o[l], pool_w[l], pool_scale[l], w_pool_o[l],
                            w_out[l])
        hn = rms_norm(h, norm_ffn[l])
        if l % 2 == 0:
            i = l // 2
            h = h + swiglu(hn, w_ff_gate[i], w_ff_up[i], w_ff_down[i])
        else:
            i = l // 2
            h = h + moe_swiglu(hn, w_router[i], b_router[i], w_e_gate[i], w_e_up[i], w_e_down[i])
    return h[:, N_META:]
```

```python
import functools

import jax
import jax.numpy as jnp
from jax import lax
from jax.experimental import pallas as pl
from jax.experimental.pallas import tpu as pltpu

F32 = jnp.float32
BF16 = jnp.bfloat16
HIGHEST = lax.Precision.HIGHEST

D_MODEL = 1024
N_META = 16
HEAD_DIM = 64
ATT_W = 512
ATT_HEADS = 8
CONV_W = 256
CONV_K = 31
POOL_W = 256
POOL_WINDOWS = (2, 4, 8, 16)
N_EXPERTS = 8
LANES = 128
HALO = 32
NEG_INF = -1e30
SEQ_ALIGN = 256
VMEM_LIMIT = 56 * 1024 * 1024

Z_GATE = 0
Z_Q = 3072
Z_K = 3584
Z_V = 4096
Z_A = 4608
Z_G = 4864
Z_P = 5120
Z_W = 5376


def _pick(n, candidates):
    for c in candidates:
        if n % c == 0:
            return c
    raise ValueError(f"no tile for {n} in {candidates}")


def _cparams(sem):
    return pltpu.CompilerParams(dimension_semantics=sem, vmem_limit_bytes=VMEM_LIMIT)


def _rms(x, g, eps=1e-6):
    return x * lax.rsqrt(jnp.mean(x * x, axis=-1, keepdims=True) + eps) * g


def _sigmoid(x):
    return 1.0 / (1.0 + jnp.exp(-x))


def _silu(x):
    return x * _sigmoid(x)


def _norm_kernel(h_ref, g_ref, o_ref):
    o_ref[...] = _rms(h_ref[...], g_ref[...]).astype(o_ref.dtype)


def _norm(h, g, tm):
    n, d = h.shape
    return pl.pallas_call(
        _norm_kernel,
        grid=(n // tm,),
        in_specs=[pl.BlockSpec((tm, d), lambda i: (i, 0)), pl.BlockSpec((1, d), lambda i: (0, 0))],
        out_specs=pl.BlockSpec((tm, d), lambda i: (i, 0)),
        out_shape=jax.ShapeDtypeStruct((n, d), BF16),
        compiler_params=_cparams(("parallel",)),
        name="rmsnorm",
    )(h, g)


def _inproj_kernel(x_ref, w_ref, b_ref, o_ref):
    acc = jnp.dot(x_ref[...], w_ref[...], preferred_element_type=F32)
    o_ref[...] = (acc + b_ref[...]).astype(o_ref.dtype)


def _inproj(hn, w, b, tm, tn):
    n, d = hn.shape
    zw = w.shape[1]
    return pl.pallas_call(
        _inproj_kernel,
        grid=(n // tm, zw // tn),
        in_specs=[pl.BlockSpec((tm, d), lambda i, j: (i, 0)),
                  pl.BlockSpec((d, tn), lambda i, j: (0, j)),
                  pl.BlockSpec((1, tn), lambda i, j: (0, j))],
        out_specs=pl.BlockSpec((tm, tn), lambda i, j: (i, j)),
        out_shape=jax.ShapeDtypeStruct((n, zw), BF16),
        compiler_params=_cparams(("parallel", "arbitrary")),
        name="inproj",
    )(hn, w, b)


def _forget_kernel(x_ref, w_ref, b_ref, o_ref, carry_ref):
    @pl.when(pl.program_id(1) == 0)
    def _():
        carry_ref[...] = jnp.zeros_like(carry_ref)

    x = x_ref[0].astype(F32)
    f = jnp.dot(x, w_ref[...], precision=HIGHEST, preferred_element_type=F32) + b_ref[...]
    ls = jnp.minimum(f, 0.0) - jnp.log(1.0 + jnp.exp(-jnp.abs(f)))
    t = ls.shape[0]
    r = lax.broadcasted_iota(jnp.int32, (t, t), 0)
    c = lax.broadcasted_iota(jnp.int32, (t, t), 1)
    tril = jnp.where(c <= r, 1.0, 0.0).astype(F32)
    cs = jnp.dot(tril, ls, precision=HIGHEST, preferred_element_type=F32) + carry_ref[...]
    o_ref[0] = cs
    carry_ref[...] = cs[t - 1:t, :]


def _forget(hn3, w_f, b_f, ts):
    b, lp, d = hn3.shape
    return pl.pallas_call(
        _forget_kernel,
        grid=(b, lp // ts),
        in_specs=[pl.BlockSpec((1, ts, d), lambda bi, t: (bi, t, 0)),
                  pl.BlockSpec((d, LANES), lambda bi, t: (0, 0)),
                  pl.BlockSpec((1, LANES), lambda bi, t: (0, 0))],
        out_specs=pl.BlockSpec((1, ts, LANES), lambda bi, t: (bi, t, 0)),
        out_shape=jax.ShapeDtypeStruct((b, lp, LANES), F32),
        scratch_shapes=[pltpu.VMEM((1, LANES), F32)],
        compiler_params=_cparams(("parallel", "arbitrary")),
        name="forget_cumsum",
    )(hn3, w_f, b_f)


def _pair_rms(x, first_head, gain, eps=1e-6):
    sq = x * x
    s0 = jnp.sum(jnp.where(first_head, sq, 0.0), axis=-1, keepdims=True)
    s1 = jnp.sum(jnp.where(first_head, 0.0, sq), axis=-1, keepdims=True)
    ms = jnp.where(first_head, s0, s1) * (1.0 / HEAD_DIM)
    return x * lax.rsqrt(ms + eps) * gain


def _attn_kernel(q_ref, k_ref, v_ref, fc_ref, fr_ref, qg_ref, kg_ref, o_ref, kn_ref, *, tq, tk):
    p = pl.program_id(1)
    i = pl.program_id(2)
    lp = k_ref.shape[1]
    lane = lax.broadcasted_iota(jnp.int32, (1, LANES), 1)
    first_head = lane < HEAD_DIM

    @pl.when(i == 0)
    def _():
        def body(c, carry):
            off = pl.multiple_of(c * tk, tk)
            kk = k_ref[0, pl.ds(off, tk), :].astype(F32)
            kn_ref[pl.ds(off, tk), :] = _pair_rms(kk, first_head, kg_ref[...]).astype(BF16)
            return carry
        lax.fori_loop(0, lp // tk, body, 0)

    qn = _pair_rms(q_ref[0].astype(F32), first_head, qg_ref[...])
    fc = fc_ref[0]
    row_pos = i * tq + lax.broadcasted_iota(jnp.int32, (tq, 1), 0)
    n_full = (i * tq) // tk
    n_all = ((i + 1) * tq + tk - 1) // tk

    outs = []
    for hh in range(2):
        head = 2 * p + hh
        keep = first_head if hh == 0 else jnp.logical_not(first_head)
        qh = jnp.where(keep, qn, 0.0).astype(BF16)
        ft = jnp.sum(jnp.where(lane == head, fc, 0.0), axis=-1, keepdims=True)

        def step(j, carry, masked, qh=qh, ft=ft, head=head):
            m, l, acc = carry
            off = pl.multiple_of(j * tk, tk)
            ks = kn_ref[pl.ds(off, tk), :]
            s = lax.dot_general(qh, ks, (((1,), (1,)), ((), ())), preferred_element_type=F32)
            fs = fr_ref[0, pl.ds(head, 1), pl.ds(off, tk)]
            s = s + (ft - fs)
            if masked:
                col_pos = off + lax.broadcasted_iota(jnp.int32, (1, tk), 1)
                s = jnp.where(col_pos <= row_pos, s, NEG_INF)
            m_new = jnp.maximum(m, jnp.max(s, axis=-1, keepdims=True))
            alpha = jnp.exp(m - m_new)
            pm = jnp.exp(s - m_new)
            l = alpha * l + jnp.sum(pm, axis=-1, keepdims=True)
            vs = v_ref[0, pl.ds(off, tk), :]
            acc = alpha * acc + jnp.dot(pm.astype(BF16), vs, preferred_element_type=F32)
            return m_new, l, acc

        init = (jnp.full((tq, 1), NEG_INF, F32), jnp.zeros((tq, 1), F32), jnp.zeros((tq, LANES), F32))
        carry = lax.fori_loop(0, n_full, functools.partial(step, masked=False), init)
        m, l, acc = lax.fori_loop(n_full, n_all, functools.partial(step, masked=True), carry)
        outs.append(acc / l)
    o_ref[0] = jnp.where(first_head, outs[0], outs[1]).astype(o_ref.dtype)


def _attention(z3, fcol, frow, qg, kg, tq, tk):
    b, lp, _ = z3.shape
    npairs = ATT_HEADS // 2
    qb, kb, vb = Z_Q // LANES, Z_K // LANES, Z_V // LANES
    return pl.pallas_call(
        functools.partial(_attn_kernel, tq=tq, tk=tk),
        grid=(b, npairs, lp // tq),
        in_specs=[pl.BlockSpec((1, tq, LANES), lambda bi, p, i: (bi, i, qb + p)),
                  pl.BlockSpec((1, lp, LANES), lambda bi, p, i: (bi, 0, kb + p)),
                  pl.BlockSpec((1, lp, LANES), lambda bi, p, i: (bi, 0, vb + p)),
                  pl.BlockSpec((1, tq, LANES), lambda bi, p, i: (bi, i, 0)),
                  pl.BlockSpec((1, ATT_HEADS, lp), lambda bi, p, i: (bi, 0, 0)),
                  pl.BlockSpec((1, LANES), lambda bi, p, i: (0, 0)),
                  pl.BlockSpec((1, LANES), lambda bi, p, i: (0, 0))],
        out_specs=pl.BlockSpec((1, tq, LANES), lambda bi, p, i: (bi, i, p)),
        out_shape=jax.ShapeDtypeStruct((b, lp, ATT_W), BF16),
        scratch_shapes=[pltpu.VMEM((lp, LANES), BF16)],
        compiler_params=_cparams(("parallel", "parallel", "arbitrary")),
        name="fox_attention",
    )(z3, z3, z3, fcol, frow, qg, kg)


def _convpool_kernel(a_ref, g_ref, p_ref, ah_ref, gh_ref, ph_ref, cw_ref, cb_ref, lg_ref, lb_ref,
                     pw_ref, ps_ref, u_ref, pm_ref, ext_ref, pext_ref, *, ts):
    i = pl.program_id(1)
    has_prev = i > 0

    u = a_ref[0].astype(F32) * _sigmoid(g_ref[0].astype(F32))
    uh = ah_ref[0].astype(F32) * _sigmoid(gh_ref[0].astype(F32))
    ext_ref[0:HALO, :] = jnp.where(has_prev, uh, 0.0)
    ext_ref[HALO:HALO + ts, :] = u
    acc = jnp.zeros((ts, CONV_W), F32) + cb_ref[...]
    for j in range(CONV_K):
        start = HALO - (CONV_K - 1) + j
        acc = acc + cw_ref[j:j + 1, :] * ext_ref[start:start + ts, :]
    mu = jnp.mean(acc, axis=-1, keepdims=True)
    cen = acc - mu
    var = jnp.mean(cen * cen, axis=-1, keepdims=True)
    y = cen * lax.rsqrt(var + 1e-5) * lg_ref[...] + lb_ref[...]
    u_ref[0] = _silu(y).astype(u_ref.dtype)

    x = p_ref[0].astype(F32)
    pext_ref[0:HALO, :] = jnp.where(has_prev, ph_ref[0].astype(F32), 0.0)
    pext_ref[HALO:HALO + ts, :] = x
    pos1 = (i * ts + 1 + lax.broadcasted_iota(jnp.int32, (ts, 1), 0)).astype(F32)
    lane = lax.broadcasted_iota(jnp.int32, (1, POOL_W), 1)
    group_w = POOL_W // len(POOL_WINDOWS)
    run = x
    pooled = jnp.zeros((ts, POOL_W), F32)
    for k in range(1, max(POOL_WINDOWS)):
        run = run + pext_ref[HALO - k:HALO - k + ts, :]
        if (k + 1) in POOL_WINDOWS:
            gi = POOL_WINDOWS.index(k + 1)
            mean = run / jnp.minimum(pos1, float(k + 1))
            in_group = (lane >= gi * group_w) & (lane < (gi + 1) * group_w)
            pooled = jnp.where(in_group, mean, pooled)
    pm = (pooled - x).astype(BF16)
    lin = jnp.dot(pm, pw_ref[...], preferred_element_type=F32) * ps_ref[...]
    pm_ref[0] = lin.astype(pm_ref.dtype)


def _convpool(z3, cw, cb, lg, lb, pw, ps, ts):
    b, lp, _ = z3.shape
    ab, gb, pb = Z_A // CONV_W, Z_G // CONV_W, Z_P // POOL_W
    hpt = ts // HALO
    main = lambda blk: pl.BlockSpec((1, ts, CONV_W), lambda bi, i: (bi, i, blk))
    halo = lambda blk: pl.BlockSpec((1, HALO, CONV_W), lambda bi, i: (bi, jnp.maximum(i * hpt - 1, 0), blk))
    const = lambda shape: pl.BlockSpec(shape, lambda bi, i: (0, 0))
    out = pl.BlockSpec((1, ts, CONV_W), lambda bi, i: (bi, i, 0))
    return pl.pallas_call(
        functools.partial(_convpool_kernel, ts=ts),
        grid=(b, lp // ts),
        in_specs=[main(ab), main(gb), main(pb), halo(ab), halo(gb), halo(pb),
                  const((HALO, CONV_W)), const((1, CONV_W)), const((1, CONV_W)), const((1, CONV_W)),
                  const((POOL_W, POOL_W)), const((1, POOL_W))],
        out_specs=[out, out],
        out_shape=[jax.ShapeDtypeStruct((b, lp, CONV_W), BF16), jax.ShapeDtypeStruct((b, lp, POOL_W), BF16)],
        scratch_shapes=[pltpu.VMEM((HALO + ts, CONV_W), F32), pltpu.VMEM((HALO + ts, POOL_W), F32)],
        compiler_params=_cparams(("parallel", "arbitrary")),
        name="conv_pool",
    )(z3, z3, z3, z3, z3, z3, cw, cb, lg, lb, pw, ps)


def _merge_kernel(att_ref, u_ref, pm_ref, g0_ref, g1_ref, g2_ref, h_ref, wa_ref, wc_ref, wp_ref, wo_ref,
                  ng_ref, h_out_ref, hn_out_ref):
    ya = jnp.dot(att_ref[...], wa_ref[...], preferred_element_type=F32)
    yc = jnp.dot(u_ref[...], wc_ref[...], preferred_element_type=F32)
    yp = jnp.dot(pm_ref[...], wp_ref[...], preferred_element_type=F32)
    m = (_sigmoid(g0_ref[...].astype(F32)) * ya + _sigmoid(g1_ref[...].astype(F32)) * yc
         + _sigmoid(g2_ref[...].astype(F32)) * yp)
    h = h_ref[...] + jnp.dot(m.astype(BF16), wo_ref[...], preferred_element_type=F32)
    h_out_ref[...] = h
    hn_out_ref[...] = _rms(h, ng_ref[...]).astype(hn_out_ref.dtype)


def _merge(att, u, pm, z, h, wa, wc, wp, wo, ng, tm):
    n, d = h.shape
    row = lambda w, blk=0: pl.BlockSpec((tm, w), lambda i: (i, blk))
    const = lambda shape: pl.BlockSpec(shape, lambda i: (0, 0))
    return pl.pallas_call(
        _merge_kernel,
        grid=(n // tm,),
        in_specs=[row(ATT_W), row(CONV_W), row(POOL_W), row(d, 0), row(d, 1), row(d, 2), row(d),
                  const(wa.shape), const(wc.shape), const(wp.shape), const(wo.shape), const((1, d))],
        out_specs=[row(d), row(d)],
        out_shape=[jax.ShapeDtypeStruct((n, d), F32), jax.ShapeDtypeStruct((n, d), BF16)],
        compiler_params=_cparams(("parallel",)),
        name="merge_outproj",
    )(att, u, pm, z, z, z, h, wa, wc, wp, wo, ng)


def _ffn_kernel(x_ref, h_ref, wg_ref, wu_ref, wd_ref, ng_ref, h_out_ref, hn_out_ref, acc_ref):
    f = pl.program_id(1)
    x = x_ref[...]
    gt = jnp.dot(x, wg_ref[...], preferred_element_type=F32)
    up = jnp.dot(x, wu_ref[...], preferred_element_type=F32)
    y = jnp.dot((_silu(gt) * up).astype(BF16), wd_ref[...], preferred_element_type=F32)

    @pl.when(f == 0)
    def _():
        acc_ref[...] = h_ref[...] + y

    @pl.when(f > 0)
    def _():
        acc_ref[...] += y

    @pl.when(f == pl.num_programs(1) - 1)
    def _():
        h = acc_ref[...]
        h_out_ref[...] = h
        hn_out_ref[...] = _rms(h, ng_ref[...]).astype(hn_out_ref.dtype)


def _ffn(hn, h, wg, wu, wd, ng, tm, tf):
    n, d = h.shape
    dff = wg.shape[1]
    return pl.pallas_call(
        _ffn_kernel,
        grid=(n // tm, dff // tf),
        in_specs=[pl.BlockSpec((tm, d), lambda i, f: (i, 0)),
                  pl.BlockSpec((tm, d), lambda i, f: (i, 0)),
                  pl.BlockSpec((d, tf), lambda i, f: (0, f)),
                  pl.BlockSpec((d, tf), lambda i, f: (0, f)),
                  pl.BlockSpec((tf, d), lambda i, f: (f, 0)),
                  pl.BlockSpec((1, d), lambda i, f: (0, 0))],
        out_specs=[pl.BlockSpec((tm, d), lambda i, f: (i, 0)), pl.BlockSpec((tm, d), lambda i, f: (i, 0))],
        out_shape=[jax.ShapeDtypeStruct((n, d), F32), jax.ShapeDtypeStruct((n, d), BF16)],
        scratch_shapes=[pltpu.VMEM((tm, d), F32)],
        compiler_params=_cparams(("parallel", "arbitrary")),
        name="dense_swiglu",
    )(hn, h, wg, wu, wd, ng)


def _router_kernel(h_ref, ng_ref, wr_ref, br_ref, comb_ref, pos_ref, cnt_ref):
    tm = h_ref.shape[0]
    hn = _rms(h_ref[...], ng_ref[...])
    logits = jnp.dot(hn, wr_ref[...], precision=HIGHEST, preferred_element_type=F32) + br_ref[...]
    lane = lax.broadcasted_iota(jnp.int32, (1, LANES), 1).astype(F32)
    lg = jnp.where(lane < N_EXPERTS, logits, -jnp.inf)
    m1 = jnp.max(lg, axis=-1, keepdims=True)
    i1 = jnp.min(jnp.where(lg == m1, lane, float(LANES)), axis=-1, keepdims=True)
    sel1 = lane == i1
    lg2 = jnp.where(sel1, -jnp.inf, lg)
    m2 = jnp.max(lg2, axis=-1, keepdims=True)
    i2 = jnp.min(jnp.where(lg2 == m2, lane, float(LANES)), axis=-1, keepdims=True)
    sel2 = lane == i2
    e = jnp.exp(m2 - m1)
    g1 = 1.0 / (1.0 + e)
    comb_ref[...] = jnp.where(sel1, g1, 0.0) + jnp.where(sel2, e * g1, 0.0)
    sel = jnp.where(sel1 | sel2, 1.0, 0.0)
    r = lax.broadcasted_iota(jnp.int32, (tm, tm), 0)
    c = lax.broadcasted_iota(jnp.int32, (tm, tm), 1)
    tril = jnp.where(c <= r, 1.0, 0.0).astype(BF16)
    incl = jnp.dot(tril, sel.astype(BF16), preferred_element_type=F32)
    pos_ref[...] = jnp.where(sel > 0.0, incl - 1.0, -1.0)
    cnt_ref[0] = jnp.broadcast_to(incl[tm - 1:tm, :], (8, LANES)).astype(jnp.int32)


def _router(h, ng, wr, br, tm):
    n, d = h.shape
    nt = n // tm
    return pl.pallas_call(
        _router_kernel,
        grid=(nt,),
        in_specs=[pl.BlockSpec((tm, d), lambda i: (i, 0)),
                  pl.BlockSpec((1, d), lambda i: (0, 0)),
                  pl.BlockSpec((d, LANES), lambda i: (0, 0)),
                  pl.BlockSpec((1, LANES), lambda i: (0, 0))],
        out_specs=[pl.BlockSpec((tm, LANES), lambda i: (i, 0)),
                   pl.BlockSpec((tm, LANES), lambda i: (i, 0)),
                   pl.BlockSpec((1, 8, LANES), lambda i: (i, 0, 0))],
        out_shape=[jax.ShapeDtypeStruct((n, LANES), F32), jax.ShapeDtypeStruct((n, LANES), F32),
                   jax.ShapeDtypeStruct((nt, 8, LANES), jnp.int32)],
        compiler_params=_cparams(("parallel",)),
        name="router_top2",
    )(h, ng, wr, br)


def _moe_kernel(cnt_ref, x_ref, post_ref, pos_ref, comb_ref, h_ref, wg_ref, wu_ref, wd_ref, ng_ref,
                h_out_ref, hn_out_ref, xs_ref, ys_ref, *, tm, rows):
    i = pl.program_id(0)
    e = pl.program_id(1)
    f = pl.program_id(2)
    nf = pl.num_programs(2)
    nch = (cnt_ref[i * N_EXPERTS + e] + rows - 1) // rows

    @pl.when((e == 0) & (f == 0))
    def _():
        h_out_ref[...] = h_ref[...]

    @pl.when(f == 0)
    def _():
        prow = post_ref[pl.ds(e, 1), :]
        def gather(c, carry):
            off = pl.multiple_of(c * rows, rows)
            tgt = (off + lax.broadcasted_iota(jnp.int32, (rows, 1), 0)).astype(F32)
            onehot = jnp.where(prow == tgt, 1.0, 0.0).astype(BF16)
            xs_ref[pl.ds(off, rows), :] = jnp.dot(onehot, x_ref[...], preferred_element_type=F32).astype(BF16)
            ys_ref[pl.ds(off, rows), :] = jnp.zeros((rows, D_MODEL), F32)
            return carry
        lax.fori_loop(0, nch, gather, 0)

    def expert(c, carry):
        off = pl.multiple_of(c * rows, rows)
        xc = xs_ref[pl.ds(off, rows), :]
        gt = jnp.dot(xc, wg_ref[0], preferred_element_type=F32)
        up = jnp.dot(xc, wu_ref[0], preferred_element_type=F32)
        ys_ref[pl.ds(off, rows), :] += jnp.dot((_silu(gt) * up).astype(BF16), wd_ref[0],
                                                preferred_element_type=F32)
        return carry
    lax.fori_loop(0, nch, expert, 0)

    @pl.when(f == nf - 1)
    def _():
        lane = lax.broadcasted_iota(jnp.int32, (1, LANES), 1)
        pcol = jnp.sum(jnp.where(lane == e, pos_ref[...], 0.0), axis=-1, keepdims=True)
        gcol = jnp.sum(jnp.where(lane == e, comb_ref[...], 0.0), axis=-1, keepdims=True)
        def scatter(c, carry):
            off = pl.multiple_of(c * rows, rows)
            tgt = (off + lax.broadcasted_iota(jnp.int32, (1, rows), 1)).astype(F32)
            onehot_t = jnp.where(pcol == tgt, 1.0, 0.0).astype(BF16)
            y = ys_ref[pl.ds(off, rows), :].astype(BF16)
            h_out_ref[...] += gcol * jnp.dot(onehot_t, y, preferred_element_type=F32)
            return carry
        lax.fori_loop(0, nch, scatter, 0)

    @pl.when((e == N_EXPERTS - 1) & (f == nf - 1))
    def _():
        hn_out_ref[...] = _rms(h_out_ref[...], ng_ref[...]).astype(hn_out_ref.dtype)


def _moe(cnt, hn, post, pos, comb, h, wg, wu, wd, ng, tm, tf, rows):
    n, d = h.shape
    dff = wg.shape[2]
    row = lambda w: pl.BlockSpec((tm, w), lambda i, e, f, cnt: (i, 0))
    grid_spec = pltpu.PrefetchScalarGridSpec(
        num_scalar_prefetch=1,
        grid=(n // tm, N_EXPERTS, dff // tf),
        in_specs=[row(d),
                  pl.BlockSpec((N_EXPERTS, tm), lambda i, e, f, cnt: (0, i)),
                  row(LANES), row(LANES), row(d),
                  pl.BlockSpec((1, d, tf), lambda i, e, f, cnt: (e, 0, f)),
                  pl.BlockSpec((1, d, tf), lambda i, e, f, cnt: (e, 0, f)),
                  pl.BlockSpec((1, tf, d), lambda i, e, f, cnt: (e, f, 0)),
                  pl.BlockSpec((1, d), lambda i, e, f, cnt: (0, 0))],
        out_specs=[row(d), row(d)],
        scratch_shapes=[pltpu.VMEM((tm, d), BF16), pltpu.VMEM((tm, d), F32)],
    )
    return pl.pallas_call(
        functools.partial(_moe_kernel, tm=tm, rows=rows),
        grid_spec=grid_spec,
        out_shape=[jax.ShapeDtypeStruct((n, d), F32), jax.ShapeDtypeStruct((n, d), BF16)],
        compiler_params=_cparams(("parallel", "arbitrary", "arbitrary")),
        name="expert_swiglu",
    )(cnt, hn, post, pos, comb, h, wg, wu, wd, ng)


def _pad_lanes(a, width=LANES):
    return jnp.pad(a, ((0, 0), (0, width - a.shape[-1])))


def kernel(x, meta, norm_mix, w_in, b_in, q_norm, k_norm, w_attn_o, conv_w, conv_b, conv_ln_g, conv_ln_b,
           w_conv_o, pool_w, pool_scale, w_pool_o, w_out, norm_ffn, w_ff_gate, w_ff_up, w_ff_down, w_router,
           b_router, w_e_gate, w_e_up, w_e_down):
    bsz, seq, d = x.shape
    depth = w_in.shape[0]
    length = seq + N_META
    lp = -(-length // SEQ_ALIGN) * SEQ_ALIGN
    n = bsz * lp

    tm = _pick(n, (1536, 768, 512, 256))
    tm_moe = _pick(n, (768, 512, 256))
    tn = _pick(Z_W, (1792, 768, 256))
    ts = _pick(lp, (768, 512, 256))
    tq = _pick(lp, (256,))
    tk = 256
    tf_dense = _pick(w_ff_gate.shape[2], (256,))
    tf_moe = _pick(w_e_gate.shape[3], (512, 256))
    rows = 256

    h = jnp.concatenate([jnp.broadcast_to(meta[None].astype(x.dtype), (bsz, N_META, d)), x], axis=1)
    h = jnp.pad(h, ((0, 0), (0, lp - length), (0, 0))).reshape(n, d)
    hn = _norm(h, norm_mix[0][None], tm)

    att_scale = HEAD_DIM ** -0.5
    gate_lo = 3 * ATT_W + ATT_HEADS + 2 * CONV_W + POOL_W
    f_lo = 3 * ATT_W
    for l in range(depth):
        wl = w_in[l]
        w_main = jnp.concatenate([wl[:, gate_lo:], wl[:, :f_lo], wl[:, f_lo + ATT_HEADS:gate_lo]], axis=1).astype(BF16)
        bl = b_in[l]
        b_main = jnp.concatenate([bl[gate_lo:], bl[:f_lo], bl[f_lo + ATT_HEADS:gate_lo]])[None]
        w_f = _pad_lanes(wl[:, f_lo:f_lo + ATT_HEADS])
        b_f = _pad_lanes(bl[None, f_lo:f_lo + ATT_HEADS])

        z = _inproj(hn, w_main, b_main, tm, tn)
        z3 = z.reshape(bsz, lp, Z_W)
        fcol = _forget(hn.reshape(bsz, lp, d), w_f, b_f, ts)
        frow = jnp.transpose(fcol[:, :, :ATT_HEADS], (0, 2, 1))
        qg = jnp.tile(q_norm[l], 2)[None] * att_scale
        kg = jnp.tile(k_norm[l], 2)[None]
        att = _attention(z3, fcol, frow, qg, kg, tq, tk)

        cw = jnp.pad(conv_w[l], ((0, HALO - CONV_K), (0, 0)))
        pw = jax.scipy.linalg.block_diag(*[pool_w[l, g] for g in range(pool_w.shape[1])]).astype(BF16)
        u, pm = _convpool(z3, cw, conv_b[l][None], conv_ln_g[l][None], conv_ln_b[l][None], pw,
                          pool_scale[l][None], ts)

        h, hn = _merge(att.reshape(n, ATT_W), u.reshape(n, CONV_W), pm.reshape(n, POOL_W), z, h,
                       w_attn_o[l].astype(BF16), w_conv_o[l].astype(BF16), w_pool_o[l].astype(BF16),
                       w_out[l].astype(BF16), norm_ffn[l][None], tm_moe)

        ng_next = norm_mix[min(l + 1, depth - 1)][None]
        i = l // 2
        if l % 2 == 0:
            h, hn = _ffn(hn, h, w_ff_gate[i].astype(BF16), w_ff_up[i].astype(BF16), w_ff_down[i].astype(BF16),
                         ng_next, tm, tf_dense)
        else:
            comb, pos, cnt = _router(h, norm_ffn[l][None], _pad_lanes(w_router[i]), _pad_lanes(b_router[i][None]),
                                     tm_moe)
            post = jnp.transpose(pos[:, :N_EXPERTS])
            cnt_flat = cnt[:, 0, :N_EXPERTS].reshape(-1)
            h, hn = _moe(cnt_flat, hn, post, pos, comb, h, w_e_gate[i].astype(BF16), w_e_up[i].astype(BF16),
                         w_e_down[i].astype(BF16), ng_next, tm_moe, tf_moe, rows)

    return h.reshape(bsz, lp, d)[:, N_META:length]
```

```python
import functools

import jax
import jax.numpy as jnp
from jax import lax
from jax.experimental import pallas as pl
from jax.experimental.pallas import tpu as pltpu

F32 = jnp.float32
BF16 = jnp.bfloat16
HIGHEST = lax.Precision.HIGHEST

D_MODEL = 1024
N_META = 16
HEAD_DIM = 64
ATT_W = 512
ATT_HEADS = 8
CONV_W = 256
CONV_K = 31
POOL_W = 256
POOL_WINDOWS = (2, 4, 8, 16)
N_EXPERTS = 8
LANES = 128
HALO = 32
NEG_INF = -1e30
EXP_UNDERFLOW = 110.0
SEQ_ALIGN = 256
VMEM_LIMIT = 56 * 1024 * 1024

Z_GATE = 0
Z_Q = 3072
Z_K = 3584
Z_V = 4096
Z_A = 4608
Z_G = 4864
Z_P = 5120
Z_W = 5376


def _pick(n, candidates):
    for c in candidates:
        if n % c == 0:
            return c
    raise ValueError(f"no tile for {n} in {candidates}")


def _cparams(sem):
    return pltpu.CompilerParams(dimension_semantics=sem, vmem_limit_bytes=VMEM_LIMIT)


def _rms(x, g, eps=1e-6):
    return x * lax.rsqrt(jnp.mean(x * x, axis=-1, keepdims=True) + eps) * g


def _sigmoid(x):
    return 1.0 / (1.0 + jnp.exp(-x))


def _silu(x):
    return x * _sigmoid(x)


def _norm_kernel(h_ref, g_ref, o_ref):
    o_ref[...] = _rms(h_ref[...], g_ref[...]).astype(o_ref.dtype)


def _norm(h, g, tm):
    n, d = h.shape
    return pl.pallas_call(
        _norm_kernel,
        grid=(n // tm,),
        in_specs=[pl.BlockSpec((tm, d), lambda i: (i, 0)), pl.BlockSpec((1, d), lambda i: (0, 0))],
        out_specs=pl.BlockSpec((tm, d), lambda i: (i, 0)),
        out_shape=jax.ShapeDtypeStruct((n, d), BF16),
        compiler_params=_cparams(("parallel",)),
        name="rmsnorm",
    )(h, g)


def _inproj_kernel(x_ref, w_ref, b_ref, o_ref):
    acc = jnp.dot(x_ref[...], w_ref[...], preferred_element_type=F32)
    o_ref[...] = (acc + b_ref[...]).astype(o_ref.dtype)


def _inproj(hn, w, b, tm, tn):
    n, d = hn.shape
    zw = w.shape[1]
    return pl.pallas_call(
        _inproj_kernel,
        grid=(n // tm, zw // tn),
        in_specs=[pl.BlockSpec((tm, d), lambda i, j: (i, 0)),
                  pl.BlockSpec((d, tn), lambda i, j: (0, j)),
                  pl.BlockSpec((1, tn), lambda i, j: (0, j))],
        out_specs=pl.BlockSpec((tm, tn), lambda i, j: (i, j)),
        out_shape=jax.ShapeDtypeStruct((n, zw), BF16),
        compiler_params=_cparams(("parallel", "arbitrary")),
        name="inproj",
    )(hn, w, b)


def _forget_kernel(x_ref, w_ref, b_ref, o_ref, carry_ref):
    @pl.when(pl.program_id(1) == 0)
    def _():
        carry_ref[...] = jnp.zeros_like(carry_ref)

    x = x_ref[0].astype(F32)
    f = jnp.dot(x, w_ref[...], precision=HIGHEST, preferred_element_type=F32) + b_ref[...]
    ls = jnp.minimum(f, 0.0) - jnp.log(1.0 + jnp.exp(-jnp.abs(f)))
    t = ls.shape[0]
    r = lax.broadcasted_iota(jnp.int32, (t, t), 0)
    c = lax.broadcasted_iota(jnp.int32, (t, t), 1)
    tril = jnp.where(c <= r, 1.0, 0.0).astype(F32)
    cs = jnp.dot(tril, ls, precision=HIGHEST, preferred_element_type=F32) + carry_ref[...]
    o_ref[0] = cs
    carry_ref[...] = cs[t - 1:t, :]


def _forget(hn3, w_f, b_f, ts):
    b, lp, d = hn3.shape
    return pl.pallas_call(
        _forget_kernel,
        grid=(b, lp // ts),
        in_specs=[pl.BlockSpec((1, ts, d), lambda bi, t: (bi, t, 0)),
                  pl.BlockSpec((d, LANES), lambda bi, t: (0, 0)),
                  pl.BlockSpec((1, LANES), lambda bi, t: (0, 0))],
        out_specs=pl.BlockSpec((1, ts, LANES), lambda bi, t: (bi, t, 0)),
        out_shape=jax.ShapeDtypeStruct((b, lp, LANES), F32),
        scratch_shapes=[pltpu.VMEM((1, LANES), F32)],
        compiler_params=_cparams(("parallel", "arbitrary")),
        name="forget_cumsum",
    )(hn3, w_f, b_f)


def _pair_rms(x, first_head, gain, eps=1e-6):
    sq = x * x
    s0 = jnp.sum(jnp.where(first_head, sq, 0.0), axis=-1, keepdims=True)
    s1 = jnp.sum(jnp.where(first_head, 0.0, sq), axis=-1, keepdims=True)
    ms = jnp.where(first_head, s0, s1) * (1.0 / HEAD_DIM)
    return x * lax.rsqrt(ms + eps) * gain


def _attn_kernel(lo_ref, q_ref, k_ref, v_ref, fc_ref, fr_ref, qg_ref, kg_ref, o_ref, kn_ref, *, tq, tk):
    bi = pl.program_id(0)
    p = pl.program_id(1)
    i = pl.program_id(2)
    nq = pl.num_programs(2)
    lp = k_ref.shape[1]
    lane = lax.broadcasted_iota(jnp.int32, (1, LANES), 1)
    first_head = lane < HEAD_DIM

    @pl.when(i == 0)
    def _():
        def body(c, carry):
            off = pl.multiple_of(c * tk, tk)
            kk = k_ref[0, pl.ds(off, tk), :].astype(F32)
            kn_ref[pl.ds(off, tk), :] = _pair_rms(kk, first_head, kg_ref[...]).astype(BF16)
            return carry
        lax.fori_loop(0, lp // tk, body, 0)

    qn = _pair_rms(q_ref[0].astype(F32), first_head, qg_ref[...])
    fc = fc_ref[0]
    row_pos = i * tq + lax.broadcasted_iota(jnp.int32, (tq, 1), 0)
    n_full = (i * tq) // tk
    n_all = ((i + 1) * tq + tk - 1) // tk

    outs = []
    for hh in range(2):
        head = 2 * p + hh
        keep = first_head if hh == 0 else jnp.logical_not(first_head)
        qh = jnp.where(keep, qn, 0.0).astype(BF16)
        ft = jnp.sum(jnp.where(lane == head, fc, 0.0), axis=-1, keepdims=True)

        def step(j, carry, masked, qh=qh, ft=ft, head=head):
            m, l, acc = carry
            off = pl.multiple_of(j * tk, tk)
            ks = kn_ref[pl.ds(off, tk), :]
            s = lax.dot_general(qh, ks, (((1,), (1,)), ((), ())), preferred_element_type=F32)
            fs = fr_ref[0, pl.ds(head, 1), pl.ds(off, tk)]
            s = s + (ft - fs)
            if masked:
                col_pos = off + lax.broadcasted_iota(jnp.int32, (1, tk), 1)
                s = jnp.where(col_pos <= row_pos, s, NEG_INF)
            m_new = jnp.maximum(m, jnp.max(s, axis=-1, keepdims=True))
            alpha = jnp.exp(m - m_new)
            pm = jnp.exp(s - m_new)
            l = alpha * l + jnp.sum(pm, axis=-1, keepdims=True)
            vs = v_ref[0, pl.ds(off, tk), :]
            acc = alpha * acc + jnp.dot(pm.astype(BF16), vs, preferred_element_type=F32)
            return m_new, l, acc

        init = (jnp.full((tq, 1), NEG_INF, F32), jnp.zeros((tq, 1), F32), jnp.zeros((tq, LANES), F32))
        first = lo_ref[(bi * ATT_HEADS + head) * nq + i]
        carry = lax.fori_loop(first, n_full, functools.partial(step, masked=False), init)
        m, l, acc = lax.fori_loop(n_full, n_all, functools.partial(step, masked=True), carry)
        outs.append(acc / l)
    o_ref[0] = jnp.where(first_head, outs[0], outs[1]).astype(o_ref.dtype)


def _first_live_chunk(frow, qg, kg, tq, tk):
    b, nh, lp = frow.shape
    qk_bound = 1.02 * HEAD_DIM * jnp.max(jnp.abs(qg)) * jnp.max(jnp.abs(kg))
    f_first = frow[:, :, 0::tq]
    f_last = frow[:, :, tk - 1::tk]
    dead = (f_first[:, :, :, None] - f_last[:, :, None, :] + 2.0 * qk_bound) < -EXP_UNDERFLOW
    n_dead = jnp.sum(dead.astype(jnp.int32), axis=-1)
    n_full = (jnp.arange(lp // tq, dtype=jnp.int32) * tq) // tk
    return jnp.minimum(n_dead, n_full[None, None, :]).reshape(-1)


def _attention(z3, fcol, frow, qg, kg, tq, tk):
    b, lp, _ = z3.shape
    npairs = ATT_HEADS // 2
    qb, kb, vb = Z_Q // LANES, Z_K // LANES, Z_V // LANES
    first = _first_live_chunk(frow, qg, kg, tq, tk)
    grid_spec = pltpu.PrefetchScalarGridSpec(
        num_scalar_prefetch=1,
        grid=(b, npairs, lp // tq),
        in_specs=[pl.BlockSpec((1, tq, LANES), lambda bi, p, i, lo: (bi, i, qb + p)),
                  pl.BlockSpec((1, lp, LANES), lambda bi, p, i, lo: (bi, 0, kb + p)),
                  pl.BlockSpec((1, lp, LANES), lambda bi, p, i, lo: (bi, 0, vb + p)),
                  pl.BlockSpec((1, tq, LANES), lambda bi, p, i, lo: (bi, i, 0)),
                  pl.BlockSpec((1, ATT_HEADS, lp), lambda bi, p, i, lo: (bi, 0, 0)),
                  pl.BlockSpec((1, LANES), lambda bi, p, i, lo: (0, 0)),
                  pl.BlockSpec((1, LANES), lambda bi, p, i, lo: (0, 0))],
        out_specs=pl.BlockSpec((1, tq, LANES), lambda bi, p, i, lo: (bi, i, p)),
        scratch_shapes=[pltpu.VMEM((lp, LANES), BF16)],
    )
    return pl.pallas_call(
        functools.partial(_attn_kernel, tq=tq, tk=tk),
        grid_spec=grid_spec,
        out_shape=jax.ShapeDtypeStruct((b, lp, ATT_W), BF16),
        compiler_params=_cparams(("parallel", "parallel", "arbitrary")),
        name="fox_attention",
    )(first, z3, z3, z3, fcol, frow, qg, kg)


def _convpool_kernel(a_ref, g_ref, p_ref, ah_ref, gh_ref, ph_ref, cw_ref, cb_ref, lg_ref, lb_ref,
                     pw_ref, ps_ref, u_ref, pm_ref, ext_ref, pext_ref, *, ts):
    i = pl.program_id(1)
    has_prev = i > 0

    u = a_ref[0].astype(F32) * _sigmoid(g_ref[0].astype(F32))
    uh = ah_ref[0].astype(F32) * _sigmoid(gh_ref[0].astype(F32))
    ext_ref[0:HALO, :] = jnp.where(has_prev, uh, 0.0)
    ext_ref[HALO:HALO + ts, :] = u
    acc = jnp.zeros((ts, CONV_W), F32) + cb_ref[...]
    for j in range(CONV_K):
        start = HALO - (CONV_K - 1) + j
        acc = acc + cw_ref[j:j + 1, :] * ext_ref[start:start + ts, :]
    mu = jnp.mean(acc, axis=-1, keepdims=True)
    cen = acc - mu
    var = jnp.mean(cen * cen, axis=-1, keepdims=True)
    y = cen * lax.rsqrt(var + 1e-5) * lg_ref[...] + lb_ref[...]
    u_ref[0] = _silu(y).astype(u_ref.dtype)

    x = p_ref[0].astype(F32)
    pext_ref[0:HALO, :] = jnp.where(has_prev, ph_ref[0].astype(F32), 0.0)
    pext_ref[HALO:HALO + ts, :] = x
    pos1 = (i * ts + 1 + lax.broadcasted_iota(jnp.int32, (ts, 1), 0)).astype(F32)
    lane = lax.broadcasted_iota(jnp.int32, (1, POOL_W), 1)
    group_w = POOL_W // len(POOL_WINDOWS)
    run = x
    pooled = jnp.zeros((ts, POOL_W), F32)
    for k in range(1, max(POOL_WINDOWS)):
        run = run + pext_ref[HALO - k:HALO - k + ts, :]
        if (k + 1) in POOL_WINDOWS:
            gi = POOL_WINDOWS.index(k + 1)
            mean = run / jnp.minimum(pos1, float(k + 1))
            in_group = (lane >= gi * group_w) & (lane < (gi + 1) * group_w)
            pooled = jnp.where(in_group, mean, pooled)
    pm = (pooled - x).astype(BF16)
    lin = jnp.dot(pm, pw_ref[...], preferred_element_type=F32) * ps_ref[...]
    pm_ref[0] = lin.astype(pm_ref.dtype)


def _convpool(z3, cw, cb, lg, lb, pw, ps, ts):
    b, lp, _ = z3.shape
    ab, gb, pb = Z_A // CONV_W, Z_G // CONV_W, Z_P // POOL_W
    hpt = ts // HALO
    main = lambda blk: pl.BlockSpec((1, ts, CONV_W), lambda bi, i: (bi, i, blk))
    halo = lambda blk: pl.BlockSpec((1, HALO, CONV_W), lambda bi, i: (bi, jnp.maximum(i * hpt - 1, 0), blk))
    const = lambda shape: pl.BlockSpec(shape, lambda bi, i: (0, 0))
    out = pl.BlockSpec((1, ts, CONV_W), lambda bi, i: (bi, i, 0))
    return pl.pallas_call(
        functools.partial(_convpool_kernel, ts=ts),
        grid=(b, lp // ts),
        in_specs=[main(ab), main(gb), main(pb), halo(ab), halo(gb), halo(pb),
                  const((HALO, CONV_W)), const((1, CONV_W)), const((1, CONV_W)), const((1, CONV_W)),
                  const((POOL_W, POOL_W)), const((1, POOL_W))],
        out_specs=[out, out],
        out_shape=[jax.ShapeDtypeStruct((b, lp, CONV_W), BF16), jax.ShapeDtypeStruct((b, lp, POOL_W), BF16)],
        scratch_shapes=[pltpu.VMEM((HALO + ts, CONV_W), F32), pltpu.VMEM((HALO + ts, POOL_W), F32)],
        compiler_params=_cparams(("parallel", "arbitrary")),
        name="conv_pool",
    )(z3, z3, z3, z3, z3, z3, cw, cb, lg, lb, pw, ps)


def _merge_kernel(att_ref, u_ref, pm_ref, g0_ref, g1_ref, g2_ref, h_ref, wa_ref, wc_ref, wp_ref, wo_ref,
                  ng_ref, h_out_ref, hn_out_ref):
    ya = jnp.dot(att_ref[...], wa_ref[...], preferred_element_type=F32)
    yc = jnp.dot(u_ref[...], wc_ref[...], preferred_element_type=F32)
    yp = jnp.dot(pm_ref[...], wp_ref[...], preferred_element_type=F32)
    m = (_sigmoid(g0_ref[...].astype(F32)) * ya + _sigmoid(g1_ref[...].astype(F32)) * yc
         + _sigmoid(g2_ref[...].astype(F32)) * yp)
    h = h_ref[...] + jnp.dot(m.astype(BF16), wo_ref[...], preferred_element_type=F32)
    h_out_ref[...] = h
    hn_out_ref[...] = _rms(h, ng_ref[...]).astype(hn_out_ref.dtype)


def _merge(att, u, pm, z, h, wa, wc, wp, wo, ng, tm):
    n, d = h.shape
    row = lambda w, blk=0: pl.BlockSpec((tm, w), lambda i: (i, blk))
    const = lambda shape: pl.BlockSpec(shape, lambda i: (0, 0))
    return pl.pallas_call(
        _merge_kernel,
        grid=(n // tm,),
        in_specs=[row(ATT_W), row(CONV_W), row(POOL_W), row(d, 0), row(d, 1), row(d, 2), row(d),
                  const(wa.shape), const(wc.shape), const(wp.shape), const(wo.shape), const((1, d))],
        out_specs=[row(d), row(d)],
        out_shape=[jax.ShapeDtypeStruct((n, d), F32), jax.ShapeDtypeStruct((n, d), BF16)],
        compiler_params=_cparams(("parallel",)),
        name="merge_outproj",
    )(att, u, pm, z, z, z, h, wa, wc, wp, wo, ng)


def _ffn_kernel(x_ref, h_ref, wg_ref, wu_ref, wd_ref, ng_ref, h_out_ref, hn_out_ref, acc_ref):
    f = pl.program_id(1)
    x = x_ref[...]
    gt = jnp.dot(x, wg_ref[...], preferred_element_type=F32)
    up = jnp.dot(x, wu_ref[...], preferred_element_type=F32)
    y = jnp.dot((_silu(gt) * up).astype(BF16), wd_ref[...], preferred_element_type=F32)

    @pl.when(f == 0)
    def _():
        acc_ref[...] = h_ref[...] + y

    @pl.when(f > 0)
    def _():
        acc_ref[...] += y

    @pl.when(f == pl.num_programs(1) - 1)
    def _():
        h = acc_ref[...]
        h_out_ref[...] = h
        hn_out_ref[...] = _rms(h, ng_ref[...]).astype(hn_out_ref.dtype)


def _ffn(hn, h, wg, wu, wd, ng, tm, tf):
    n, d = h.shape
    dff = wg.shape[1]
    return pl.pallas_call(
        _ffn_kernel,
        grid=(n // tm, dff // tf),
        in_specs=[pl.BlockSpec((tm, d), lambda i, f: (i, 0)),
                  pl.BlockSpec((tm, d), lambda i, f: (i, 0)),
                  pl.BlockSpec((d, tf), lambda i, f: (0, f)),
                  pl.BlockSpec((d, tf), lambda i, f: (0, f)),
                  pl.BlockSpec((tf, d), lambda i, f: (f, 0)),
                  pl.BlockSpec((1, d), lambda i, f: (0, 0))],
        out_specs=[pl.BlockSpec((tm, d), lambda i, f: (i, 0)), pl.BlockSpec((tm, d), lambda i, f: (i, 0))],
        out_shape=[jax.ShapeDtypeStruct((n, d), F32), jax.ShapeDtypeStruct((n, d), BF16)],
        scratch_shapes=[pltpu.VMEM((tm, d), F32)],
        compiler_params=_cparams(("parallel", "arbitrary")),
        name="dense_swiglu",
    )(hn, h, wg, wu, wd, ng)


def _router_kernel(h_ref, ng_ref, wr_ref, br_ref, comb_ref, pos_ref, cnt_ref):
    tm = h_ref.shape[0]
    hn = _rms(h_ref[...], ng_ref[...])
    logits = jnp.dot(hn, wr_ref[...], precision=HIGHEST, preferred_element_type=F32) + br_ref[...]
    lane = lax.broadcasted_iota(jnp.int32, (1, LANES), 1).astype(F32)
    lg = jnp.where(lane < N_EXPERTS, logits, -jnp.inf)
    m1 = jnp.max(lg, axis=-1, keepdims=True)
    i1 = jnp.min(jnp.where(lg == m1, lane, float(LANES)), axis=-1, keepdims=True)
    sel1 = lane == i1
    lg2 = jnp.where(sel1, -jnp.inf, lg)
    m2 = jnp.max(lg2, axis=-1, keepdims=True)
    i2 = jnp.min(jnp.where(lg2 == m2, lane, float(LANES)), axis=-1, keepdims=True)
    sel2 = lane == i2
    e = jnp.exp(m2 - m1)
    g1 = 1.0 / (1.0 + e)
    comb_ref[...] = jnp.where(sel1, g1, 0.0) + jnp.where(sel2, e * g1, 0.0)
    sel = jnp.where(sel1 | sel2, 1.0, 0.0)
    r = lax.broadcasted_iota(jnp.int32, (tm, tm), 0)
    c = lax.broadcasted_iota(jnp.int32, (tm, tm), 1)
    tril = jnp.where(c <= r, 1.0, 0.0).astype(BF16)
    incl = jnp.dot(tril, sel.astype(BF16), preferred_element_type=F32)
    pos_ref[...] = jnp.where(sel > 0.0, incl - 1.0, -1.0)
    cnt_ref[0] = jnp.broadcast_to(incl[tm - 1:tm, :], (8, LANES)).astype(jnp.int32)


def _router(h, ng, wr, br, tm):
    n, d = h.shape
    nt = n // tm
    return pl.pallas_call(
        _router_kernel,
        grid=(nt,),
        in_specs=[pl.BlockSpec((tm, d), lambda i: (i, 0)),
                  pl.BlockSpec((1, d), lambda i: (0, 0)),
                  pl.BlockSpec((d, LANES), lambda i: (0, 0)),
                  pl.BlockSpec((1, LANES), lambda i: (0, 0))],
        out_specs=[pl.BlockSpec((tm, LANES), lambda i: (i, 0)),
                   pl.BlockSpec((tm, LANES), lambda i: (i, 0)),
                   pl.BlockSpec((1, 8, LANES), lambda i: (i, 0, 0))],
        out_shape=[jax.ShapeDtypeStruct((n, LANES), F32), jax.ShapeDtypeStruct((n, LANES), F32),
                   jax.ShapeDtypeStruct((nt, 8, LANES), jnp.int32)],
        compiler_params=_cparams(("parallel",)),
        name="router_top2",
    )(h, ng, wr, br)


def _moe_kernel(cnt_ref, x_ref, post_ref, pos_ref, comb_ref, h_ref, wg_ref, wu_ref, wd_ref, ng_ref,
                h_out_ref, hn_out_ref, xs_ref, ys_ref, *, tm, rows):
    i = pl.program_id(0)
    e = pl.program_id(1)
    f = pl.program_id(2)
    nf = pl.num_programs(2)
    nch = (cnt_ref[i * N_EXPERTS + e] + rows - 1) // rows

    @pl.when((e == 0) & (f == 0))
    def _():
        h_out_ref[...] = h_ref[...]

    @pl.when(f == 0)
    def _():
        prow = post_ref[pl.ds(e, 1), :]
        def gather(c, carry):
            off = pl.multiple_of(c * rows, rows)
            tgt = (off + lax.broadcasted_iota(jnp.int32, (rows, 1), 0)).astype(F32)
            onehot = jnp.where(prow == tgt, 1.0, 0.0).astype(BF16)
            xs_ref[pl.ds(off, rows), :] = jnp.dot(onehot, x_ref[...], preferred_element_type=F32).astype(BF16)
            ys_ref[pl.ds(off, rows), :] = jnp.zeros((rows, D_MODEL), F32)
            return carry
        lax.fori_loop(0, nch, gather, 0)

    def expert(c, carry):
        off = pl.multiple_of(c * rows, rows)
        xc = xs_ref[pl.ds(off, rows), :]
        gt = jnp.dot(xc, wg_ref[0], preferred_element_type=F32)
        up = jnp.dot(xc, wu_ref[0], preferred_element_type=F32)
        ys_ref[pl.ds(off, rows), :] += jnp.dot((_silu(gt) * up).astype(BF16), wd_ref[0],
                                                preferred_element_type=F32)
        return carry
    lax.fori_loop(0, nch, expert, 0)

    @pl.when(f == nf - 1)
    def _():
        lane = lax.broadcasted_iota(jnp.int32, (1, LANES), 1)
        pcol = jnp.sum(jnp.where(lane == e, pos_ref[...], 0.0), axis=-1, keepdims=True)
        gcol = jnp.sum(jnp.where(lane == e, comb_ref[...], 0.0), axis=-1, keepdims=True)
        def scatter(c, carry):
            off = pl.multiple_of(c * rows, rows)
            tgt = (off + lax.broadcasted_iota(jnp.int32, (1, rows), 1)).astype(F32)
            onehot_t = jnp.where(pcol == tgt, 1.0, 0.0).astype(BF16)
            y = ys_ref[pl.ds(off, rows), :].astype(BF16)
            h_out_ref[...] += gcol * jnp.dot(onehot_t, y, preferred_element_type=F32)
            return carry
        lax.fori_loop(0, nch, scatter, 0)

    @pl.when((e == N_EXPERTS - 1) & (f == nf - 1))
    def _():
        hn_out_ref[...] = _rms(h_out_ref[...], ng_ref[...]).astype(hn_out_ref.dtype)


def _moe(cnt, hn, post, pos, comb, h, wg, wu, wd, ng, tm, tf, rows):
    n, d = h.shape
    dff = wg.shape[2]
    row = lambda w: pl.BlockSpec((tm, w), lambda i, e, f, cnt: (i, 0))
    grid_spec = pltpu.PrefetchScalarGridSpec(
        num_scalar_prefetch=1,
        grid=(n // tm, N_EXPERTS, dff // tf),
        in_specs=[row(d),
                  pl.BlockSpec((N_EXPERTS, tm), lambda i, e, f, cnt: (0, i)),
                  row(LANES), row(LANES), row(d),
                  pl.BlockSpec((1, d, tf), lambda i, e, f, cnt: (e, 0, f)),
                  pl.BlockSpec((1, d, tf), lambda i, e, f, cnt: (e, 0, f)),
                  pl.BlockSpec((1, tf, d), lambda i, e, f, cnt: (e, f, 0)),
                  pl.BlockSpec((1, d), lambda i, e, f, cnt: (0, 0))],
        out_specs=[row(d), row(d)],
        scratch_shapes=[pltpu.VMEM((tm, d), BF16), pltpu.VMEM((tm, d), F32)],
    )
    return pl.pallas_call(
        functools.partial(_moe_kernel, tm=tm, rows=rows),
        grid_spec=grid_spec,
        out_shape=[jax.ShapeDtypeStruct((n, d), F32), jax.ShapeDtypeStruct((n, d), BF16)],
        compiler_params=_cparams(("parallel", "arbitrary", "arbitrary")),
        name="expert_swiglu",
    )(cnt, hn, post, pos, comb, h, wg, wu, wd, ng)


def _pad_lanes(a, width=LANES):
    return jnp.pad(a, ((0, 0), (0, width - a.shape[-1])))


def kernel(x, meta, norm_mix, w_in, b_in, q_norm, k_norm, w_attn_o, conv_w, conv_b, conv_ln_g, conv_ln_b,
           w_conv_o, pool_w, pool_scale, w_pool_o, w_out, norm_ffn, w_ff_gate, w_ff_up, w_ff_down, w_router,
           b_router, w_e_gate, w_e_up, w_e_down):
    bsz, seq, d = x.shape
    depth = w_in.shape[0]
    length = seq + N_META
    lp = -(-length // SEQ_ALIGN) * SEQ_ALIGN
    n = bsz * lp

    tm = _pick(n, (1536, 768, 512, 256))
    tm_moe = _pick(n, (768, 512, 256))
    tn = _pick(Z_W, (1792, 768, 256))
    ts = _pick(lp, (768, 512, 256))
    tq = _pick(lp, (256,))
    tk = 256
    tf_dense = _pick(w_ff_gate.shape[2], (256,))
    tf_moe = _pick(w_e_gate.shape[3], (512, 256))
    rows = 256

    h = jnp.concatenate([jnp.broadcast_to(meta[None].astype(x.dtype), (bsz, N_META, d)), x], axis=1)
    h = jnp.pad(h, ((0, 0), (0, lp - length), (0, 0))).reshape(n, d)
    hn = _norm(h, norm_mix[0][None], tm)

    att_scale = HEAD_DIM ** -0.5
    gate_lo = 3 * ATT_W + ATT_HEADS + 2 * CONV_W + POOL_W
    f_lo = 3 * ATT_W
    for l in range(depth):
        wl = w_in[l]
        w_main = jnp.concatenate([wl[:, gate_lo:], wl[:, :f_lo], wl[:, f_lo + ATT_HEADS:gate_lo]], axis=1).astype(BF16)
        bl = b_in[l]
        b_main = jnp.concatenate([bl[gate_lo:], bl[:f_lo], bl[f_lo + ATT_HEADS:gate_lo]])[None]
        w_f = _pad_lanes(wl[:, f_lo:f_lo + ATT_HEADS])
        b_f = _pad_lanes(bl[None, f_lo:f_lo + ATT_HEADS])

        z = _inproj(hn, w_main, b_main, tm, tn)
        z3 = z.reshape(bsz, lp, Z_W)
        fcol = _forget(hn.reshape(bsz, lp, d), w_f, b_f, ts)
        frow = jnp.transpose(fcol[:, :, :ATT_HEADS], (0, 2, 1))
        qg = jnp.tile(q_norm[l], 2)[None] * att_scale
        kg = jnp.tile(k_norm[l], 2)[None]
        att = _attention(z3, fcol, frow, qg, kg, tq, tk)

        cw = jnp.pad(conv_w[l], ((0, HALO - CONV_K), (0, 0)))
        pw = jax.scipy.linalg.block_diag(*[pool_w[l, g] for g in range(pool_w.shape[1])]).astype(BF16)
        u, pm = _convpool(z3, cw, conv_b[l][None], conv_ln_g[l][None], conv_ln_b[l][None], pw,
                          pool_scale[l][None], ts)

        h, hn = _merge(att.reshape(n, ATT_W), u.reshape(n, CONV_W), pm.reshape(n, POOL_W), z, h,
                       w_attn_o[l].astype(BF16), w_conv_o[l].astype(BF16), w_pool_o[l].astype(BF16),
                       w_out[l].astype(BF16), norm_ffn[l][None], tm_moe)

        ng_next = norm_mix[min(l + 1, depth - 1)][None]
        i = l // 2
        if l % 2 == 0:
            h, hn = _ffn(hn, h, w_ff_gate[i].astype(BF16), w_ff_up[i].astype(BF16), w_ff_down[i].astype(BF16),
                         ng_next, tm, tf_dense)
        else:
            comb, pos, cnt = _router(h, norm_ffn[l][None], _pad_lanes(w_router[i]), _pad_lanes(b_router[i][None]),
                                     tm_moe)
            post = jnp.transpose(pos[:, :N_EXPERTS])
            cnt_flat = cnt[:, 0, :N_EXPERTS].reshape(-1)
            h, hn = _moe(cnt_flat, hn, post, pos, comb, h, w_e_gate[i].astype(BF16), w_e_up[i].astype(BF16),
                         w_e_down[i].astype(BF16), ng_next, tm_moe, tf_moe, rows)

    return h.reshape(bsz, lp, d)[:, N_META:length]
```

```python
import functools

import jax
import jax.numpy as jnp
from jax import lax
from jax.experimental import pallas as pl
from jax.experimental.pallas import tpu as pltpu

F32 = jnp.float32
BF16 = jnp.bfloat16
HIGHEST = lax.Precision.HIGHEST

D_MODEL = 1024
N_META = 16
HEAD_DIM = 64
ATT_W = 512
ATT_HEADS = 8
CONV_W = 256
CONV_K = 31
POOL_W = 256
POOL_WINDOWS = (2, 4, 8, 16)
N_EXPERTS = 8
LANES = 128
HALO = 32
NEG_INF = -1e30
EXP_UNDERFLOW = 110.0
SEQ_ALIGN = 256
VMEM_LIMIT = 56 * 1024 * 1024

Z_GATE = 0
Z_Q = 3072
Z_K = 3584
Z_V = 4096
Z_A = 4608
Z_G = 4864
Z_P = 5120
Z_W = 5376


def _pick(n, candidates):
    for c in candidates:
        if n % c == 0:
            return c
    raise ValueError(f"no tile for {n} in {candidates}")


def _cparams(sem):
    return pltpu.CompilerParams(dimension_semantics=sem, vmem_limit_bytes=VMEM_LIMIT)


def _rms(x, g, eps=1e-6):
    return x * lax.rsqrt(jnp.mean(x * x, axis=-1, keepdims=True) + eps) * g


def _sigmoid(x):
    return 1.0 / (1.0 + jnp.exp(-x))


def _silu(x):
    return x * _sigmoid(x)


def _norm_kernel(h_ref, g_ref, o_ref):
    o_ref[...] = _rms(h_ref[...], g_ref[...]).astype(o_ref.dtype)


def _norm(h, g, tm):
    n, d = h.shape
    return pl.pallas_call(
        _norm_kernel,
        grid=(n // tm,),
        in_specs=[pl.BlockSpec((tm, d), lambda i: (i, 0)), pl.BlockSpec((1, d), lambda i: (0, 0))],
        out_specs=pl.BlockSpec((tm, d), lambda i: (i, 0)),
        out_shape=jax.ShapeDtypeStruct((n, d), BF16),
        compiler_params=_cparams(("parallel",)),
        name="rmsnorm",
    )(h, g)


def _inproj_kernel(x_ref, w_ref, b_ref, o_ref):
    acc = jnp.dot(x_ref[...], w_ref[...], preferred_element_type=F32)
    o_ref[...] = (acc + b_ref[...]).astype(o_ref.dtype)


def _inproj(hn, w, b, tm, tn):
    n, d = hn.shape
    zw = w.shape[1]
    return pl.pallas_call(
        _inproj_kernel,
        grid=(n // tm, zw // tn),
        in_specs=[pl.BlockSpec((tm, d), lambda i, j: (i, 0)),
                  pl.BlockSpec((d, tn), lambda i, j: (0, j)),
                  pl.BlockSpec((1, tn), lambda i, j: (0, j))],
        out_specs=pl.BlockSpec((tm, tn), lambda i, j: (i, j)),
        out_shape=jax.ShapeDtypeStruct((n, zw), BF16),
        compiler_params=_cparams(("parallel", "arbitrary")),
        name="inproj",
    )(hn, w, b)


def _forget_kernel(x_ref, w_ref, b_ref, o_ref, carry_ref):
    @pl.when(pl.program_id(1) == 0)
    def _():
        carry_ref[...] = jnp.zeros_like(carry_ref)

    x = x_ref[0].astype(F32)
    f = jnp.dot(x, w_ref[...], precision=HIGHEST, preferred_element_type=F32) + b_ref[...]
    ls = jnp.minimum(f, 0.0) - jnp.log(1.0 + jnp.exp(-jnp.abs(f)))
    t = ls.shape[0]
    r = lax.broadcasted_iota(jnp.int32, (t, t), 0)
    c = lax.broadcasted_iota(jnp.int32, (t, t), 1)
    tril = jnp.where(c <= r, 1.0, 0.0).astype(F32)
    cs = jnp.dot(tril, ls, precision=HIGHEST, preferred_element_type=F32) + carry_ref[...]
    o_ref[0] = cs
    carry_ref[...] = cs[t - 1:t, :]


def _forget(hn3, w_f, b_f, ts):
    b, lp, d = hn3.shape
    return pl.pallas_call(
        _forget_kernel,
        grid=(b, lp // ts),
        in_specs=[pl.BlockSpec((1, ts, d), lambda bi, t: (bi, t, 0)),
                  pl.BlockSpec((d, LANES), lambda bi, t: (0, 0)),
                  pl.BlockSpec((1, LANES), lambda bi, t: (0, 0))],
        out_specs=pl.BlockSpec((1, ts, LANES), lambda bi, t: (bi, t, 0)),
        out_shape=jax.ShapeDtypeStruct((b, lp, LANES), F32),
        scratch_shapes=[pltpu.VMEM((1, LANES), F32)],
        compiler_params=_cparams(("parallel", "arbitrary")),
        name="forget_cumsum",
    )(hn3, w_f, b_f)


def _pair_rms(x, first_head, gain, eps=1e-6):
    sq = x * x
    s0 = jnp.sum(jnp.where(first_head, sq, 0.0), axis=-1, keepdims=True)
    s1 = jnp.sum(jnp.where(first_head, 0.0, sq), axis=-1, keepdims=True)
    ms = jnp.where(first_head, s0, s1) * (1.0 / HEAD_DIM)
    return x * lax.rsqrt(ms + eps) * gain


def _attn_kernel(lo_ref, q_ref, k_ref, v_ref, fc_ref, fr_ref, qg_ref, kg_ref, o_ref, kn_ref, *, tq, tk):
    bi = pl.program_id(0)
    p = pl.program_id(1)
    i = pl.program_id(2)
    nq = pl.num_programs(2)
    lp = k_ref.shape[1]
    lane = lax.broadcasted_iota(jnp.int32, (1, LANES), 1)
    first_head = lane < HEAD_DIM

    @pl.when(i == 0)
    def _():
        def body(c, carry):
            off = pl.multiple_of(c * tk, tk)
            kk = k_ref[0, pl.ds(off, tk), :].astype(F32)
            kn_ref[pl.ds(off, tk), :] = _pair_rms(kk, first_head, kg_ref[...]).astype(BF16)
            return carry
        lax.fori_loop(0, lp // tk, body, 0)

    qn = _pair_rms(q_ref[0].astype(F32), first_head, qg_ref[...])
    fc = fc_ref[0]
    row_pos = i * tq + lax.broadcasted_iota(jnp.int32, (tq, 1), 0)
    n_full = (i * tq) // tk
    n_all = ((i + 1) * tq + tk - 1) // tk

    outs = []
    for hh in range(2):
        head = 2 * p + hh
        keep = first_head if hh == 0 else jnp.logical_not(first_head)
        qh = jnp.where(keep, qn, 0.0).astype(BF16)
        ft = jnp.sum(jnp.where(lane == head, fc, 0.0), axis=-1, keepdims=True)

        def step(j, carry, masked, qh=qh, ft=ft, head=head):
            m, l, acc = carry
            off = pl.multiple_of(j * tk, tk)
            ks = kn_ref[pl.ds(off, tk), :]
            s = lax.dot_general(qh, ks, (((1,), (1,)), ((), ())), preferred_element_type=F32)
            fs = fr_ref[0, pl.ds(head, 1), pl.ds(off, tk)]
            s = s + (ft - fs)
            if masked:
                col_pos = off + lax.broadcasted_iota(jnp.int32, (1, tk), 1)
                s = jnp.where(col_pos <= row_pos, s, NEG_INF)
            m_new = jnp.maximum(m, jnp.max(s, axis=-1, keepdims=True))
            alpha = jnp.exp(m - m_new)
            pm = jnp.exp(s - m_new)
            l = alpha * l + jnp.sum(pm, axis=-1, keepdims=True)
            vs = v_ref[0, pl.ds(off, tk), :]
            acc = alpha * acc + jnp.dot(pm.astype(BF16), vs, preferred_element_type=F32)
            return m_new, l, acc

        init = (jnp.full((tq, 1), NEG_INF, F32), jnp.zeros((tq, 1), F32), jnp.zeros((tq, LANES), F32))
        first = lo_ref[(bi * ATT_HEADS + head) * nq + i]
        carry = lax.fori_loop(first, n_full, functools.partial(step, masked=False), init)
        m, l, acc = lax.fori_loop(n_full, n_all, functools.partial(step, masked=True), carry)
        outs.append(acc / l)
    o_ref[0] = jnp.where(first_head, outs[0], outs[1]).astype(o_ref.dtype)


def _first_live_chunk(frow, qg, kg, tq, tk):
    b, nh, lp = frow.shape
    qk_bound = 1.02 * HEAD_DIM * jnp.max(jnp.abs(qg)) * jnp.max(jnp.abs(kg))
    f_first = frow[:, :, 0::tq]
    f_last = frow[:, :, tk - 1::tk]
    dead = (f_first[:, :, :, None] - f_last[:, :, None, :] + 2.0 * qk_bound) < -EXP_UNDERFLOW
    n_dead = jnp.sum(dead.astype(jnp.int32), axis=-1)
    n_full = (jnp.arange(lp // tq, dtype=jnp.int32) * tq) // tk
    return jnp.minimum(n_dead, n_full[None, None, :]).reshape(-1)


def _attention(z3, fcol, frow, qg, kg, tq, tk):
    b, lp, _ = z3.shape
    npairs = ATT_HEADS // 2
    qb, kb, vb = Z_Q // LANES, Z_K // LANES, Z_V // LANES
    first = _first_live_chunk(frow, qg, kg, tq, tk)
    grid_spec = pltpu.PrefetchScalarGridSpec(
        num_scalar_prefetch=1,
        grid=(b, npairs, lp // tq),
        in_specs=[pl.BlockSpec((1, tq, LANES), lambda bi, p, i, lo: (bi, i, qb + p)),
                  pl.BlockSpec((1, lp, LANES), lambda bi, p, i, lo: (bi, 0, kb + p)),
                  pl.BlockSpec((1, lp, LANES), lambda bi, p, i, lo: (bi, 0, vb + p)),
                  pl.BlockSpec((1, tq, LANES), lambda bi, p, i, lo: (bi, i, 0)),
                  pl.BlockSpec((1, ATT_HEADS, lp), lambda bi, p, i, lo: (bi, 0, 0)),
                  pl.BlockSpec((1, LANES), lambda bi, p, i, lo: (0, 0)),
                  pl.BlockSpec((1, LANES), lambda bi, p, i, lo: (0, 0))],
        out_specs=pl.BlockSpec((1, tq, LANES), lambda bi, p, i, lo: (bi, i, p)),
        scratch_shapes=[pltpu.VMEM((lp, LANES), BF16)],
    )
    return pl.pallas_call(
        functools.partial(_attn_kernel, tq=tq, tk=tk),
        grid_spec=grid_spec,
        out_shape=jax.ShapeDtypeStruct((b, lp, ATT_W), BF16),
        compiler_params=_cparams(("parallel", "parallel", "arbitrary")),
        name="fox_attention",
    )(first, z3, z3, z3, fcol, frow, qg, kg)


def _convpool_kernel(a_ref, g_ref, p_ref, ah_ref, gh_ref, ph_ref, cw_ref, cb_ref, lg_ref, lb_ref,
                     pw_ref, ps_ref, u_ref, pm_ref, ext_ref, pext_ref, *, ts):
    i = pl.program_id(1)
    has_prev = i > 0

    u = a_ref[0].astype(F32) * _sigmoid(g_ref[0].astype(F32))
    uh = ah_ref[0].astype(F32) * _sigmoid(gh_ref[0].astype(F32))
    ext_ref[0:HALO, :] = jnp.where(has_prev, uh, 0.0)
    ext_ref[HALO:HALO + ts, :] = u
    acc = jnp.zeros((ts, CONV_W), F32) + cb_ref[...]
    for j in range(CONV_K):
        start = HALO - (CONV_K - 1) + j
        acc = acc + cw_ref[j:j + 1, :] * ext_ref[start:start + ts, :]
    mu = jnp.mean(acc, axis=-1, keepdims=True)
    cen = acc - mu
    var = jnp.mean(cen * cen, axis=-1, keepdims=True)
    y = cen * lax.rsqrt(var + 1e-5) * lg_ref[...] + lb_ref[...]
    u_ref[0] = _silu(y).astype(u_ref.dtype)

    x = p_ref[0].astype(F32)
    pext_ref[0:HALO, :] = jnp.where(has_prev, ph_ref[0].astype(F32), 0.0)
    pext_ref[HALO:HALO + ts, :] = x
    pos1 = (i * ts + 1 + lax.broadcasted_iota(jnp.int32, (ts, 1), 0)).astype(F32)
    lane = lax.broadcasted_iota(jnp.int32, (1, POOL_W), 1)
    group_w = POOL_W // len(POOL_WINDOWS)
    run = x
    pooled = jnp.zeros((ts, POOL_W), F32)
    for k in range(1, max(POOL_WINDOWS)):
        run = run + pext_ref[HALO - k:HALO - k + ts, :]
        if (k + 1) in POOL_WINDOWS:
            gi = POOL_WINDOWS.index(k + 1)
            mean = run / jnp.minimum(pos1, float(k + 1))
            in_group = (lane >= gi * group_w) & (lane < (gi + 1) * group_w)
            pooled = jnp.where(in_group, mean, pooled)
    pm = (pooled - x).astype(BF16)
    lin = jnp.dot(pm, pw_ref[...], preferred_element_type=F32) * ps_ref[...]
    pm_ref[0] = lin.astype(pm_ref.dtype)


def _convpool(z3, cw, cb, lg, lb, pw, ps, ts):
    b, lp, _ = z3.shape
    ab, gb, pb = Z_A // CONV_W, Z_G // CONV_W, Z_P // POOL_W
    hpt = ts // HALO
    main = lambda blk: pl.BlockSpec((1, ts, CONV_W), lambda bi, i: (bi, i, blk))
    halo = lambda blk: pl.BlockSpec((1, HALO, CONV_W), lambda bi, i: (bi, jnp.maximum(i * hpt - 1, 0), blk))
    const = lambda shape: pl.BlockSpec(shape, lambda bi, i: (0, 0))
    out = pl.BlockSpec((1, ts, CONV_W), lambda bi, i: (bi, i, 0))
    return pl.pallas_call(
        functools.partial(_convpool_kernel, ts=ts),
        grid=(b, lp // ts),
        in_specs=[main(ab), main(gb), main(pb), halo(ab), halo(gb), halo(pb),
                  const((HALO, CONV_W)), const((1, CONV_W)), const((1, CONV_W)), const((1, CONV_W)),
                  const((POOL_W, POOL_W)), const((1, POOL_W))],
        out_specs=[out, out],
        out_shape=[jax.ShapeDtypeStruct((b, lp, CONV_W), BF16), jax.ShapeDtypeStruct((b, lp, POOL_W), BF16)],
        scratch_shapes=[pltpu.VMEM((HALO + ts, CONV_W), F32), pltpu.VMEM((HALO + ts, POOL_W), F32)],
        compiler_params=_cparams(("parallel", "arbitrary")),
        name="conv_pool",
    )(z3, z3, z3, z3, z3, z3, cw, cb, lg, lb, pw, ps)


def _merge_kernel(att_ref, u_ref, pm_ref, g0_ref, g1_ref, g2_ref, h_ref, wa_ref, wc_ref, wp_ref, wo_ref,
                  ng_ref, h_out_ref, hn_out_ref):
    ya = jnp.dot(att_ref[...], wa_ref[...], preferred_element_type=F32)
    yc = jnp.dot(u_ref[...], wc_ref[...], preferred_element_type=F32)
    yp = jnp.dot(pm_ref[...], wp_ref[...], preferred_element_type=F32)
    m = (_sigmoid(g0_ref[...].astype(F32)) * ya + _sigmoid(g1_ref[...].astype(F32)) * yc
         + _sigmoid(g2_ref[...].astype(F32)) * yp)
    h = h_ref[...] + jnp.dot(m.astype(BF16), wo_ref[...], preferred_element_type=F32)
    h_out_ref[...] = h
    hn_out_ref[...] = _rms(h, ng_ref[...]).astype(hn_out_ref.dtype)


def _merge(att, u, pm, z, h, wa, wc, wp, wo, ng, tm):
    n, d = h.shape
    row = lambda w, blk=0: pl.BlockSpec((tm, w), lambda i: (i, blk))
    const = lambda shape: pl.BlockSpec(shape, lambda i: (0, 0))
    return pl.pallas_call(
        _merge_kernel,
        grid=(n // tm,),
        in_specs=[row(ATT_W), row(CONV_W), row(POOL_W), row(d, 0), row(d, 1), row(d, 2), row(d),
                  const(wa.shape), const(wc.shape), const(wp.shape), const(wo.shape), const((1, d))],
        out_specs=[row(d), row(d)],
        out_shape=[jax.ShapeDtypeStruct((n, d), F32), jax.ShapeDtypeStruct((n, d), BF16)],
        compiler_params=_cparams(("parallel",)),
        name="merge_outproj",
    )(att, u, pm, z, z, z, h, wa, wc, wp, wo, ng)


def _ffn_kernel(x_ref, h_ref, wg_ref, wu_ref, wd_ref, ng_ref, h_out_ref, hn_out_ref, *, tf):
    x = x_ref[...]
    h = h_ref[...]
    for c in range(wg_ref.shape[1] // tf):
        gt = jnp.dot(x, wg_ref[:, c * tf:(c + 1) * tf], preferred_element_type=F32)
        up = jnp.dot(x, wu_ref[:, c * tf:(c + 1) * tf], preferred_element_type=F32)
        h = h + jnp.dot((_silu(gt) * up).astype(BF16), wd_ref[c * tf:(c + 1) * tf, :],
                        preferred_element_type=F32)
    h_out_ref[...] = h
    hn_out_ref[...] = _rms(h, ng_ref[...]).astype(hn_out_ref.dtype)


def _ffn(hn, h, wg, wu, wd, ng, tm, tf):
    n, d = h.shape
    row = pl.BlockSpec((tm, d), lambda i: (i, 0))
    const = lambda shape: pl.BlockSpec(shape, lambda i: (0, 0))
    return pl.pallas_call(
        functools.partial(_ffn_kernel, tf=tf),
        grid=(n // tm,),
        in_specs=[row, row, const(wg.shape), const(wu.shape), const(wd.shape), const((1, d))],
        out_specs=[row, row],
        out_shape=[jax.ShapeDtypeStruct((n, d), F32), jax.ShapeDtypeStruct((n, d), BF16)],
        compiler_params=_cparams(("parallel",)),
        name="dense_swiglu",
    )(hn, h, wg, wu, wd, ng)


def _router_kernel(h_ref, ng_ref, wr_ref, br_ref, comb_ref, pos_ref, cnt_ref):
    tm = h_ref.shape[0]
    hn = _rms(h_ref[...], ng_ref[...])
    logits = jnp.dot(hn, wr_ref[...], precision=HIGHEST, preferred_element_type=F32) + br_ref[...]
    lane = lax.broadcasted_iota(jnp.int32, (1, LANES), 1).astype(F32)
    lg = jnp.where(lane < N_EXPERTS, logits, -jnp.inf)
    m1 = jnp.max(lg, axis=-1, keepdims=True)
    i1 = jnp.min(jnp.where(lg == m1, lane, float(LANES)), axis=-1, keepdims=True)
    sel1 = lane == i1
    lg2 = jnp.where(sel1, -jnp.inf, lg)
    m2 = jnp.max(lg2, axis=-1, keepdims=True)
    i2 = jnp.min(jnp.where(lg2 == m2, lane, float(LANES)), axis=-1, keepdims=True)
    sel2 = lane == i2
    e = jnp.exp(m2 - m1)
    g1 = 1.0 / (1.0 + e)
    comb_ref[...] = jnp.where(sel1, g1, 0.0) + jnp.where(sel2, e * g1, 0.0)
    sel = jnp.where(sel1 | sel2, 1.0, 0.0)
    r = lax.broadcasted_iota(jnp.int32, (tm, tm), 0)
    c = lax.broadcasted_iota(jnp.int32, (tm, tm), 1)
    tril = jnp.where(c <= r, 1.0, 0.0).astype(BF16)
    incl = jnp.dot(tril, sel.astype(BF16), preferred_element_type=F32)
    pos_ref[...] = jnp.where(sel > 0.0, incl - 1.0, -1.0)
    cnt_ref[0] = jnp.broadcast_to(incl[tm - 1:tm, :], (8, LANES)).astype(jnp.int32)


def _router(h, ng, wr, br, tm):
    n, d = h.shape
    nt = n // tm
    return pl.pallas_call(
        _router_kernel,
        grid=(nt,),
        in_specs=[pl.BlockSpec((tm, d), lambda i: (i, 0)),
                  pl.BlockSpec((1, d), lambda i: (0, 0)),
                  pl.BlockSpec((d, LANES), lambda i: (0, 0)),
                  pl.BlockSpec((1, LANES), lambda i: (0, 0))],
        out_specs=[pl.BlockSpec((tm, LANES), lambda i: (i, 0)),
                   pl.BlockSpec((tm, LANES), lambda i: (i, 0)),
                   pl.BlockSpec((1, 8, LANES), lambda i: (i, 0, 0))],
        out_shape=[jax.ShapeDtypeStruct((n, LANES), F32), jax.ShapeDtypeStruct((n, LANES), F32),
                   jax.ShapeDtypeStruct((nt, 8, LANES), jnp.int32)],
        compiler_params=_cparams(("parallel",)),
        name="router_top2",
    )(h, ng, wr, br)


def _moe_kernel(cnt_ref, x_ref, post_ref, pos_ref, comb_ref, wg_ref, wu_ref, wd_ref,
                out_ref, xs_ref, ys_ref, *, rows):
    i = pl.program_id(0)
    e = pl.program_id(1)
    f = pl.program_id(2)
    nf = pl.num_programs(2)
    nch = (cnt_ref[i * N_EXPERTS + e] + rows - 1) // rows

    @pl.when((e == 0) & (f == 0))
    def _():
        out_ref[...] = jnp.zeros_like(out_ref)

    @pl.when(f == 0)
    def _():
        prow = post_ref[pl.ds(e, 1), :]
        def gather(c, carry):
            off = pl.multiple_of(c * rows, rows)
            tgt = (off + lax.broadcasted_iota(jnp.int32, (rows, 1), 0)).astype(F32)
            onehot = jnp.where(prow == tgt, 1.0, 0.0).astype(BF16)
            xs_ref[pl.ds(off, rows), :] = jnp.dot(onehot, x_ref[...], preferred_element_type=F32).astype(BF16)
            ys_ref[pl.ds(off, rows), :] = jnp.zeros((rows, D_MODEL), F32)
            return carry
        lax.fori_loop(0, nch, gather, 0)

    def expert(c, carry):
        off = pl.multiple_of(c * rows, rows)
        xc = xs_ref[pl.ds(off, rows), :]
        gt = jnp.dot(xc, wg_ref[0, 0], preferred_element_type=F32)
        up = jnp.dot(xc, wu_ref[0, 0], preferred_element_type=F32)
        ys_ref[pl.ds(off, rows), :] += jnp.dot((_silu(gt) * up).astype(BF16), wd_ref[0, 0],
                                                preferred_element_type=F32)
        return carry
    lax.fori_loop(0, nch, expert, 0)

    @pl.when(f == nf - 1)
    def _():
        lane = lax.broadcasted_iota(jnp.int32, (1, LANES), 1)
        pcol = jnp.sum(jnp.where(lane == e, pos_ref[...], 0.0), axis=-1, keepdims=True)
        gcol = jnp.sum(jnp.where(lane == e, comb_ref[...], 0.0), axis=-1, keepdims=True)
        def scatter(c, carry):
            off = pl.multiple_of(c * rows, rows)
            tgt = (off + lax.broadcasted_iota(jnp.int32, (1, rows), 1)).astype(F32)
            onehot_t = jnp.where(pcol == tgt, 1.0, 0.0).astype(BF16)
            y = ys_ref[pl.ds(off, rows), :].astype(BF16)
            out_ref[...] += gcol * jnp.dot(onehot_t, y, preferred_element_type=F32)
            return carry
        lax.fori_loop(0, nch, scatter, 0)


def _moe(cnt, hn, post, pos, comb, wg, wu, wd, tm, rows):
    n, d = hn.shape
    nf, tf = wg.shape[1], wg.shape[3]
    cap = -(-tm // rows) * rows
    row = lambda w: pl.BlockSpec((tm, w), lambda i, e, f, cnt: (i, 0))
    grid_spec = pltpu.PrefetchScalarGridSpec(
        num_scalar_prefetch=1,
        grid=(n // tm, N_EXPERTS, nf),
        in_specs=[row(d),
                  pl.BlockSpec((N_EXPERTS, tm), lambda i, e, f, cnt: (0, i)),
                  row(LANES), row(LANES),
                  pl.BlockSpec((1, 1, d, tf), lambda i, e, f, cnt: (e, f, 0, 0)),
                  pl.BlockSpec((1, 1, d, tf), lambda i, e, f, cnt: (e, f, 0, 0)),
                  pl.BlockSpec((1, 1, tf, d), lambda i, e, f, cnt: (e, f, 0, 0))],
        out_specs=row(d),
        scratch_shapes=[pltpu.VMEM((cap, d), BF16), pltpu.VMEM((cap, d), F32)],
    )
    return pl.pallas_call(
        functools.partial(_moe_kernel, rows=rows),
        grid_spec=grid_spec,
        out_shape=jax.ShapeDtypeStruct((n, d), F32),
        compiler_params=_cparams(("parallel", "arbitrary", "arbitrary")),
        name="expert_swiglu",
    )(cnt, hn, post, pos, comb, wg, wu, wd)


def _add_norm_kernel(h_ref, d_ref, g_ref, h_out_ref, hn_out_ref):
    h = h_ref[...] + d_ref[...]
    h_out_ref[...] = h
    hn_out_ref[...] = _rms(h, g_ref[...]).astype(hn_out_ref.dtype)


def _add_norm(h, delta, g, tm):
    n, d = h.shape
    row = pl.BlockSpec((tm, d), lambda i: (i, 0))
    return pl.pallas_call(
        _add_norm_kernel,
        grid=(n // tm,),
        in_specs=[row, row, pl.BlockSpec((1, d), lambda i: (0, 0))],
        out_specs=[row, row],
        out_shape=[jax.ShapeDtypeStruct((n, d), F32), jax.ShapeDtypeStruct((n, d), BF16)],
        compiler_params=_cparams(("parallel",)),
        name="residual_norm",
    )(h, delta, g)


def _pad_lanes(a, width=LANES):
    return jnp.pad(a, ((0, 0), (0, width - a.shape[-1])))


def kernel(x, meta, norm_mix, w_in, b_in, q_norm, k_norm, w_attn_o, conv_w, conv_b, conv_ln_g, conv_ln_b,
           w_conv_o, pool_w, pool_scale, w_pool_o, w_out, norm_ffn, w_ff_gate, w_ff_up, w_ff_down, w_router,
           b_router, w_e_gate, w_e_up, w_e_down):
    bsz, seq, d = x.shape
    depth = w_in.shape[0]
    length = seq + N_META
    lp = -(-length // SEQ_ALIGN) * SEQ_ALIGN
    n = bsz * lp

    tm = _pick(n, (1536, 768, 512, 256))
    tm_mid = _pick(n, (768, 512, 256))
    tm_ffn = _pick(n, (512, 256))
    tm_moe = _pick(n, (1408, 768, 512, 256))
    tn = _pick(Z_W, (1792, 768, 256))
    ts = _pick(lp, (768, 512, 256))
    ts_f = _pick(lp, (256,))
    tq = _pick(lp, (256,))
    tk = 256
    tf_dense = _pick(w_ff_gate.shape[2], (256,))
    tf_moe = _pick(w_e_gate.shape[3], (896, 512, 256))
    rows = {1408: 384}.get(tm_moe, 256)

    h = jnp.concatenate([jnp.broadcast_to(meta[None].astype(x.dtype), (bsz, N_META, d)), x], axis=1)
    h = jnp.pad(h, ((0, 0), (0, lp - length), (0, 0))).reshape(n, d)
    hn = _norm(h, norm_mix[0][None], tm)

    att_scale = HEAD_DIM ** -0.5
    gate_lo = 3 * ATT_W + ATT_HEADS + 2 * CONV_W + POOL_W
    f_lo = 3 * ATT_W
    for l in range(depth):
        wl = w_in[l]
        w_main = jnp.concatenate([wl[:, gate_lo:], wl[:, :f_lo], wl[:, f_lo + ATT_HEADS:gate_lo]], axis=1).astype(BF16)
        bl = b_in[l]
        b_main = jnp.concatenate([bl[gate_lo:], bl[:f_lo], bl[f_lo + ATT_HEADS:gate_lo]])[None]
        w_f = _pad_lanes(wl[:, f_lo:f_lo + ATT_HEADS])
        b_f = _pad_lanes(bl[None, f_lo:f_lo + ATT_HEADS])

        z = _inproj(hn, w_main, b_main, tm, tn)
        z3 = z.reshape(bsz, lp, Z_W)
        fcol = _forget(hn.reshape(bsz, lp, d), w_f, b_f, ts_f)
        frow = jnp.transpose(fcol[:, :, :ATT_HEADS], (0, 2, 1))
        qg = jnp.tile(q_norm[l], 2)[None] * att_scale
        kg = jnp.tile(k_norm[l], 2)[None]
        att = _attention(z3, fcol, frow, qg, kg, tq, tk)

        cw = jnp.pad(conv_w[l], ((0, HALO - CONV_K), (0, 0)))
        pw = jax.scipy.linalg.block_diag(*[pool_w[l, g] for g in range(pool_w.shape[1])]).astype(BF16)
        u, pm = _convpool(z3, cw, conv_b[l][None], conv_ln_g[l][None], conv_ln_b[l][None], pw,
                          pool_scale[l][None], ts)

        h, hn = _merge(att.reshape(n, ATT_W), u.reshape(n, CONV_W), pm.reshape(n, POOL_W), z, h,
                       w_attn_o[l].astype(BF16), w_conv_o[l].astype(BF16), w_pool_o[l].astype(BF16),
                       w_out[l].astype(BF16), norm_ffn[l][None], tm_mid)

        ng_next = norm_mix[min(l + 1, depth - 1)][None]
        i = l // 2
        if l % 2 == 0:
            h, hn = _ffn(hn, h, w_ff_gate[i].astype(BF16), w_ff_up[i].astype(BF16), w_ff_down[i].astype(BF16),
                         ng_next, tm_ffn, tf_dense)
        else:
            comb, pos, cnt = _router(h, norm_ffn[l][None], _pad_lanes(w_router[i]), _pad_lanes(b_router[i][None]),
                                     tm_moe)
            post = jnp.transpose(pos[:, :N_EXPERTS])
            cnt_flat = cnt[:, 0, :N_EXPERTS].reshape(-1)
            ne, _, dffe = w_e_gate[i].shape
            nf = dffe // tf_moe
            wg = w_e_gate[i].reshape(ne, d, nf, tf_moe).transpose(0, 2, 1, 3).astype(BF16)
            wu = w_e_up[i].reshape(ne, d, nf, tf_moe).transpose(0, 2, 1, 3).astype(BF16)
            wd = w_e_down[i].reshape(ne, nf, tf_moe, d).astype(BF16)
            delta = _moe(cnt_flat, hn, post, pos, comb, wg, wu, wd, tm_moe, rows)
            h, hn = _add_norm(h, delta, ng_next, tm)

    return h.reshape(bsz, lp, d)[:, N_META:length]
```

```python
import functools

import jax
import jax.numpy as jnp
from jax import lax
from jax.experimental import pallas as pl
from jax.experimental.pallas import tpu as pltpu

F32 = jnp.float32
BF16 = jnp.bfloat16
HIGHEST = lax.Precision.HIGHEST

D_MODEL = 1024
N_META = 16
HEAD_DIM = 64
ATT_W = 512
ATT_HEADS = 8
CONV_W = 256
CONV_K = 31
POOL_W = 256
POOL_WINDOWS = (2, 4, 8, 16)
N_EXPERTS = 8
LANES = 128
HALO = 32
NEG_INF = -1e30
EXP_UNDERFLOW = 110.0
SEQ_ALIGN = 256
VMEM_LIMIT = 56 * 1024 * 1024

Z_GATE = 0
Z_Q = 3072
Z_K = 3584
Z_V = 4096
Z_A = 4608
Z_G = 4864
Z_P = 5120
Z_W = 5376


def _pick(n, candidates):
    for c in candidates:
        if n % c == 0:
            return c
    raise ValueError(f"no tile for {n} in {candidates}")


def _cparams(sem):
    return pltpu.CompilerParams(dimension_semantics=sem, vmem_limit_bytes=VMEM_LIMIT)


def _rms(x, g, eps=1e-6):
    return x * lax.rsqrt(jnp.mean(x * x, axis=-1, keepdims=True) + eps) * g


def _sigmoid(x):
    return 1.0 / (1.0 + jnp.exp(-x))


def _silu(x):
    return x * _sigmoid(x)


def _norm_kernel(h_ref, g_ref, o_ref):
    o_ref[...] = _rms(h_ref[...], g_ref[...]).astype(o_ref.dtype)


def _norm(h, g, tm):
    n, d = h.shape
    return pl.pallas_call(
        _norm_kernel,
        grid=(n // tm,),
        in_specs=[pl.BlockSpec((tm, d), lambda i: (i, 0)), pl.BlockSpec((1, d), lambda i: (0, 0))],
        out_specs=pl.BlockSpec((tm, d), lambda i: (i, 0)),
        out_shape=jax.ShapeDtypeStruct((n, d), BF16),
        compiler_params=_cparams(("parallel",)),
        name="rmsnorm",
    )(h, g)


def _inproj_kernel(x_ref, w_ref, b_ref, o_ref):
    acc = jnp.dot(x_ref[...], w_ref[...], preferred_element_type=F32)
    o_ref[...] = (acc + b_ref[...]).astype(o_ref.dtype)


def _inproj(hn, w, b, tm, tn):
    n, d = hn.shape
    zw = w.shape[1]
    return pl.pallas_call(
        _inproj_kernel,
        grid=(n // tm, zw // tn),
        in_specs=[pl.BlockSpec((tm, d), lambda i, j: (i, 0)),
                  pl.BlockSpec((d, tn), lambda i, j: (0, j)),
                  pl.BlockSpec((1, tn), lambda i, j: (0, j))],
        out_specs=pl.BlockSpec((tm, tn), lambda i, j: (i, j)),
        out_shape=jax.ShapeDtypeStruct((n, zw), BF16),
        compiler_params=_cparams(("parallel", "arbitrary")),
        name="inproj",
    )(hn, w, b)


def _forget_kernel(x_ref, w_ref, b_ref, o_ref, carry_ref):
    @pl.when(pl.program_id(1) == 0)
    def _():
        carry_ref[...] = jnp.zeros_like(carry_ref)

    fw = jnp.dot(x_ref[0], w_ref[...], preferred_element_type=F32)
    f = fw[:, :LANES] + fw[:, LANES:] + b_ref[...]
    ls = jnp.minimum(f, 0.0) - jnp.log(1.0 + jnp.exp(-jnp.abs(f)))
    hi = ls.astype(BF16)
    r1 = ls - hi.astype(F32)
    mid = r1.astype(BF16)
    lo = (r1 - mid.astype(F32)).astype(BF16)
    t = ls.shape[0]
    r = lax.broadcasted_iota(jnp.int32, (t, t), 0)
    c = lax.broadcasted_iota(jnp.int32, (t, t), 1)
    tril = jnp.where(c <= r, 1.0, 0.0).astype(BF16)
    parts = jnp.dot(tril, jnp.concatenate([hi, mid, lo], axis=-1), preferred_element_type=F32)
    cs = (parts[:, :LANES] + parts[:, LANES:2 * LANES]) + parts[:, 2 * LANES:] + carry_ref[...]
    o_ref[0] = cs
    carry_ref[...] = cs[t - 1:t, :]


def _forget(hn3, w_f, b_f, ts):
    b, lp, d = hn3.shape
    return pl.pallas_call(
        _forget_kernel,
        grid=(b, lp // ts),
        in_specs=[pl.BlockSpec((1, ts, d), lambda bi, t: (bi, t, 0)),
                  pl.BlockSpec((d, 2 * LANES), lambda bi, t: (0, 0)),
                  pl.BlockSpec((1, LANES), lambda bi, t: (0, 0))],
        out_specs=pl.BlockSpec((1, ts, LANES), lambda bi, t: (bi, t, 0)),
        out_shape=jax.ShapeDtypeStruct((b, lp, LANES), F32),
        scratch_shapes=[pltpu.VMEM((1, LANES), F32)],
        compiler_params=_cparams(("parallel", "arbitrary")),
        name="forget_cumsum",
    )(hn3, w_f, b_f)


def _pair_rms(x, first_head, gain, eps=1e-6):
    sq = x * x
    s0 = jnp.sum(jnp.where(first_head, sq, 0.0), axis=-1, keepdims=True)
    s1 = jnp.sum(jnp.where(first_head, 0.0, sq), axis=-1, keepdims=True)
    ms = jnp.where(first_head, s0, s1) * (1.0 / HEAD_DIM)
    return x * lax.rsqrt(ms + eps) * gain


def _attn_kernel(lo_ref, q_ref, k_ref, v_ref, fc_ref, fr_ref, qg_ref, kg_ref, o_ref, kn_ref, *, tq, tk):
    bi = pl.program_id(0)
    p = pl.program_id(1)
    i = pl.program_id(2)
    nq = pl.num_programs(2)
    lp = k_ref.shape[1]
    lane = lax.broadcasted_iota(jnp.int32, (1, LANES), 1)
    first_head = lane < HEAD_DIM

    @pl.when(i == 0)
    def _():
        def body(c, carry):
            off = pl.multiple_of(c * tk, tk)
            kk = k_ref[0, pl.ds(off, tk), :].astype(F32)
            kn_ref[pl.ds(off, tk), :] = _pair_rms(kk, first_head, kg_ref[...]).astype(BF16)
            return carry
        lax.fori_loop(0, lp // tk, body, 0)

    qn = _pair_rms(q_ref[0].astype(F32), first_head, qg_ref[...])
    fc = fc_ref[0]
    row_pos = i * tq + lax.broadcasted_iota(jnp.int32, (tq, 1), 0)

    heads = [2 * p, 2 * p + 1]
    qhs = [jnp.where(first_head, qn, 0.0).astype(BF16), jnp.where(first_head, 0.0, qn).astype(BF16)]
    fts = [jnp.sum(jnp.where(lane == hd, fc, 0.0), axis=-1, keepdims=True) for hd in heads]

    def update(off, width, carry, hh, masked):
        m, l, acc = carry
        ks = kn_ref[pl.ds(off, width), :]
        s = lax.dot_general(qhs[hh], ks, (((1,), (1,)), ((), ())), preferred_element_type=F32)
        fs = fr_ref[0, pl.ds(heads[hh], 1), pl.ds(off, width)]
        s = s + (fts[hh] - fs)
        if masked:
            col_pos = off + lax.broadcasted_iota(jnp.int32, (1, width), 1)
            s = jnp.where(col_pos <= row_pos, s, NEG_INF)
        m_new = jnp.maximum(m, jnp.max(s, axis=-1, keepdims=True))
        alpha = jnp.exp(m - m_new)
        pm = jnp.exp(s - m_new)
        l = alpha * l + jnp.sum(pm, axis=-1, keepdims=True)
        vs = v_ref[0, pl.ds(off, width), :]
        acc = alpha * acc + jnp.dot(pm.astype(BF16), vs, preferred_element_type=F32)
        return m_new, l, acc

    win = jnp.maximum(i - 1, 0)
    firsts = [lo_ref[(bi * ATT_HEADS + hd) * nq + i] for hd in heads]
    init = (jnp.full((tq, 1), NEG_INF, F32), jnp.zeros((tq, 1), F32), jnp.zeros((tq, LANES), F32))

    def far(j, carry):
        off = pl.multiple_of(j * tk, tk)
        return tuple(update(off, tk, carry[hh], hh, masked=False) for hh in range(2))

    carry = lax.fori_loop(jnp.minimum(firsts[0], firsts[1]), win, far, (init, init))
    off = pl.multiple_of(win * tk, tk)
    outs = []
    for hh in range(2):
        m, l, acc = update(off, 2 * tk, carry[hh], hh, masked=True)
        outs.append(acc / l)
    o_ref[0] = jnp.where(first_head, outs[0], outs[1]).astype(o_ref.dtype)


def _first_live_chunk(frow, qg, kg, tq, tk):
    b, nh, lp = frow.shape
    qk_bound = 1.02 * HEAD_DIM * jnp.max(jnp.abs(qg)) * jnp.max(jnp.abs(kg))
    f_first = frow[:, :, 0::tq]
    f_last = frow[:, :, tk - 1::tk]
    dead = (f_first[:, :, :, None] - f_last[:, :, None, :] + 2.0 * qk_bound) < -EXP_UNDERFLOW
    n_dead = jnp.sum(dead.astype(jnp.int32), axis=-1)
    n_full = (jnp.arange(lp // tq, dtype=jnp.int32) * tq) // tk
    return jnp.minimum(n_dead, n_full[None, None, :]).reshape(-1)


def _attention(z3, fcol, frow, qg, kg, tq, tk):
    b, lp, _ = z3.shape
    npairs = ATT_HEADS // 2
    qb, kb, vb = Z_Q // LANES, Z_K // LANES, Z_V // LANES
    assert tq == tk and lp >= 2 * tk, (tq, tk, lp)
    first = _first_live_chunk(frow, qg, kg, tq, tk)
    grid_spec = pltpu.PrefetchScalarGridSpec(
        num_scalar_prefetch=1,
        grid=(b, npairs, lp // tq),
        in_specs=[pl.BlockSpec((1, tq, LANES), lambda bi, p, i, lo: (bi, i, qb + p)),
                  pl.BlockSpec((1, lp, LANES), lambda bi, p, i, lo: (bi, 0, kb + p)),
                  pl.BlockSpec((1, lp, LANES), lambda bi, p, i, lo: (bi, 0, vb + p)),
                  pl.BlockSpec((1, tq, LANES), lambda bi, p, i, lo: (bi, i, 0)),
                  pl.BlockSpec((1, ATT_HEADS, lp), lambda bi, p, i, lo: (bi, 0, 0)),
                  pl.BlockSpec((1, LANES), lambda bi, p, i, lo: (0, 0)),
                  pl.BlockSpec((1, LANES), lambda bi, p, i, lo: (0, 0))],
        out_specs=pl.BlockSpec((1, tq, LANES), lambda bi, p, i, lo: (bi, i, p)),
        scratch_shapes=[pltpu.VMEM((lp, LANES), BF16)],
    )
    return pl.pallas_call(
        functools.partial(_attn_kernel, tq=tq, tk=tk),
        grid_spec=grid_spec,
        out_shape=jax.ShapeDtypeStruct((b, lp, ATT_W), BF16),
        compiler_params=_cparams(("parallel", "parallel", "arbitrary")),
        name="fox_attention",
    )(first, z3, z3, z3, fcol, frow, qg, kg)


def _convpool_kernel(a_ref, g_ref, p_ref, ah_ref, gh_ref, ph_ref, cw_ref, cb_ref, lg_ref, lb_ref,
                     pw_ref, ps_ref, u_ref, pm_ref, ext_ref, pext_ref, *, ts):
    i = pl.program_id(1)
    has_prev = i > 0

    u = a_ref[0].astype(F32) * _sigmoid(g_ref[0].astype(F32))
    uh = ah_ref[0].astype(F32) * _sigmoid(gh_ref[0].astype(F32))
    ext_ref[0:HALO, :] = jnp.where(has_prev, uh, 0.0)
    ext_ref[HALO:HALO + ts, :] = u
    acc = jnp.zeros((ts, CONV_W), F32) + cb_ref[...]
    for j in range(CONV_K):
        start = HALO - (CONV_K - 1) + j
        acc = acc + cw_ref[j:j + 1, :] * ext_ref[start:start + ts, :]
    mu = jnp.mean(acc, axis=-1, keepdims=True)
    cen = acc - mu
    var = jnp.mean(cen * cen, axis=-1, keepdims=True)
    y = cen * lax.rsqrt(var + 1e-5) * lg_ref[...] + lb_ref[...]
    u_ref[0] = _silu(y).astype(u_ref.dtype)

    x = p_ref[0].astype(F32)
    pext_ref[0:HALO, :] = jnp.where(has_prev, ph_ref[0].astype(F32), 0.0)
    pext_ref[HALO:HALO + ts, :] = x
    pos1 = (i * ts + 1 + lax.broadcasted_iota(jnp.int32, (ts, 1), 0)).astype(F32)
    lane = lax.broadcasted_iota(jnp.int32, (1, POOL_W), 1)
    group_w = POOL_W // len(POOL_WINDOWS)
    run = x
    pooled = jnp.zeros((ts, POOL_W), F32)
    for k in range(1, max(POOL_WINDOWS)):
        run = run + pext_ref[HALO - k:HALO - k + ts, :]
        if (k + 1) in POOL_WINDOWS:
            gi = POOL_WINDOWS.index(k + 1)
            mean = run / jnp.minimum(pos1, float(k + 1))
            in_group = (lane >= gi * group_w) & (lane < (gi + 1) * group_w)
            pooled = jnp.where(in_group, mean, pooled)
    pm = (pooled - x).astype(BF16)
    lin = jnp.dot(pm, pw_ref[...], preferred_element_type=F32) * ps_ref[...]
    pm_ref[0] = lin.astype(pm_ref.dtype)


def _convpool(z3, cw, cb, lg, lb, pw, ps, ts):
    b, lp, _ = z3.shape
    ab, gb, pb = Z_A // CONV_W, Z_G // CONV_W, Z_P // POOL_W
    hpt = ts // HALO
    main = lambda blk: pl.BlockSpec((1, ts, CONV_W), lambda bi, i: (bi, i, blk))
    halo = lambda blk: pl.BlockSpec((1, HALO, CONV_W), lambda bi, i: (bi, jnp.maximum(i * hpt - 1, 0), blk))
    const = lambda shape: pl.BlockSpec(shape, lambda bi, i: (0, 0))
    out = pl.BlockSpec((1, ts, CONV_W), lambda bi, i: (bi, i, 0))
    return pl.pallas_call(
        functools.partial(_convpool_kernel, ts=ts),
        grid=(b, lp // ts),
        in_specs=[main(ab), main(gb), main(pb), halo(ab), halo(gb), halo(pb),
                  const((HALO, CONV_W)), const((1, CONV_W)), const((1, CONV_W)), const((1, CONV_W)),
                  const((POOL_W, POOL_W)), const((1, POOL_W))],
        out_specs=[out, out],
        out_shape=[jax.ShapeDtypeStruct((b, lp, CONV_W), BF16), jax.ShapeDtypeStruct((b, lp, POOL_W), BF16)],
        scratch_shapes=[pltpu.VMEM((HALO + ts, CONV_W), F32), pltpu.VMEM((HALO + ts, POOL_W), F32)],
        compiler_params=_cparams(("parallel", "arbitrary")),
        name="conv_pool",
    )(z3, z3, z3, z3, z3, z3, cw, cb, lg, lb, pw, ps)


def _merge_kernel(att_ref, u_ref, pm_ref, g0_ref, g1_ref, g2_ref, h_ref, wa_ref, wc_ref, wp_ref, wo_ref,
                  ng_ref, h_out_ref, hn_out_ref):
    ya = jnp.dot(att_ref[...], wa_ref[...], preferred_element_type=F32)
    yc = jnp.dot(u_ref[...], wc_ref[...], preferred_element_type=F32)
    yp = jnp.dot(pm_ref[...], wp_ref[...], preferred_element_type=F32)
    m = (_sigmoid(g0_ref[...].astype(F32)) * ya + _sigmoid(g1_ref[...].astype(F32)) * yc
         + _sigmoid(g2_ref[...].astype(F32)) * yp)
    h = h_ref[...] + jnp.dot(m.astype(BF16), wo_ref[...], preferred_element_type=F32)
    h_out_ref[...] = h
    hn_out_ref[...] = _rms(h, ng_ref[...]).astype(hn_out_ref.dtype)


def _merge(att, u, pm, z, h, wa, wc, wp, wo, ng, tm):
    n, d = h.shape
    row = lambda w, blk=0: pl.BlockSpec((tm, w), lambda i: (i, blk))
    const = lambda shape: pl.BlockSpec(shape, lambda i: (0, 0))
    return pl.pallas_call(
        _merge_kernel,
        grid=(n // tm,),
        in_specs=[row(ATT_W), row(CONV_W), row(POOL_W), row(d, 0), row(d, 1), row(d, 2), row(d),
                  const(wa.shape), const(wc.shape), const(wp.shape), const(wo.shape), const((1, d))],
        out_specs=[row(d), row(d)],
        out_shape=[jax.ShapeDtypeStruct((n, d), F32), jax.ShapeDtypeStruct((n, d), BF16)],
        compiler_params=_cparams(("parallel",)),
        name="merge_outproj",
    )(att, u, pm, z, z, z, h, wa, wc, wp, wo, ng)


def _ffn_kernel(x_ref, h_ref, wg_ref, wu_ref, wd_ref, ng_ref, h_out_ref, hn_out_ref, *, tf):
    x = x_ref[...]
    h = h_ref[...]
    for c in range(wg_ref.shape[1] // tf):
        gt = jnp.dot(x, wg_ref[:, c * tf:(c + 1) * tf], preferred_element_type=F32)
        up = jnp.dot(x, wu_ref[:, c * tf:(c + 1) * tf], preferred_element_type=F32)
        h = h + jnp.dot((_silu(gt) * up).astype(BF16), wd_ref[c * tf:(c + 1) * tf, :],
                        preferred_element_type=F32)
    h_out_ref[...] = h
    hn_out_ref[...] = _rms(h, ng_ref[...]).astype(hn_out_ref.dtype)


def _ffn(hn, h, wg, wu, wd, ng, tm, tf):
    n, d = h.shape
    row = pl.BlockSpec((tm, d), lambda i: (i, 0))
    const = lambda shape: pl.BlockSpec(shape, lambda i: (0, 0))
    return pl.pallas_call(
        functools.partial(_ffn_kernel, tf=tf),
        grid=(n // tm,),
        in_specs=[row, row, const(wg.shape), const(wu.shape), const(wd.shape), const((1, d))],
        out_specs=[row, row],
        out_shape=[jax.ShapeDtypeStruct((n, d), F32), jax.ShapeDtypeStruct((n, d), BF16)],
        compiler_params=_cparams(("parallel",)),
        name="dense_swiglu",
    )(hn, h, wg, wu, wd, ng)


def _router_kernel(h_ref, ng_ref, wr_ref, br_ref, comb_ref, pos_ref, cnt_ref):
    tm = h_ref.shape[0]
    hn = _rms(h_ref[...], ng_ref[...])
    logits = jnp.dot(hn, wr_ref[...], precision=HIGHEST, preferred_element_type=F32) + br_ref[...]
    lane = lax.broadcasted_iota(jnp.int32, (1, LANES), 1).astype(F32)
    lg = jnp.where(lane < N_EXPERTS, logits, -jnp.inf)
    m1 = jnp.max(lg, axis=-1, keepdims=True)
    i1 = jnp.min(jnp.where(lg == m1, lane, float(LANES)), axis=-1, keepdims=True)
    sel1 = lane == i1
    lg2 = jnp.where(sel1, -jnp.inf, lg)
    m2 = jnp.max(lg2, axis=-1, keepdims=True)
    i2 = jnp.min(jnp.where(lg2 == m2, lane, float(LANES)), axis=-1, keepdims=True)
    sel2 = lane == i2
    e = jnp.exp(m2 - m1)
    g1 = 1.0 / (1.0 + e)
    comb_ref[...] = jnp.where(sel1, g1, 0.0) + jnp.where(sel2, e * g1, 0.0)
    sel = jnp.where(sel1 | sel2, 1.0, 0.0)
    r = lax.broadcasted_iota(jnp.int32, (tm, tm), 0)
    c = lax.broadcasted_iota(jnp.int32, (tm, tm), 1)
    tril = jnp.where(c <= r, 1.0, 0.0).astype(BF16)
    incl = jnp.dot(tril, sel.astype(BF16), preferred_element_type=F32)
    pos_ref[...] = jnp.where(sel > 0.0, incl - 1.0, -1.0)
    cnt_ref[0] = jnp.broadcast_to(incl[tm - 1:tm, :], (8, LANES)).astype(jnp.int32)


def _router(h, ng, wr, br, tm):
    n, d = h.shape
    nt = n // tm
    return pl.pallas_call(
        _router_kernel,
        grid=(nt,),
        in_specs=[pl.BlockSpec((tm, d), lambda i: (i, 0)),
                  pl.BlockSpec((1, d), lambda i: (0, 0)),
                  pl.BlockSpec((d, LANES), lambda i: (0, 0)),
                  pl.BlockSpec((1, LANES), lambda i: (0, 0))],
        out_specs=[pl.BlockSpec((tm, LANES), lambda i: (i, 0)),
                   pl.BlockSpec((tm, LANES), lambda i: (i, 0)),
                   pl.BlockSpec((1, 8, LANES), lambda i: (i, 0, 0))],
        out_shape=[jax.ShapeDtypeStruct((n, LANES), F32), jax.ShapeDtypeStruct((n, LANES), F32),
                   jax.ShapeDtypeStruct((nt, 8, LANES), jnp.int32)],
        compiler_params=_cparams(("parallel",)),
        name="router_top2",
    )(h, ng, wr, br)


def _moe_kernel(cnt_ref, x_ref, post_ref, pos_ref, comb_ref, wg_ref, wu_ref, wd_ref,
                out_ref, xs_ref, ys_ref, *, rows):
    i = pl.program_id(0)
    e = pl.program_id(1)
    f = pl.program_id(2)
    nf = pl.num_programs(2)
    nch = (cnt_ref[i * N_EXPERTS + e] + rows - 1) // rows

    @pl.when((e == 0) & (f == 0))
    def _():
        out_ref[...] = jnp.zeros_like(out_ref)

    @pl.when(f == 0)
    def _():
        prow = post_ref[pl.ds(e, 1), :]
        def gather(c, carry):
            off = pl.multiple_of(c * rows, rows)
            tgt = (off + lax.broadcasted_iota(jnp.int32, (rows, 1), 0)).astype(F32)
            onehot = jnp.where(prow == tgt, 1.0, 0.0).astype(BF16)
            xs_ref[pl.ds(off, rows), :] = jnp.dot(onehot, x_ref[...], preferred_element_type=F32).astype(BF16)
            ys_ref[pl.ds(off, rows), :] = jnp.zeros((rows, D_MODEL), F32)
            return carry
        lax.fori_loop(0, nch, gather, 0)

    def expert(c, carry):
        off = pl.multiple_of(c * rows, rows)
        xc = xs_ref[pl.ds(off, rows), :]
        gt = jnp.dot(xc, wg_ref[0, 0], preferred_element_type=F32)
        up = jnp.dot(xc, wu_ref[0, 0], preferred_element_type=F32)
        ys_ref[pl.ds(off, rows), :] += jnp.dot((_silu(gt) * up).astype(BF16), wd_ref[0, 0],
                                                preferred_element_type=F32)
        return carry
    lax.fori_loop(0, nch, expert, 0)

    @pl.when(f == nf - 1)
    def _():
        lane = lax.broadcasted_iota(jnp.int32, (1, LANES), 1)
        pcol = jnp.sum(jnp.where(lane == e, pos_ref[...], 0.0), axis=-1, keepdims=True)
        gcol = jnp.sum(jnp.where(lane == e, comb_ref[...], 0.0), axis=-1, keepdims=True)
        def scatter(c, carry):
            off = pl.multiple_of(c * rows, rows)
            tgt = (off + lax.broadcasted_iota(jnp.int32, (1, rows), 1)).astype(F32)
            onehot_t = jnp.where(pcol == tgt, 1.0, 0.0).astype(BF16)
            y = ys_ref[pl.ds(off, rows), :].astype(BF16)
            out_ref[...] += gcol * jnp.dot(onehot_t, y, preferred_element_type=F32)
            return carry
        lax.fori_loop(0, nch, scatter, 0)


def _moe(cnt, hn, post, pos, comb, wg, wu, wd, tm, rows):
    n, d = hn.shape
    nf, tf = wg.shape[1], wg.shape[3]
    cap = -(-tm // rows) * rows
    row = lambda w: pl.BlockSpec((tm, w), lambda i, e, f, cnt: (i, 0))
    grid_spec = pltpu.PrefetchScalarGridSpec(
        num_scalar_prefetch=1,
        grid=(n // tm, N_EXPERTS, nf),
        in_specs=[row(d),
                  pl.BlockSpec((N_EXPERTS, tm), lambda i, e, f, cnt: (0, i)),
                  row(LANES), row(LANES),
                  pl.BlockSpec((1, 1, d, tf), lambda i, e, f, cnt: (e, f, 0, 0)),
                  pl.BlockSpec((1, 1, d, tf), lambda i, e, f, cnt: (e, f, 0, 0)),
                  pl.BlockSpec((1, 1, tf, d), lambda i, e, f, cnt: (e, f, 0, 0))],
        out_specs=row(d),
        scratch_shapes=[pltpu.VMEM((cap, d), BF16), pltpu.VMEM((cap, d), F32)],
    )
    return pl.pallas_call(
        functools.partial(_moe_kernel, rows=rows),
        grid_spec=grid_spec,
        out_shape=jax.ShapeDtypeStruct((n, d), F32),
        compiler_params=_cparams(("parallel", "arbitrary", "arbitrary")),
        name="expert_swiglu",
    )(cnt, hn, post, pos, comb, wg, wu, wd)


def _add_norm_kernel(h_ref, d_ref, g_ref, h_out_ref, hn_out_ref):
    h = h_ref[...] + d_ref[...]
    h_out_ref[...] = h
    hn_out_ref[...] = _rms(h, g_ref[...]).astype(hn_out_ref.dtype)


def _add_norm(h, delta, g, tm):
    n, d = h.shape
    row = pl.BlockSpec((tm, d), lambda i: (i, 0))
    return pl.pallas_call(
        _add_norm_kernel,
        grid=(n // tm,),
        in_specs=[row, row, pl.BlockSpec((1, d), lambda i: (0, 0))],
        out_specs=[row, row],
        out_shape=[jax.ShapeDtypeStruct((n, d), F32), jax.ShapeDtypeStruct((n, d), BF16)],
        compiler_params=_cparams(("parallel",)),
        name="residual_norm",
    )(h, delta, g)


def _pad_lanes(a, width=LANES):
    return jnp.pad(a, ((0, 0), (0, width - a.shape[-1])))


def kernel(x, meta, norm_mix, w_in, b_in, q_norm, k_norm, w_attn_o, conv_w, conv_b, conv_ln_g, conv_ln_b,
           w_conv_o, pool_w, pool_scale, w_pool_o, w_out, norm_ffn, w_ff_gate, w_ff_up, w_ff_down, w_router,
           b_router, w_e_gate, w_e_up, w_e_down):
    bsz, seq, d = x.shape
    depth = w_in.shape[0]
    length = seq + N_META
    lp = -(-length // SEQ_ALIGN) * SEQ_ALIGN
    n = bsz * lp

    tm = _pick(n, (1536, 768, 512, 256))
    tm_mid = _pick(n, (768, 512, 256))
    tm_ffn = _pick(n, (512, 256))
    tm_moe = _pick(n, (1408, 768, 512, 256))
    tn = _pick(Z_W, (1792, 768, 256))
    ts = _pick(lp, (768, 512, 256))
    ts_f = ts
    tq = _pick(lp, (256,))
    tk = 256
    tf_dense = _pick(w_ff_gate.shape[2], (256,))
    tf_moe = _pick(w_e_gate.shape[3], (896, 512, 256))
    rows = {1408: 384}.get(tm_moe, 256)

    h = jnp.concatenate([jnp.broadcast_to(meta[None].astype(x.dtype), (bsz, N_META, d)), x], axis=1)
    h = jnp.pad(h, ((0, 0), (0, lp - length), (0, 0))).reshape(n, d)
    hn = _norm(h, norm_mix[0][None], tm)

    att_scale = HEAD_DIM ** -0.5
    gate_lo = 3 * ATT_W + ATT_HEADS + 2 * CONV_W + POOL_W
    f_lo = 3 * ATT_W
    for l in range(depth):
        wl = w_in[l]
        w_main = jnp.concatenate([wl[:, gate_lo:], wl[:, :f_lo], wl[:, f_lo + ATT_HEADS:gate_lo]], axis=1).astype(BF16)
        bl = b_in[l]
        b_main = jnp.concatenate([bl[gate_lo:], bl[:f_lo], bl[f_lo + ATT_HEADS:gate_lo]])[None]
        w_f = _pad_lanes(wl[:, f_lo:f_lo + ATT_HEADS])
        w_f_hi = w_f.astype(BF16)
        w_f = jnp.concatenate([w_f_hi, (w_f - w_f_hi.astype(F32)).astype(BF16)], axis=1)
        b_f = _pad_lanes(bl[None, f_lo:f_lo + ATT_HEADS])

        z = _inproj(hn, w_main, b_main, tm, tn)
        z3 = z.reshape(bsz, lp, Z_W)
        fcol = _forget(hn.reshape(bsz, lp, d), w_f, b_f, ts_f)
        frow = jnp.transpose(fcol[:, :, :ATT_HEADS], (0, 2, 1))
        qg = jnp.tile(q_norm[l], 2)[None] * att_scale
        kg = jnp.tile(k_norm[l], 2)[None]
        att = _attention(z3, fcol, frow, qg, kg, tq, tk)

        cw = jnp.pad(conv_w[l], ((0, HALO - CONV_K), (0, 0)))
        pw = jax.scipy.linalg.block_diag(*[pool_w[l, g] for g in range(pool_w.shape[1])]).astype(BF16)
        u, pm = _convpool(z3, cw, conv_b[l][None], conv_ln_g[l][None], conv_ln_b[l][None], pw,
                          pool_scale[l][None], ts)

        h, hn = _merge(att.reshape(n, ATT_W), u.reshape(n, CONV_W), pm.reshape(n, POOL_W), z, h,
                       w_attn_o[l].astype(BF16), w_conv_o[l].astype(BF16), w_pool_o[l].astype(BF16),
                       w_out[l].astype(BF16), norm_ffn[l][None], tm_mid)

        ng_next = norm_mix[min(l + 1, depth - 1)][None]
        i = l // 2
        if l % 2 == 0:
            h, hn = _ffn(hn, h, w_ff_gate[i].astype(BF16), w_ff_up[i].astype(BF16), w_ff_down[i].astype(BF16),
                         ng_next, tm_ffn, tf_dense)
        else:
            comb, pos, cnt = _router(h, norm_ffn[l][None], _pad_lanes(w_router[i]), _pad_lanes(b_router[i][None]),
                                     tm_moe)
            post = jnp.transpose(pos[:, :N_EXPERTS])
            cnt_flat = cnt[:, 0, :N_EXPERTS].reshape(-1)
            ne, _, dffe = w_e_gate[i].shape
            nf = dffe // tf_moe
            wg = w_e_gate[i].reshape(ne, d, nf, tf_moe).transpose(0, 2, 1, 3).astype(BF16)
            wu = w_e_up[i].reshape(ne, d, nf, tf_moe).transpose(0, 2, 1, 3).astype(BF16)
            wd = w_e_down[i].reshape(ne, nf, tf_moe, d).astype(BF16)
            delta = _moe(cnt_flat, hn, post, pos, comb, wg, wu, wd, tm_moe, rows)
            h, hn = _add_norm(h, delta, ng_next, tm)

    return h.reshape(bsz, lp, d)[:, N_META:length]
```

```python
import functools

import jax
import jax.numpy as jnp
from jax import lax
from jax.experimental import pallas as pl
from jax.experimental.pallas import tpu as pltpu

F32 = jnp.float32
BF16 = jnp.bfloat16
HIGHEST = lax.Precision.HIGHEST

D_MODEL = 1024
N_META = 16
HEAD_DIM = 64
ATT_W = 512
ATT_HEADS = 8
CONV_W = 256
CONV_K = 31
POOL_W = 256
POOL_WINDOWS = (2, 4, 8, 16)
N_EXPERTS = 8
LANES = 128
HALO = 32
NEG_INF = -1e30
EXP_UNDERFLOW = 110.0
SEQ_ALIGN = 256
VMEM_LIMIT = 56 * 1024 * 1024
MOE_VMEM_LIMIT = 60 * 1024 * 1024

Z_GATE = 0
Z_Q = 3072
Z_K = 3584
Z_V = 4096
Z_A = 4608
Z_G = 4864
Z_P = 5120
Z_W = 5376


def _pick(n, candidates):
    for c in candidates:
        if n % c == 0:
            return c
    raise ValueError(f"no tile for {n} in {candidates}")


def _cparams(sem):
    return pltpu.CompilerParams(dimension_semantics=sem, vmem_limit_bytes=VMEM_LIMIT)


def _rms(x, g, eps=1e-6):
    return x * lax.rsqrt(jnp.mean(x * x, axis=-1, keepdims=True) + eps) * g


def _sigmoid(x):
    return 1.0 / (1.0 + jnp.exp(-x))


def _silu(x):
    return x * _sigmoid(x)


def _norm_kernel(h_ref, g_ref, o_ref):
    o_ref[...] = _rms(h_ref[...], g_ref[...]).astype(o_ref.dtype)


def _norm(h, g, tm):
    n, d = h.shape
    return pl.pallas_call(
        _norm_kernel,
        grid=(n // tm,),
        in_specs=[pl.BlockSpec((tm, d), lambda i: (i, 0)), pl.BlockSpec((1, d), lambda i: (0, 0))],
        out_specs=pl.BlockSpec((tm, d), lambda i: (i, 0)),
        out_shape=jax.ShapeDtypeStruct((n, d), BF16),
        compiler_params=_cparams(("parallel",)),
        name="rmsnorm",
    )(h, g)


def _inproj_kernel(x_ref, w_ref, b_ref, o_ref):
    acc = jnp.dot(x_ref[...], w_ref[...], preferred_element_type=F32)
    o_ref[...] = (acc + b_ref[...]).astype(o_ref.dtype)


def _inproj(hn, w, b, tm, tn):
    n, d = hn.shape
    zw = w.shape[1]
    return pl.pallas_call(
        _inproj_kernel,
        grid=(n // tm, zw // tn),
        in_specs=[pl.BlockSpec((tm, d), lambda i, j: (i, 0)),
                  pl.BlockSpec((d, tn), lambda i, j: (0, j)),
                  pl.BlockSpec((1, tn), lambda i, j: (0, j))],
        out_specs=pl.BlockSpec((tm, tn), lambda i, j: (i, j)),
        out_shape=jax.ShapeDtypeStruct((n, zw), BF16),
        compiler_params=_cparams(("parallel", "arbitrary")),
        name="inproj",
    )(hn, w, b)


def _forget_kernel(x_ref, w_ref, b_ref, o_ref, carry_ref):
    @pl.when(pl.program_id(1) == 0)
    def _():
        carry_ref[...] = jnp.zeros_like(carry_ref)

    fw = jnp.dot(x_ref[0], w_ref[...], preferred_element_type=F32)
    f = fw[:, :LANES] + fw[:, LANES:] + b_ref[...]
    ls = jnp.minimum(f, 0.0) - jnp.log(1.0 + jnp.exp(-jnp.abs(f)))
    hi = ls.astype(BF16)
    r1 = ls - hi.astype(F32)
    mid = r1.astype(BF16)
    lo = (r1 - mid.astype(F32)).astype(BF16)
    t = ls.shape[0]
    r = lax.broadcasted_iota(jnp.int32, (t, t), 0)
    c = lax.broadcasted_iota(jnp.int32, (t, t), 1)
    tril = jnp.where(c <= r, 1.0, 0.0).astype(BF16)
    parts = jnp.dot(tril, jnp.concatenate([hi, mid, lo], axis=-1), preferred_element_type=F32)
    cs = (parts[:, :LANES] + parts[:, LANES:2 * LANES]) + parts[:, 2 * LANES:] + carry_ref[...]
    o_ref[0] = cs
    carry_ref[...] = cs[t - 1:t, :]


def _forget(hn3, w_f, b_f, ts):
    b, lp, d = hn3.shape
    return pl.pallas_call(
        _forget_kernel,
        grid=(b, lp // ts),
        in_specs=[pl.BlockSpec((1, ts, d), lambda bi, t: (bi, t, 0)),
                  pl.BlockSpec((d, 2 * LANES), lambda bi, t: (0, 0)),
                  pl.BlockSpec((1, LANES), lambda bi, t: (0, 0))],
        out_specs=pl.BlockSpec((1, ts, LANES), lambda bi, t: (bi, t, 0)),
        out_shape=jax.ShapeDtypeStruct((b, lp, LANES), F32),
        scratch_shapes=[pltpu.VMEM((1, LANES), F32)],
        compiler_params=_cparams(("parallel", "arbitrary")),
        name="forget_cumsum",
    )(hn3, w_f, b_f)


def _pair_rms(x, first_head, gain, eps=1e-6):
    sq = x * x
    s0 = jnp.sum(jnp.where(first_head, sq, 0.0), axis=-1, keepdims=True)
    s1 = jnp.sum(jnp.where(first_head, 0.0, sq), axis=-1, keepdims=True)
    ms = jnp.where(first_head, s0, s1) * (1.0 / HEAD_DIM)
    return x * lax.rsqrt(ms + eps) * gain


def _attn_kernel(lo_ref, q_ref, k_ref, v_ref, fc_ref, fr_ref, qg_ref, kg_ref, o_ref, kn_ref, *, tq, tk):
    bi = pl.program_id(0)
    p = pl.program_id(1)
    i = pl.program_id(2)
    nq = pl.num_programs(2)
    lp = k_ref.shape[1]
    lane = lax.broadcasted_iota(jnp.int32, (1, LANES), 1)
    first_head = lane < HEAD_DIM

    @pl.when(i == 0)
    def _():
        def body(c, carry):
            off = pl.multiple_of(c * tk, tk)
            kk = k_ref[0, pl.ds(off, tk), :].astype(F32)
            kn_ref[pl.ds(off, tk), :] = _pair_rms(kk, first_head, kg_ref[...]).astype(BF16)
            return carry
        lax.fori_loop(0, lp // tk, body, 0)

    qn = _pair_rms(q_ref[0].astype(F32), first_head, qg_ref[...])
    fc = fc_ref[0]
    row_pos = i * tq + lax.broadcasted_iota(jnp.int32, (tq, 1), 0)

    heads = [2 * p, 2 * p + 1]
    qhs = [jnp.where(first_head, qn, 0.0).astype(BF16), jnp.where(first_head, 0.0, qn).astype(BF16)]
    fts = [jnp.sum(jnp.where(lane == hd, fc, 0.0), axis=-1, keepdims=True) for hd in heads]

    def update(off, width, carry, hh, masked):
        m, l, acc = carry
        ks = kn_ref[pl.ds(off, width), :]
        s = lax.dot_general(qhs[hh], ks, (((1,), (1,)), ((), ())), preferred_element_type=F32)
        fs = fr_ref[0, pl.ds(heads[hh], 1), pl.ds(off, width)]
        s = s + (fts[hh] - fs)
        if masked:
            col_pos = off + lax.broadcasted_iota(jnp.int32, (1, width), 1)
            s = jnp.where(col_pos <= row_pos, s, NEG_INF)
        m_new = jnp.maximum(m, jnp.max(s, axis=-1, keepdims=True))
        alpha = jnp.exp(m - m_new)
        pm = jnp.exp(s - m_new)
        l = alpha * l + jnp.sum(pm, axis=-1, keepdims=True)
        vs = v_ref[0, pl.ds(off, width), :]
        acc = alpha * acc + jnp.dot(pm.astype(BF16), vs, preferred_element_type=F32)
        return m_new, l, acc

    win = jnp.maximum(i - 1, 0)
    firsts = [lo_ref[(bi * ATT_HEADS + hd) * nq + i] for hd in heads]
    init = (jnp.full((tq, 1), NEG_INF, F32), jnp.zeros((tq, 1), F32), jnp.zeros((tq, LANES), F32))

    def far(j, carry):
        off = pl.multiple_of(j * tk, tk)
        return tuple(update(off, tk, carry[hh], hh, masked=False) for hh in range(2))

    carry = lax.fori_loop(jnp.minimum(firsts[0], firsts[1]), win, far, (init, init))
    off = pl.multiple_of(win * tk, tk)
    outs = []
    for hh in range(2):
        m, l, acc = update(off, 2 * tk, carry[hh], hh, masked=True)
        outs.append(acc / l)
    o_ref[0] = jnp.where(first_head, outs[0], outs[1]).astype(o_ref.dtype)


def _first_live_chunk(frow, qg, kg, tq, tk):
    b, nh, lp = frow.shape
    qk_bound = 1.02 * HEAD_DIM * jnp.max(jnp.abs(qg)) * jnp.max(jnp.abs(kg))
    f_first = frow[:, :, 0::tq]
    f_last = frow[:, :, tk - 1::tk]
    dead = (f_first[:, :, :, None] - f_last[:, :, None, :] + 2.0 * qk_bound) < -EXP_UNDERFLOW
    n_dead = jnp.sum(dead.astype(jnp.int32), axis=-1)
    n_full = (jnp.arange(lp // tq, dtype=jnp.int32) * tq) // tk
    return jnp.minimum(n_dead, n_full[None, None, :]).reshape(-1)


def _attention(z3, fcol, frow, qg, kg, tq, tk):
    b, lp, _ = z3.shape
    npairs = ATT_HEADS // 2
    qb, kb, vb = Z_Q // LANES, Z_K // LANES, Z_V // LANES
    assert tq == tk and lp >= 2 * tk, (tq, tk, lp)
    first = _first_live_chunk(frow, qg, kg, tq, tk)
    grid_spec = pltpu.PrefetchScalarGridSpec(
        num_scalar_prefetch=1,
        grid=(b, npairs, lp // tq),
        in_specs=[pl.BlockSpec((1, tq, LANES), lambda bi, p, i, lo: (bi, i, qb + p)),
                  pl.BlockSpec((1, lp, LANES), lambda bi, p, i, lo: (bi, 0, kb + p)),
                  pl.BlockSpec((1, lp, LANES), lambda bi, p, i, lo: (bi, 0, vb + p)),
                  pl.BlockSpec((1, tq, LANES), lambda bi, p, i, lo: (bi, i, 0)),
                  pl.BlockSpec((1, ATT_HEADS, lp), lambda bi, p, i, lo: (bi, 0, 0)),
                  pl.BlockSpec((1, LANES), lambda bi, p, i, lo: (0, 0)),
                  pl.BlockSpec((1, LANES), lambda bi, p, i, lo: (0, 0))],
        out_specs=pl.BlockSpec((1, tq, LANES), lambda bi, p, i, lo: (bi, i, p)),
        scratch_shapes=[pltpu.VMEM((lp, LANES), BF16)],
    )
    return pl.pallas_call(
        functools.partial(_attn_kernel, tq=tq, tk=tk),
        grid_spec=grid_spec,
        out_shape=jax.ShapeDtypeStruct((b, lp, ATT_W), BF16),
        compiler_params=_cparams(("parallel", "parallel", "arbitrary")),
        name="fox_attention",
    )(first, z3, z3, z3, fcol, frow, qg, kg)


def _convpool_kernel(a_ref, g_ref, p_ref, ah_ref, gh_ref, ph_ref, cw_ref, cb_ref, lg_ref, lb_ref,
                     pw_ref, ps_ref, u_ref, pm_ref, ext_ref, pext_ref, *, ts):
    i = pl.program_id(1)
    has_prev = i > 0

    u = a_ref[0].astype(F32) * _sigmoid(g_ref[0].astype(F32))
    uh = ah_ref[0].astype(F32) * _sigmoid(gh_ref[0].astype(F32))
    ext_ref[0:HALO, :] = jnp.where(has_prev, uh, 0.0)
    ext_ref[HALO:HALO + ts, :] = u
    acc = jnp.zeros((ts, CONV_W), F32) + cb_ref[...]
    for j in range(CONV_K):
        start = HALO - (CONV_K - 1) + j
        acc = acc + cw_ref[j:j + 1, :] * ext_ref[start:start + ts, :]
    mu = jnp.mean(acc, axis=-1, keepdims=True)
    cen = acc - mu
    var = jnp.mean(cen * cen, axis=-1, keepdims=True)
    y = cen * lax.rsqrt(var + 1e-5) * lg_ref[...] + lb_ref[...]
    u_ref[0] = _silu(y).astype(u_ref.dtype)

    x = p_ref[0].astype(F32)
    pext_ref[0:HALO, :] = jnp.where(has_prev, ph_ref[0].astype(F32), 0.0)
    pext_ref[HALO:HALO + ts, :] = x
    pos1 = (i * ts + 1 + lax.broadcasted_iota(jnp.int32, (ts, 1), 0)).astype(F32)
    lane = lax.broadcasted_iota(jnp.int32, (1, POOL_W), 1)
    group_w = POOL_W // len(POOL_WINDOWS)
    run = x
    pooled = jnp.zeros((ts, POOL_W), F32)
    for k in range(1, max(POOL_WINDOWS)):
        run = run + pext_ref[HALO - k:HALO - k + ts, :]
        if (k + 1) in POOL_WINDOWS:
            gi = POOL_WINDOWS.index(k + 1)
            mean = run / jnp.minimum(pos1, float(k + 1))
            in_group = (lane >= gi * group_w) & (lane < (gi + 1) * group_w)
            pooled = jnp.where(in_group, mean, pooled)
    pm = (pooled - x).astype(BF16)
    lin = jnp.dot(pm, pw_ref[...], preferred_element_type=F32) * ps_ref[...]
    pm_ref[0] = lin.astype(pm_ref.dtype)


def _convpool(z3, cw, cb, lg, lb, pw, ps, ts):
    b, lp, _ = z3.shape
    ab, gb, pb = Z_A // CONV_W, Z_G // CONV_W, Z_P // POOL_W
    hpt = ts // HALO
    main = lambda blk: pl.BlockSpec((1, ts, CONV_W), lambda bi, i: (bi, i, blk))
    halo = lambda blk: pl.BlockSpec((1, HALO, CONV_W), lambda bi, i: (bi, jnp.maximum(i * hpt - 1, 0), blk))
    const = lambda shape: pl.BlockSpec(shape, lambda bi, i: (0, 0))
    out = pl.BlockSpec((1, ts, CONV_W), lambda bi, i: (bi, i, 0))
    return pl.pallas_call(
        functools.partial(_convpool_kernel, ts=ts),
        grid=(b, lp // ts),
        in_specs=[main(ab), main(gb), main(pb), halo(ab), halo(gb), halo(pb),
                  const((HALO, CONV_W)), const((1, CONV_W)), const((1, CONV_W)), const((1, CONV_W)),
                  const((POOL_W, POOL_W)), const((1, POOL_W))],
        out_specs=[out, out],
        out_shape=[jax.ShapeDtypeStruct((b, lp, CONV_W), BF16), jax.ShapeDtypeStruct((b, lp, POOL_W), BF16)],
        scratch_shapes=[pltpu.VMEM((HALO + ts, CONV_W), F32), pltpu.VMEM((HALO + ts, POOL_W), F32)],
        compiler_params=_cparams(("parallel", "arbitrary")),
        name="conv_pool",
    )(z3, z3, z3, z3, z3, z3, cw, cb, lg, lb, pw, ps)


def _merge_kernel(att_ref, u_ref, pm_ref, g0_ref, g1_ref, g2_ref, h_ref, wa_ref, wc_ref, wp_ref, wo_ref,
                  ng_ref, h_out_ref, hn_out_ref):
    ya = jnp.dot(att_ref[...], wa_ref[...], preferred_element_type=F32)
    yc = jnp.dot(u_ref[...], wc_ref[...], preferred_element_type=F32)
    yp = jnp.dot(pm_ref[...], wp_ref[...], preferred_element_type=F32)
    m = (_sigmoid(g0_ref[...].astype(F32)) * ya + _sigmoid(g1_ref[...].astype(F32)) * yc
         + _sigmoid(g2_ref[...].astype(F32)) * yp)
    h = h_ref[...] + jnp.dot(m.astype(BF16), wo_ref[...], preferred_element_type=F32)
    h_out_ref[...] = h
    hn_out_ref[...] = _rms(h, ng_ref[...]).astype(hn_out_ref.dtype)


def _merge(att, u, pm, z, h, wa, wc, wp, wo, ng, tm):
    n, d = h.shape
    row = lambda w, blk=0: pl.BlockSpec((tm, w), lambda i: (i, blk))
    const = lambda shape: pl.BlockSpec(shape, lambda i: (0, 0))
    return pl.pallas_call(
        _merge_kernel,
        grid=(n // tm,),
        in_specs=[row(ATT_W), row(CONV_W), row(POOL_W), row(d, 0), row(d, 1), row(d, 2), row(d),
                  const(wa.shape), const(wc.shape), const(wp.shape), const(wo.shape), const((1, d))],
        out_specs=[row(d), row(d)],
        out_shape=[jax.ShapeDtypeStruct((n, d), F32), jax.ShapeDtypeStruct((n, d), BF16)],
        compiler_params=_cparams(("parallel",)),
        name="merge_outproj",
    )(att, u, pm, z, z, z, h, wa, wc, wp, wo, ng)


def _ffn_kernel(x_ref, h_ref, wg_ref, wu_ref, wd_ref, ng_ref, h_out_ref, hn_out_ref, *, tf):
    x = x_ref[...]
    h = h_ref[...]
    for c in range(wg_ref.shape[1] // tf):
        gt = jnp.dot(x, wg_ref[:, c * tf:(c + 1) * tf], preferred_element_type=F32)
        up = jnp.dot(x, wu_ref[:, c * tf:(c + 1) * tf], preferred_element_type=F32)
        h = h + jnp.dot((_silu(gt) * up).astype(BF16), wd_ref[c * tf:(c + 1) * tf, :],
                        preferred_element_type=F32)
    h_out_ref[...] = h
    hn_out_ref[...] = _rms(h, ng_ref[...]).astype(hn_out_ref.dtype)


def _ffn(hn, h, wg, wu, wd, ng, tm, tf):
    n, d = h.shape
    row = pl.BlockSpec((tm, d), lambda i: (i, 0))
    const = lambda shape: pl.BlockSpec(shape, lambda i: (0, 0), pipeline_mode=pl.Buffered(1))
    return pl.pallas_call(
        functools.partial(_ffn_kernel, tf=tf),
        grid=(n // tm,),
        in_specs=[row, row, const(wg.shape), const(wu.shape), const(wd.shape), const((1, d))],
        out_specs=[row, row],
        out_shape=[jax.ShapeDtypeStruct((n, d), F32), jax.ShapeDtypeStruct((n, d), BF16)],
        compiler_params=_cparams(("parallel",)),
        name="dense_swiglu",
    )(hn, h, wg, wu, wd, ng)


def _router_kernel(h_ref, ng_ref, wr_ref, br_ref, comb_ref, pos_ref, cnt_ref, *, sb):
    tm = h_ref.shape[0]
    hn = _rms(h_ref[...], ng_ref[...])
    logits = jnp.dot(hn, wr_ref[...], precision=HIGHEST, preferred_element_type=F32) + br_ref[...]
    lane = lax.broadcasted_iota(jnp.int32, (1, LANES), 1).astype(F32)
    lg = jnp.where(lane < N_EXPERTS, logits, -jnp.inf)
    m1 = jnp.max(lg, axis=-1, keepdims=True)
    i1 = jnp.min(jnp.where(lg == m1, lane, float(LANES)), axis=-1, keepdims=True)
    sel1 = lane == i1
    lg2 = jnp.where(sel1, -jnp.inf, lg)
    m2 = jnp.max(lg2, axis=-1, keepdims=True)
    i2 = jnp.min(jnp.where(lg2 == m2, lane, float(LANES)), axis=-1, keepdims=True)
    sel2 = lane == i2
    e = jnp.exp(m2 - m1)
    g1 = 1.0 / (1.0 + e)
    comb_ref[...] = jnp.where(sel1, g1, 0.0) + jnp.where(sel2, e * g1, 0.0)
    sel = jnp.where(sel1 | sel2, 1.0, 0.0)
    r = lax.broadcasted_iota(jnp.int32, (sb, sb), 0)
    c = lax.broadcasted_iota(jnp.int32, (sb, sb), 1)
    tril = jnp.where(c <= r, 1.0, 0.0).astype(BF16)
    carry = jnp.zeros((1, LANES), F32)
    for s in range(tm // sb):
        blk = slice(s * sb, (s + 1) * sb)
        incl = jnp.dot(tril, sel[blk].astype(BF16), preferred_element_type=F32) + carry
        pos_ref[blk, :] = jnp.where(sel[blk] > 0.0, incl - 1.0, -1.0)
        carry = incl[sb - 1:sb, :]
    cnt_ref[0] = jnp.broadcast_to(carry, (8, LANES)).astype(jnp.int32)


def _router(h, ng, wr, br, tm, sb):
    n, d = h.shape
    nt = n // tm
    return pl.pallas_call(
        functools.partial(_router_kernel, sb=sb),
        grid=(nt,),
        in_specs=[pl.BlockSpec((tm, d), lambda i: (i, 0)),
                  pl.BlockSpec((1, d), lambda i: (0, 0)),
                  pl.BlockSpec((d, LANES), lambda i: (0, 0)),
                  pl.BlockSpec((1, LANES), lambda i: (0, 0))],
        out_specs=[pl.BlockSpec((tm, LANES), lambda i: (i, 0)),
                   pl.BlockSpec((tm, LANES), lambda i: (i, 0)),
                   pl.BlockSpec((1, 8, LANES), lambda i: (i, 0, 0))],
        out_shape=[jax.ShapeDtypeStruct((n, LANES), F32), jax.ShapeDtypeStruct((n, LANES), F32),
                   jax.ShapeDtypeStruct((nt, 8, LANES), jnp.int32)],
        compiler_params=_cparams(("parallel",)),
        name="router_top2",
    )(h, ng, wr, br)


def _moe_kernel(cnt_ref, x_ref, post_ref, pos_ref, comb_ref, wg_ref, wu_ref, wd_ref,
                out_ref, xs_ref, ys_ref, *, rows, sub):
    i = pl.program_id(0)
    e = pl.program_id(1)
    f = pl.program_id(2)
    nf = pl.num_programs(2)
    nch = (cnt_ref[i * N_EXPERTS + e] + rows - 1) // rows

    @pl.when((e == 0) & (f == 0))
    def _():
        out_ref[...] = jnp.zeros_like(out_ref)

    @pl.when(f == 0)
    def _():
        prow = post_ref[0, pl.ds(e, 1), :]
        def gather(c, carry):
            off = pl.multiple_of(c * rows, rows)
            tgt = (off + lax.broadcasted_iota(jnp.int32, (rows, 1), 0)).astype(F32)
            onehot = jnp.where(prow == tgt, 1.0, 0.0).astype(BF16)
            xs_ref[pl.ds(off, rows), :] = jnp.dot(onehot, x_ref[...], preferred_element_type=F32).astype(BF16)
            ys_ref[pl.ds(off, rows), :] = jnp.zeros((rows, D_MODEL), F32)
            return carry
        lax.fori_loop(0, nch, gather, 0)

    def expert(c, carry):
        off = pl.multiple_of(c * rows, rows)
        xc = xs_ref[pl.ds(off, rows), :]
        y = ys_ref[pl.ds(off, rows), :]
        for s in range(wg_ref.shape[3] // sub):
            cols = slice(s * sub, (s + 1) * sub)
            gt = jnp.dot(xc, wg_ref[0, 0, :, cols], preferred_element_type=F32)
            up = jnp.dot(xc, wu_ref[0, 0, :, cols], preferred_element_type=F32)
            y = y + jnp.dot((_silu(gt) * up).astype(BF16), wd_ref[0, 0, cols, :], preferred_element_type=F32)
        ys_ref[pl.ds(off, rows), :] = y
        return carry
    lax.fori_loop(0, nch, expert, 0)

    @pl.when(f == nf - 1)
    def _():
        lane = lax.broadcasted_iota(jnp.int32, (1, LANES), 1)
        pcol = jnp.sum(jnp.where(lane == e, pos_ref[...], 0.0), axis=-1, keepdims=True)
        gcol = jnp.sum(jnp.where(lane == e, comb_ref[...], 0.0), axis=-1, keepdims=True)
        def scatter(c, carry):
            off = pl.multiple_of(c * rows, rows)
            tgt = (off + lax.broadcasted_iota(jnp.int32, (1, rows), 1)).astype(F32)
            onehot_t = jnp.where(pcol == tgt, 1.0, 0.0).astype(BF16)
            y = ys_ref[pl.ds(off, rows), :].astype(BF16)
            out_ref[...] += gcol * jnp.dot(onehot_t, y, preferred_element_type=F32)
            return carry
        lax.fori_loop(0, nch, scatter, 0)


def _moe(cnt, hn, post, pos, comb, wg, wu, wd, tm, rows, sub):
    n, d = hn.shape
    nf, tf = wg.shape[1], wg.shape[3]
    cap = -(-tm // rows) * rows
    once = pl.Buffered(1)
    row = lambda w: pl.BlockSpec((tm, w), lambda i, e, f, cnt: (i, 0), pipeline_mode=once)
    grid_spec = pltpu.PrefetchScalarGridSpec(
        num_scalar_prefetch=1,
        grid=(n // tm, N_EXPERTS, nf),
        in_specs=[row(d),
                  pl.BlockSpec((1, N_EXPERTS, tm), lambda i, e, f, cnt: (i, 0, 0), pipeline_mode=once),
                  row(LANES), row(LANES),
                  pl.BlockSpec((1, 1, d, tf), lambda i, e, f, cnt: (e, f, 0, 0)),
                  pl.BlockSpec((1, 1, d, tf), lambda i, e, f, cnt: (e, f, 0, 0)),
                  pl.BlockSpec((1, 1, tf, d), lambda i, e, f, cnt: (e, f, 0, 0))],
        out_specs=pl.BlockSpec((tm, d), lambda i, e, f, cnt: (i, 0)),
        scratch_shapes=[pltpu.VMEM((cap, d), BF16), pltpu.VMEM((cap, d), F32)],
    )
    return pl.pallas_call(
        functools.partial(_moe_kernel, rows=rows, sub=sub),
        grid_spec=grid_spec,
        out_shape=jax.ShapeDtypeStruct((n, d), F32),
        compiler_params=pltpu.CompilerParams(dimension_semantics=("parallel", "arbitrary", "arbitrary"),
                                             vmem_limit_bytes=MOE_VMEM_LIMIT),
        name="expert_swiglu",
    )(cnt, hn, post, pos, comb, wg, wu, wd)


def _add_norm_kernel(h_ref, d_ref, g_ref, h_out_ref, hn_out_ref):
    h = h_ref[...] + d_ref[...]
    h_out_ref[...] = h
    hn_out_ref[...] = _rms(h, g_ref[...]).astype(hn_out_ref.dtype)


def _add_norm(h, delta, g, tm):
    n, d = h.shape
    row = pl.BlockSpec((tm, d), lambda i: (i, 0))
    return pl.pallas_call(
        _add_norm_kernel,
        grid=(n // tm,),
        in_specs=[row, row, pl.BlockSpec((1, d), lambda i: (0, 0))],
        out_specs=[row, row],
        out_shape=[jax.ShapeDtypeStruct((n, d), F32), jax.ShapeDtypeStruct((n, d), BF16)],
        compiler_params=_cparams(("parallel",)),
        name="residual_norm",
    )(h, delta, g)


def _pad_lanes(a, width=LANES):
    return jnp.pad(a, ((0, 0), (0, width - a.shape[-1])))


def kernel(x, meta, norm_mix, w_in, b_in, q_norm, k_norm, w_attn_o, conv_w, conv_b, conv_ln_g, conv_ln_b,
           w_conv_o, pool_w, pool_scale, w_pool_o, w_out, norm_ffn, w_ff_gate, w_ff_up, w_ff_down, w_router,
           b_router, w_e_gate, w_e_up, w_e_down):
    bsz, seq, d = x.shape
    depth = w_in.shape[0]
    length = seq + N_META
    lp = -(-length // SEQ_ALIGN) * SEQ_ALIGN
    n = bsz * lp

    tm = _pick(n, (1536, 768, 512, 256))
    tm_mid = _pick(n, (768, 512, 256))
    tm_ffn = _pick(n, (768, 512, 256))
    tm_moe = _pick(n, (2112, 1408, 768, 512, 256))
    sb_router = _pick(tm_moe, (704, 768, 512, 256))
    tn = _pick(Z_W, (1792, 768, 256))
    ts = _pick(lp, (768, 512, 256))
    ts_f = ts
    tq = _pick(lp, (256,))
    tk = 256
    tf_dense = _pick(w_ff_gate.shape[2], (256,))
    tf_moe = _pick(w_e_gate.shape[3], (512, 256))
    sub_moe = _pick(tf_moe, (256,))
    rows = {2112: 576, 1408: 384}.get(tm_moe, 256)

    h = jnp.concatenate([jnp.broadcast_to(meta[None].astype(x.dtype), (bsz, N_META, d)), x], axis=1)
    h = jnp.pad(h, ((0, 0), (0, lp - length), (0, 0))).reshape(n, d)
    hn = _norm(h, norm_mix[0][None], tm)

    att_scale = HEAD_DIM ** -0.5
    gate_lo = 3 * ATT_W + ATT_HEADS + 2 * CONV_W + POOL_W
    f_lo = 3 * ATT_W
    for l in range(depth):
        wl = w_in[l]
        w_main = jnp.concatenate([wl[:, gate_lo:], wl[:, :f_lo], wl[:, f_lo + ATT_HEADS:gate_lo]], axis=1).astype(BF16)
        bl = b_in[l]
        b_main = jnp.concatenate([bl[gate_lo:], bl[:f_lo], bl[f_lo + ATT_HEADS:gate_lo]])[None]
        w_f = _pad_lanes(wl[:, f_lo:f_lo + ATT_HEADS])
        w_f_hi = w_f.astype(BF16)
        w_f = jnp.concatenate([w_f_hi, (w_f - w_f_hi.astype(F32)).astype(BF16)], axis=1)
        b_f = _pad_lanes(bl[None, f_lo:f_lo + ATT_HEADS])

        z = _inproj(hn, w_main, b_main, tm, tn)
        z3 = z.reshape(bsz, lp, Z_W)
        fcol = _forget(hn.reshape(bsz, lp, d), w_f, b_f, ts_f)
        frow = jnp.transpose(fcol[:, :, :ATT_HEADS], (0, 2, 1))
        qg = jnp.tile(q_norm[l], 2)[None] * att_scale
        kg = jnp.tile(k_norm[l], 2)[None]
        att = _attention(z3, fcol, frow, qg, kg, tq, tk)

        cw = jnp.pad(conv_w[l], ((0, HALO - CONV_K), (0, 0)))
        pw = jax.scipy.linalg.block_diag(*[pool_w[l, g] for g in range(pool_w.shape[1])]).astype(BF16)
        u, pm = _convpool(z3, cw, conv_b[l][None], conv_ln_g[l][None], conv_ln_b[l][None], pw,
                          pool_scale[l][None], ts)

        h, hn = _merge(att.reshape(n, ATT_W), u.reshape(n, CONV_W), pm.reshape(n, POOL_W), z, h,
                       w_attn_o[l].astype(BF16), w_conv_o[l].astype(BF16), w_pool_o[l].astype(BF16),
                       w_out[l].astype(BF16), norm_ffn[l][None], tm_mid)

        ng_next = norm_mix[min(l + 1, depth - 1)][None]
        i = l // 2
        if l % 2 == 0:
            h, hn = _ffn(hn, h, w_ff_gate[i].astype(BF16), w_ff_up[i].astype(BF16), w_ff_down[i].astype(BF16),
                         ng_next, tm_ffn, tf_dense)
        else:
            comb, pos, cnt = _router(h, norm_ffn[l][None], _pad_lanes(w_router[i]), _pad_lanes(b_router[i][None]),
                                     tm_moe, sb_router)
            post = jnp.transpose(pos[:, :N_EXPERTS].reshape(n // tm_moe, tm_moe, N_EXPERTS), (0, 2, 1))
            cnt_flat = cnt[:, 0, :N_EXPERTS].reshape(-1)
            ne, _, dffe = w_e_gate[i].shape
            nf = dffe // tf_moe
            wg = w_e_gate[i].reshape(ne, d, nf, tf_moe).transpose(0, 2, 1, 3).astype(BF16)
            wu = w_e_up[i].reshape(ne, d, nf, tf_moe).transpose(0, 2, 1, 3).astype(BF16)
            wd = w_e_down[i].reshape(ne, nf, tf_moe, d).astype(BF16)
            delta = _moe(cnt_flat, hn, post, pos, comb, wg, wu, wd, tm_moe, rows, sub_moe)
            h, hn = _add_norm(h, delta, ng_next, tm)

    return h.reshape(bsz, lp, d)[:, N_META:length]
```

```python
import functools

import jax
import jax.numpy as jnp
from jax import lax
from jax.experimental import pallas as pl
from jax.experimental.pallas import tpu as pltpu

F32 = jnp.float32
BF16 = jnp.bfloat16
HIGHEST = lax.Precision.HIGHEST

D_MODEL = 1024
N_META = 16
HEAD_DIM = 64
ATT_W = 512
ATT_HEADS = 8
CONV_W = 256
CONV_K = 31
POOL_W = 256
POOL_WINDOWS = (2, 4, 8, 16)
N_EXPERTS = 8
LANES = 128
HALO = 32
NEG_INF = -1e30
EXP_UNDERFLOW = 110.0
SEQ_ALIGN = 256
VMEM_LIMIT = 56 * 1024 * 1024
MOE_VMEM_LIMIT = 60 * 1024 * 1024
POST_ROWS = 32

Z_GATE = 0
Z_Q = 3072
Z_K = 3584
Z_V = 4096
Z_A = 4608
Z_G = 4864
Z_P = 5120
Z_W = 5376


def _pick(n, candidates):
    for c in candidates:
        if n % c == 0:
            return c
    raise ValueError(f"no tile for {n} in {candidates}")


def _cparams(sem):
    return pltpu.CompilerParams(dimension_semantics=sem, vmem_limit_bytes=VMEM_LIMIT)


def _rms(x, g, eps=1e-6):
    return x * lax.rsqrt(jnp.mean(x * x, axis=-1, keepdims=True) + eps) * g


def _sigmoid(x):
    return 1.0 / (1.0 + jnp.exp(-x))


def _silu(x):
    return x * _sigmoid(x)


def _norm_kernel(h_ref, g_ref, o_ref):
    o_ref[...] = _rms(h_ref[...], g_ref[...]).astype(o_ref.dtype)


def _norm(h, g, tm):
    n, d = h.shape
    return pl.pallas_call(
        _norm_kernel,
        grid=(n // tm,),
        in_specs=[pl.BlockSpec((tm, d), lambda i: (i, 0)), pl.BlockSpec((1, d), lambda i: (0, 0))],
        out_specs=pl.BlockSpec((tm, d), lambda i: (i, 0)),
        out_shape=jax.ShapeDtypeStruct((n, d), BF16),
        compiler_params=_cparams(("parallel",)),
        name="rmsnorm",
    )(h, g)


def _inproj_kernel(x_ref, w_ref, b_ref, o_ref):
    acc = jnp.dot(x_ref[...], w_ref[...], preferred_element_type=F32)
    o_ref[...] = (acc + b_ref[...]).astype(o_ref.dtype)


def _inproj(hn, w, b, tm, tn):
    n, d = hn.shape
    zw = w.shape[1]
    return pl.pallas_call(
        _inproj_kernel,
        grid=(n // tm, zw // tn),
        in_specs=[pl.BlockSpec((tm, d), lambda i, j: (i, 0)),
                  pl.BlockSpec((d, tn), lambda i, j: (0, j)),
                  pl.BlockSpec((1, tn), lambda i, j: (0, j))],
        out_specs=pl.BlockSpec((tm, tn), lambda i, j: (i, j)),
        out_shape=jax.ShapeDtypeStruct((n, zw), BF16),
        compiler_params=_cparams(("parallel", "arbitrary")),
        name="inproj",
    )(hn, w, b)


def _forget_kernel(x_ref, w_ref, b_ref, o_ref, carry_ref):
    @pl.when(pl.program_id(1) == 0)
    def _():
        carry_ref[...] = jnp.zeros_like(carry_ref)

    fw = jnp.dot(x_ref[0], w_ref[...], preferred_element_type=F32)
    f = fw[:, :LANES] + fw[:, LANES:] + b_ref[...]
    ls = jnp.minimum(f, 0.0) - jnp.log(1.0 + jnp.exp(-jnp.abs(f)))
    hi = ls.astype(BF16)
    r1 = ls - hi.astype(F32)
    mid = r1.astype(BF16)
    lo = (r1 - mid.astype(F32)).astype(BF16)
    t = ls.shape[0]
    r = lax.broadcasted_iota(jnp.int32, (t, t), 0)
    c = lax.broadcasted_iota(jnp.int32, (t, t), 1)
    tril = jnp.where(c <= r, 1.0, 0.0).astype(BF16)
    parts = jnp.dot(tril, jnp.concatenate([hi, mid, lo], axis=-1), preferred_element_type=F32)
    cs = (parts[:, :LANES] + parts[:, LANES:2 * LANES]) + parts[:, 2 * LANES:] + carry_ref[...]
    o_ref[0] = cs
    carry_ref[...] = cs[t - 1:t, :]


def _forget(hn3, w_f, b_f, ts):
    b, lp, d = hn3.shape
    return pl.pallas_call(
        _forget_kernel,
        grid=(b, lp // ts),
        in_specs=[pl.BlockSpec((1, ts, d), lambda bi, t: (bi, t, 0)),
                  pl.BlockSpec((d, 2 * LANES), lambda bi, t: (0, 0)),
                  pl.BlockSpec((1, LANES), lambda bi, t: (0, 0))],
        out_specs=pl.BlockSpec((1, ts, LANES), lambda bi, t: (bi, t, 0)),
        out_shape=jax.ShapeDtypeStruct((b, lp, LANES), F32),
        scratch_shapes=[pltpu.VMEM((1, LANES), F32)],
        compiler_params=_cparams(("parallel", "arbitrary")),
        name="forget_cumsum",
    )(hn3, w_f, b_f)


def _pair_rms(x, first_head, gain, eps=1e-6):
    sq = x * x
    s0 = jnp.sum(jnp.where(first_head, sq, 0.0), axis=-1, keepdims=True)
    s1 = jnp.sum(jnp.where(first_head, 0.0, sq), axis=-1, keepdims=True)
    ms = jnp.where(first_head, s0, s1) * (1.0 / HEAD_DIM)
    return x * lax.rsqrt(ms + eps) * gain


def _attn_kernel(lo_ref, q_ref, k_ref, v_ref, fc_ref, fr_ref, qg_ref, kg_ref, o_ref, kn_ref, *, tq, tk):
    bi = pl.program_id(0)
    p = pl.program_id(1)
    i = pl.program_id(2)
    nq = pl.num_programs(2)
    lp = k_ref.shape[1]
    lane = lax.broadcasted_iota(jnp.int32, (1, LANES), 1)
    first_head = lane < HEAD_DIM

    @pl.when(i == 0)
    def _():
        def body(c, carry):
            off = pl.multiple_of(c * tk, tk)
            kk = k_ref[0, pl.ds(off, tk), :].astype(F32)
            kn_ref[pl.ds(off, tk), :] = _pair_rms(kk, first_head, kg_ref[...]).astype(BF16)
            return carry
        lax.fori_loop(0, lp // tk, body, 0)

    qn = _pair_rms(q_ref[0].astype(F32), first_head, qg_ref[...])
    fc = fc_ref[0]
    row_pos = i * tq + lax.broadcasted_iota(jnp.int32, (tq, 1), 0)

    heads = [2 * p, 2 * p + 1]
    qhs = [jnp.where(first_head, qn, 0.0).astype(BF16), jnp.where(first_head, 0.0, qn).astype(BF16)]
    fts = [jnp.sum(jnp.where(lane == hd, fc, 0.0), axis=-1, keepdims=True) for hd in heads]

    def update(off, width, carry, hh, masked):
        m, l, acc = carry
        ks = kn_ref[pl.ds(off, width), :]
        s = lax.dot_general(qhs[hh], ks, (((1,), (1,)), ((), ())), preferred_element_type=F32)
        fs = fr_ref[0, pl.ds(heads[hh], 1), pl.ds(off, width)]
        s = s + (fts[hh] - fs)
        if masked:
            col_pos = off + lax.broadcasted_iota(jnp.int32, (1, width), 1)
            s = jnp.where(col_pos <= row_pos, s, NEG_INF)
        m_new = jnp.maximum(m, jnp.max(s, axis=-1, keepdims=True))
        alpha = jnp.exp(m - m_new)
        pm = jnp.exp(s - m_new)
        l = alpha * l + jnp.sum(pm, axis=-1, keepdims=True)
        vs = v_ref[0, pl.ds(off, width), :]
        acc = alpha * acc + jnp.dot(pm.astype(BF16), vs, preferred_element_type=F32)
        return m_new, l, acc

    win = jnp.maximum(i - 1, 0)
    firsts = [lo_ref[(bi * ATT_HEADS + hd) * nq + i] for hd in heads]
    init = (jnp.full((tq, 1), NEG_INF, F32), jnp.zeros((tq, 1), F32), jnp.zeros((tq, LANES), F32))

    def far(j, carry):
        off = pl.multiple_of(j * tk, tk)
        return tuple(update(off, tk, carry[hh], hh, masked=False) for hh in range(2))

    carry = lax.fori_loop(jnp.minimum(firsts[0], firsts[1]), win, far, (init, init))
    off = pl.multiple_of(win * tk, tk)
    outs = []
    for hh in range(2):
        m, l, acc = update(off, 2 * tk, carry[hh], hh, masked=True)
        outs.append(acc / l)
    o_ref[0] = jnp.where(first_head, outs[0], outs[1]).astype(o_ref.dtype)


def _first_live_chunk(frow, qg, kg, tq, tk):
    b, nh, lp = frow.shape
    qk_bound = 1.02 * HEAD_DIM * jnp.max(jnp.abs(qg)) * jnp.max(jnp.abs(kg))
    f_first = frow[:, :, 0::tq]
    f_last = frow[:, :, tk - 1::tk]
    dead = (f_first[:, :, :, None] - f_last[:, :, None, :] + 2.0 * qk_bound) < -EXP_UNDERFLOW
    n_dead = jnp.sum(dead.astype(jnp.int32), axis=-1)
    n_full = (jnp.arange(lp // tq, dtype=jnp.int32) * tq) // tk
    return jnp.minimum(n_dead, n_full[None, None, :]).reshape(-1)


def _attention(z3, fcol, frow, qg, kg, tq, tk):
    b, lp, _ = z3.shape
    npairs = ATT_HEADS // 2
    qb, kb, vb = Z_Q // LANES, Z_K // LANES, Z_V // LANES
    assert tq == tk and lp >= 2 * tk, (tq, tk, lp)
    first = _first_live_chunk(frow, qg, kg, tq, tk)
    grid_spec = pltpu.PrefetchScalarGridSpec(
        num_scalar_prefetch=1,
        grid=(b, npairs, lp // tq),
        in_specs=[pl.BlockSpec((1, tq, LANES), lambda bi, p, i, lo: (bi, i, qb + p)),
                  pl.BlockSpec((1, lp, LANES), lambda bi, p, i, lo: (bi, 0, kb + p)),
                  pl.BlockSpec((1, lp, LANES), lambda bi, p, i, lo: (bi, 0, vb + p)),
                  pl.BlockSpec((1, tq, LANES), lambda bi, p, i, lo: (bi, i, 0)),
                  pl.BlockSpec((1, ATT_HEADS, lp), lambda bi, p, i, lo: (bi, 0, 0)),
                  pl.BlockSpec((1, LANES), lambda bi, p, i, lo: (0, 0)),
                  pl.BlockSpec((1, LANES), lambda bi, p, i, lo: (0, 0))],
        out_specs=pl.BlockSpec((1, tq, LANES), lambda bi, p, i, lo: (bi, i, p)),
        scratch_shapes=[pltpu.VMEM((lp, LANES), BF16)],
    )
    return pl.pallas_call(
        functools.partial(_attn_kernel, tq=tq, tk=tk),
        grid_spec=grid_spec,
        out_shape=jax.ShapeDtypeStruct((b, lp, ATT_W), BF16),
        compiler_params=_cparams(("parallel", "parallel", "arbitrary")),
        name="fox_attention",
    )(first, z3, z3, z3, fcol, frow, qg, kg)


def _convpool_kernel(a_ref, g_ref, p_ref, ah_ref, gh_ref, ph_ref, cw_ref, cb_ref, lg_ref, lb_ref,
                     pw_ref, ps_ref, u_ref, pm_ref, ext_ref, pext_ref, *, ts):
    i = pl.program_id(1)
    has_prev = i > 0

    u = a_ref[0].astype(F32) * _sigmoid(g_ref[0].astype(F32))
    uh = ah_ref[0].astype(F32) * _sigmoid(gh_ref[0].astype(F32))
    ext_ref[0:HALO, :] = jnp.where(has_prev, uh, 0.0)
    ext_ref[HALO:HALO + ts, :] = u
    acc = jnp.zeros((ts, CONV_W), F32) + cb_ref[...]
    for j in range(CONV_K):
        start = HALO - (CONV_K - 1) + j
        acc = acc + cw_ref[j:j + 1, :] * ext_ref[start:start + ts, :]
    mu = jnp.mean(acc, axis=-1, keepdims=True)
    cen = acc - mu
    var = jnp.mean(cen * cen, axis=-1, keepdims=True)
    y = cen * lax.rsqrt(var + 1e-5) * lg_ref[...] + lb_ref[...]
    u_ref[0] = _silu(y).astype(u_ref.dtype)

    x = p_ref[0].astype(F32)
    pext_ref[0:HALO, :] = jnp.where(has_prev, ph_ref[0].astype(F32), 0.0)
    pext_ref[HALO:HALO + ts, :] = x
    pos1 = (i * ts + 1 + lax.broadcasted_iota(jnp.int32, (ts, 1), 0)).astype(F32)
    lane = lax.broadcasted_iota(jnp.int32, (1, POOL_W), 1)
    group_w = POOL_W // len(POOL_WINDOWS)
    run = x
    pooled = jnp.zeros((ts, POOL_W), F32)
    for k in range(1, max(POOL_WINDOWS)):
        run = run + pext_ref[HALO - k:HALO - k + ts, :]
        if (k + 1) in POOL_WINDOWS:
            gi = POOL_WINDOWS.index(k + 1)
            mean = run / jnp.minimum(pos1, float(k + 1))
            in_group = (lane >= gi * group_w) & (lane < (gi + 1) * group_w)
            pooled = jnp.where(in_group, mean, pooled)
    pm = (pooled - x).astype(BF16)
    lin = jnp.dot(pm, pw_ref[...], preferred_element_type=F32) * ps_ref[...]
    pm_ref[0] = lin.astype(pm_ref.dtype)


def _convpool(z3, cw, cb, lg, lb, pw, ps, ts):
    b, lp, _ = z3.shape
    ab, gb, pb = Z_A // CONV_W, Z_G // CONV_W, Z_P // POOL_W
    hpt = ts // HALO
    main = lambda blk: pl.BlockSpec((1, ts, CONV_W), lambda bi, i: (bi, i, blk))
    halo = lambda blk: pl.BlockSpec((1, HALO, CONV_W), lambda bi, i: (bi, jnp.maximum(i * hpt - 1, 0), blk))
    const = lambda shape: pl.BlockSpec(shape, lambda bi, i: (0, 0))
    out = pl.BlockSpec((1, ts, CONV_W), lambda bi, i: (bi, i, 0))
    return pl.pallas_call(
        functools.partial(_convpool_kernel, ts=ts),
        grid=(b, lp // ts),
        in_specs=[main(ab), main(gb), main(pb), halo(ab), halo(gb), halo(pb),
                  const((HALO, CONV_W)), const((1, CONV_W)), const((1, CONV_W)), const((1, CONV_W)),
                  const((POOL_W, POOL_W)), const((1, POOL_W))],
        out_specs=[out, out],
        out_shape=[jax.ShapeDtypeStruct((b, lp, CONV_W), BF16), jax.ShapeDtypeStruct((b, lp, POOL_W), BF16)],
        scratch_shapes=[pltpu.VMEM((HALO + ts, CONV_W), F32), pltpu.VMEM((HALO + ts, POOL_W), F32)],
        compiler_params=_cparams(("parallel", "arbitrary")),
        name="conv_pool",
    )(z3, z3, z3, z3, z3, z3, cw, cb, lg, lb, pw, ps)


def _merge_kernel(att_ref, u_ref, pm_ref, g0_ref, g1_ref, g2_ref, h_ref, wa_ref, wc_ref, wp_ref, wo_ref,
                  ng_ref, h_out_ref, hn_out_ref):
    ya = jnp.dot(att_ref[...], wa_ref[...], preferred_element_type=F32)
    yc = jnp.dot(u_ref[...], wc_ref[...], preferred_element_type=F32)
    yp = jnp.dot(pm_ref[...], wp_ref[...], preferred_element_type=F32)
    m = (_sigmoid(g0_ref[...].astype(F32)) * ya + _sigmoid(g1_ref[...].astype(F32)) * yc
         + _sigmoid(g2_ref[...].astype(F32)) * yp)
    h = h_ref[...] + jnp.dot(m.astype(BF16), wo_ref[...], preferred_element_type=F32)
    h_out_ref[...] = h
    hn_out_ref[...] = _rms(h, ng_ref[...]).astype(hn_out_ref.dtype)


def _merge(att, u, pm, z, h, wa, wc, wp, wo, ng, tm):
    n, d = h.shape
    row = lambda w, blk=0: pl.BlockSpec((tm, w), lambda i: (i, blk))
    const = lambda shape: pl.BlockSpec(shape, lambda i: (0, 0))
    return pl.pallas_call(
        _merge_kernel,
        grid=(n // tm,),
        in_specs=[row(ATT_W), row(CONV_W), row(POOL_W), row(d, 0), row(d, 1), row(d, 2), row(d),
                  const(wa.shape), const(wc.shape), const(wp.shape), const(wo.shape), const((1, d))],
        out_specs=[row(d), row(d)],
        out_shape=[jax.ShapeDtypeStruct((n, d), F32), jax.ShapeDtypeStruct((n, d), BF16)],
        compiler_params=_cparams(("parallel",)),
        name="merge_outproj",
    )(att, u, pm, z, z, z, h, wa, wc, wp, wo, ng)


def _ffn_kernel(x_ref, h_ref, wg_ref, wu_ref, wd_ref, ng_ref, h_out_ref, hn_out_ref, *, tf):
    x = x_ref[...]
    h = h_ref[...]
    for c in range(wg_ref.shape[1] // tf):
        gt = jnp.dot(x, wg_ref[:, c * tf:(c + 1) * tf], preferred_element_type=F32)
        up = jnp.dot(x, wu_ref[:, c * tf:(c + 1) * tf], preferred_element_type=F32)
        h = h + jnp.dot((_silu(gt) * up).astype(BF16), wd_ref[c * tf:(c + 1) * tf, :],
                        preferred_element_type=F32)
    h_out_ref[...] = h
    hn_out_ref[...] = _rms(h, ng_ref[...]).astype(hn_out_ref.dtype)


def _ffn(hn, h, wg, wu, wd, ng, tm, tf):
    n, d = h.shape
    row = pl.BlockSpec((tm, d), lambda i: (i, 0))
    const = lambda shape: pl.BlockSpec(shape, lambda i: (0, 0), pipeline_mode=pl.Buffered(1))
    return pl.pallas_call(
        functools.partial(_ffn_kernel, tf=tf),
        grid=(n // tm,),
        in_specs=[row, row, const(wg.shape), const(wu.shape), const(wd.shape), const((1, d))],
        out_specs=[row, row],
        out_shape=[jax.ShapeDtypeStruct((n, d), F32), jax.ShapeDtypeStruct((n, d), BF16)],
        compiler_params=_cparams(("parallel",)),
        name="dense_swiglu",
    )(hn, h, wg, wu, wd, ng)


def _router_kernel(h_ref, ng_ref, wr_ref, br_ref, comb_ref, pos_ref, cnt_ref, *, sb):
    tm = h_ref.shape[0]
    hn = _rms(h_ref[...], ng_ref[...])
    logits = jnp.dot(hn, wr_ref[...], precision=HIGHEST, preferred_element_type=F32) + br_ref[...]
    lane = lax.broadcasted_iota(jnp.int32, (1, LANES), 1).astype(F32)
    lg = jnp.where(lane < N_EXPERTS, logits, -jnp.inf)
    m1 = jnp.max(lg, axis=-1, keepdims=True)
    i1 = jnp.min(jnp.where(lg == m1, lane, float(LANES)), axis=-1, keepdims=True)
    sel1 = lane == i1
    lg2 = jnp.where(sel1, -jnp.inf, lg)
    m2 = jnp.max(lg2, axis=-1, keepdims=True)
    i2 = jnp.min(jnp.where(lg2 == m2, lane, float(LANES)), axis=-1, keepdims=True)
    sel2 = lane == i2
    e = jnp.exp(m2 - m1)
    g1 = 1.0 / (1.0 + e)
    comb_ref[...] = jnp.where(sel1, g1, 0.0) + jnp.where(sel2, e * g1, 0.0)
    sel = jnp.where(sel1 | sel2, 1.0, 0.0)
    r = lax.broadcasted_iota(jnp.int32, (sb, sb), 0)
    c = lax.broadcasted_iota(jnp.int32, (sb, sb), 1)
    tril = jnp.where(c <= r, 1.0, 0.0).astype(BF16)
    carry = jnp.zeros((1, LANES), F32)
    for s in range(tm // sb):
        blk = slice(s * sb, (s + 1) * sb)
        incl = jnp.dot(tril, sel[blk].astype(BF16), preferred_element_type=F32) + carry
        pos_ref[blk, :] = jnp.where(sel[blk] > 0.0, incl - 1.0, -1.0)
        carry = incl[sb - 1:sb, :]
    cnt_ref[0] = jnp.broadcast_to(carry, (8, LANES)).astype(jnp.int32)


def _router(h, ng, wr, br, tm, sb):
    n, d = h.shape
    nt = n // tm
    return pl.pallas_call(
        functools.partial(_router_kernel, sb=sb),
        grid=(nt,),
        in_specs=[pl.BlockSpec((tm, d), lambda i: (i, 0)),
                  pl.BlockSpec((1, d), lambda i: (0, 0)),
                  pl.BlockSpec((d, LANES), lambda i: (0, 0)),
                  pl.BlockSpec((1, LANES), lambda i: (0, 0))],
        out_specs=[pl.BlockSpec((tm, LANES), lambda i: (i, 0)),
                   pl.BlockSpec((tm, LANES), lambda i: (i, 0)),
                   pl.BlockSpec((1, 8, LANES), lambda i: (i, 0, 0))],
        out_shape=[jax.ShapeDtypeStruct((n, LANES), F32), jax.ShapeDtypeStruct((n, LANES), F32),
                   jax.ShapeDtypeStruct((nt, 8, LANES), jnp.int32)],
        compiler_params=_cparams(("parallel",)),
        name="router_top2",
    )(h, ng, wr, br)


def _moe_kernel(cnt_ref, x_ref, post_ref, pos_ref, comb_ref, wg_ref, wu_ref, wd_ref,
                out_ref, xs_ref, ys_ref, *, sizes, sub):
    i = pl.program_id(0)
    e = pl.program_id(1)
    f = pl.program_id(2)
    nf = pl.num_programs(2)
    cnt = cnt_ref[i * N_EXPERTS + e]
    big = sizes[-1]

    def for_each_chunk(body):
        lo = 0
        for r in sizes:
            @pl.when((cnt > lo) & (cnt <= r))
            def _(r=r):
                body(0, r)
            lo = r

        @pl.when(cnt > big)
        def _():
            def step(c, carry):
                body(pl.multiple_of(c * big, big), big)
                return carry
            lax.fori_loop(0, (cnt + big - 1) // big, step, 0)

    @pl.when((e == 0) & (f == 0))
    def _():
        out_ref[...] = jnp.zeros_like(out_ref)

    @pl.when(f == 0)
    def _():
        prow = post_ref[0, pl.ds(e, 1), :]

        def gather(off, r):
            tgt = (off + lax.broadcasted_iota(jnp.int32, (r, 1), 0)).astype(F32)
            onehot = jnp.where(prow == tgt, 1.0, 0.0).astype(BF16)
            xs_ref[pl.ds(off, r), :] = jnp.dot(onehot, x_ref[...], preferred_element_type=F32).astype(BF16)
            ys_ref[pl.ds(off, r), :] = jnp.zeros((r, D_MODEL), F32)
        for_each_chunk(gather)

    def expert(off, r):
        xc = xs_ref[pl.ds(off, r), :]
        y = ys_ref[pl.ds(off, r), :]
        for s in range(wg_ref.shape[3] // sub):
            cols = slice(s * sub, (s + 1) * sub)
            gt = jnp.dot(xc, wg_ref[0, 0, :, cols], preferred_element_type=F32)
            up = jnp.dot(xc, wu_ref[0, 0, :, cols], preferred_element_type=F32)
            y = y + jnp.dot((_silu(gt) * up).astype(BF16), wd_ref[0, 0, cols, :], preferred_element_type=F32)
        ys_ref[pl.ds(off, r), :] = y
    for_each_chunk(expert)

    @pl.when(f == nf - 1)
    def _():
        lane = lax.broadcasted_iota(jnp.int32, (1, LANES), 1)
        pcol = jnp.sum(jnp.where(lane == e, pos_ref[...], 0.0), axis=-1, keepdims=True)
        gcol = jnp.sum(jnp.where(lane == e, comb_ref[...], 0.0), axis=-1, keepdims=True)

        def scatter(off, r):
            tgt = (off + lax.broadcasted_iota(jnp.int32, (1, r), 1)).astype(F32)
            onehot_t = jnp.where(pcol == tgt, 1.0, 0.0).astype(BF16)
            y = ys_ref[pl.ds(off, r), :].astype(BF16)
            out_ref[...] += gcol * jnp.dot(onehot_t, y, preferred_element_type=F32)
        for_each_chunk(scatter)


def _moe(cnt, hn, post, pos, comb, wg, wu, wd, tm, sizes, sub):
    n, d = hn.shape
    nf, tf = wg.shape[1], wg.shape[3]
    cap = -(-tm // sizes[-1]) * sizes[-1]
    once = pl.Buffered(1)
    row = lambda w: pl.BlockSpec((tm, w), lambda i, e, f, cnt: (i, 0), pipeline_mode=once)
    grid_spec = pltpu.PrefetchScalarGridSpec(
        num_scalar_prefetch=1,
        grid=(n // tm, N_EXPERTS, nf),
        in_specs=[row(d),
                  pl.BlockSpec((1, POST_ROWS, tm), lambda i, e, f, cnt: (i, 0, 0), pipeline_mode=once),
                  row(LANES), row(LANES),
                  pl.BlockSpec((1, 1, d, tf), lambda i, e, f, cnt: (e, f, 0, 0)),
                  pl.BlockSpec((1, 1, d, tf), lambda i, e, f, cnt: (e, f, 0, 0)),
                  pl.BlockSpec((1, 1, tf, d), lambda i, e, f, cnt: (e, f, 0, 0))],
        out_specs=pl.BlockSpec((tm, d), lambda i, e, f, cnt: (i, 0)),
        scratch_shapes=[pltpu.VMEM((cap, d), BF16), pltpu.VMEM((cap, d), F32)],
    )
    return pl.pallas_call(
        functools.partial(_moe_kernel, sizes=sizes, sub=sub),
        grid_spec=grid_spec,
        out_shape=jax.ShapeDtypeStruct((n, d), F32),
        compiler_params=pltpu.CompilerParams(dimension_semantics=("parallel", "arbitrary", "arbitrary"),
                                             vmem_limit_bytes=MOE_VMEM_LIMIT),
        name="expert_swiglu",
    )(cnt, hn, post, pos, comb, wg, wu, wd)


def _add_norm_kernel(h_ref, d_ref, g_ref, h_out_ref, hn_out_ref):
    h = h_ref[...] + d_ref[...]
    h_out_ref[...] = h
    hn_out_ref[...] = _rms(h, g_ref[...]).astype(hn_out_ref.dtype)


def _add_norm(h, delta, g, tm):
    n, d = h.shape
    row = pl.BlockSpec((tm, d), lambda i: (i, 0))
    return pl.pallas_call(
        _add_norm_kernel,
        grid=(n // tm,),
        in_specs=[row, row, pl.BlockSpec((1, d), lambda i: (0, 0))],
        out_specs=[row, row],
        out_shape=[jax.ShapeDtypeStruct((n, d), F32), jax.ShapeDtypeStruct((n, d), BF16)],
        compiler_params=_cparams(("parallel",)),
        name="residual_norm",
    )(h, delta, g)


def _pad_lanes(a, width=LANES):
    return jnp.pad(a, ((0, 0), (0, width - a.shape[-1])))


def kernel(x, meta, norm_mix, w_in, b_in, q_norm, k_norm, w_attn_o, conv_w, conv_b, conv_ln_g, conv_ln_b,
           w_conv_o, pool_w, pool_scale, w_pool_o, w_out, norm_ffn, w_ff_gate, w_ff_up, w_ff_down, w_router,
           b_router, w_e_gate, w_e_up, w_e_down):
    bsz, seq, d = x.shape
    depth = w_in.shape[0]
    length = seq + N_META
    lp = -(-length // SEQ_ALIGN) * SEQ_ALIGN
    n = bsz * lp

    tm = _pick(n, (1536, 768, 512, 256))
    tm_mid = _pick(n, (768, 512, 256))
    tm_ffn = _pick(n, (768, 512, 256))
    tm_moe = _pick(n, (1408, 768, 512, 256))
    sb_router = _pick(tm_moe, (704, 768, 512, 256))
    tn = _pick(Z_W, (1792, 768, 256))
    ts = _pick(lp, (768, 512, 256))
    ts_f = ts
    tq = _pick(lp, (256,))
    tk = 256
    tf_dense = _pick(w_ff_gate.shape[2], (256,))
    tf_moe = _pick(w_e_gate.shape[3], (512, 256))
    sub_moe = _pick(tf_moe, (256,))
    quarter = tm_moe // 4
    moe_sizes = tuple(range(max(quarter - 96, 64) // 64 * 64, quarter + 161, 64))

    h = jnp.concatenate([jnp.broadcast_to(meta[None].astype(x.dtype), (bsz, N_META, d)), x], axis=1)
    h = jnp.pad(h, ((0, 0), (0, lp - length), (0, 0))).reshape(n, d)
    hn = _norm(h, norm_mix[0][None], tm)

    att_scale = HEAD_DIM ** -0.5
    gate_lo = 3 * ATT_W + ATT_HEADS + 2 * CONV_W + POOL_W
    f_lo = 3 * ATT_W
    for l in range(depth):
        wl = w_in[l]
        w_main = jnp.concatenate([wl[:, gate_lo:], wl[:, :f_lo], wl[:, f_lo + ATT_HEADS:gate_lo]], axis=1).astype(BF16)
        bl = b_in[l]
        b_main = jnp.concatenate([bl[gate_lo:], bl[:f_lo], bl[f_lo + ATT_HEADS:gate_lo]])[None]
        w_f = _pad_lanes(wl[:, f_lo:f_lo + ATT_HEADS])
        w_f_hi = w_f.astype(BF16)
        w_f = jnp.concatenate([w_f_hi, (w_f - w_f_hi.astype(F32)).astype(BF16)], axis=1)
        b_f = _pad_lanes(bl[None, f_lo:f_lo + ATT_HEADS])

        z = _inproj(hn, w_main, b_main, tm, tn)
        z3 = z.reshape(bsz, lp, Z_W)
        fcol = _forget(hn.reshape(bsz, lp, d), w_f, b_f, ts_f)
        frow = jnp.transpose(fcol[:, :, :ATT_HEADS], (0, 2, 1))
        qg = jnp.tile(q_norm[l], 2)[None] * att_scale
        kg = jnp.tile(k_norm[l], 2)[None]
        att = _attention(z3, fcol, frow, qg, kg, tq, tk)

        cw = jnp.pad(conv_w[l], ((0, HALO - CONV_K), (0, 0)))
        pw = jax.scipy.linalg.block_diag(*[pool_w[l, g] for g in range(pool_w.shape[1])]).astype(BF16)
        u, pm = _convpool(z3, cw, conv_b[l][None], conv_ln_g[l][None], conv_ln_b[l][None], pw,
                          pool_scale[l][None], ts)

        h, hn = _merge(att.reshape(n, ATT_W), u.reshape(n, CONV_W), pm.reshape(n, POOL_W), z, h,
                       w_attn_o[l].astype(BF16), w_conv_o[l].astype(BF16), w_pool_o[l].astype(BF16),
                       w_out[l].astype(BF16), norm_ffn[l][None], tm_mid)

        ng_next = norm_mix[min(l + 1, depth - 1)][None]
        i = l // 2
        if l % 2 == 0:
            h, hn = _ffn(hn, h, w_ff_gate[i].astype(BF16), w_ff_up[i].astype(BF16), w_ff_down[i].astype(BF16),
                         ng_next, tm_ffn, tf_dense)
        else:
            comb, pos, cnt = _router(h, norm_ffn[l][None], _pad_lanes(w_router[i]), _pad_lanes(b_router[i][None]),
                                     tm_moe, sb_router)
            post = jnp.transpose(pos[:, :POST_ROWS].reshape(n // tm_moe, tm_moe, POST_ROWS), (0, 2, 1))
            cnt_flat = cnt[:, 0, :N_EXPERTS].reshape(-1)
            ne, _, dffe = w_e_gate[i].shape
            nf = dffe // tf_moe
            wg = w_e_gate[i].reshape(ne, d, nf, tf_moe).transpose(0, 2, 1, 3).astype(BF16)
            wu = w_e_up[i].reshape(ne, d, nf, tf_moe).transpose(0, 2, 1, 3).astype(BF16)
            wd = w_e_down[i].reshape(ne, nf, tf_moe, d).astype(BF16)
            delta = _moe(cnt_flat, hn, post, pos, comb, wg, wu, wd, tm_moe, moe_sizes, sub_moe)
            h, hn = _add_norm(h, delta, ng_next, tm)

    return h.reshape(bsz, lp, d)[:, N_META:length]
```

```python
import functools

import jax
import jax.numpy as jnp
from jax import lax
from jax.experimental import pallas as pl
from jax.experimental.pallas import tpu as pltpu

F32 = jnp.float32
BF16 = jnp.bfloat16
HIGHEST = lax.Precision.HIGHEST

D_MODEL = 1024
N_META = 16
HEAD_DIM = 64
ATT_W = 512
ATT_HEADS = 8
CONV_W = 256
CONV_K = 31
POOL_W = 256
POOL_WINDOWS = (2, 4, 8, 16)
N_EXPERTS = 8
LANES = 128
HALO = 32
NEG_INF = -1e30
EXP_UNDERFLOW = 110.0
SEQ_ALIGN = 256
VMEM_LIMIT = 56 * 1024 * 1024
MOE_VMEM_LIMIT = 60 * 1024 * 1024
POST_ROWS = 32

Z_GATE = 0
Z_Q = 3072
Z_K = 3584
Z_V = 4096
Z_A = 4608
Z_G = 4864
Z_P = 5120
Z_W = 5376


def _pick(n, candidates):
    for c in candidates:
        if n % c == 0:
            return c
    raise ValueError(f"no tile for {n} in {candidates}")


def _cparams(sem):
    return pltpu.CompilerParams(dimension_semantics=sem, vmem_limit_bytes=VMEM_LIMIT)


def _rms(x, g, eps=1e-6):
    return x * lax.rsqrt(jnp.mean(x * x, axis=-1, keepdims=True) + eps) * g


def _sigmoid(x):
    return 1.0 / (1.0 + jnp.exp(-x))


def _silu(x):
    return x * _sigmoid(x)


def _norm_kernel(h_ref, g_ref, o_ref):
    o_ref[...] = _rms(h_ref[...], g_ref[...]).astype(o_ref.dtype)


def _norm(h, g, tm):
    n, d = h.shape
    return pl.pallas_call(
        _norm_kernel,
        grid=(n // tm,),
        in_specs=[pl.BlockSpec((tm, d), lambda i: (i, 0)), pl.BlockSpec((1, d), lambda i: (0, 0))],
        out_specs=pl.BlockSpec((tm, d), lambda i: (i, 0)),
        out_shape=jax.ShapeDtypeStruct((n, d), BF16),
        compiler_params=_cparams(("parallel",)),
        name="rmsnorm",
    )(h, g)


def _inproj_kernel(x_ref, w_ref, b_ref, o_ref):
    acc = jnp.dot(x_ref[...], w_ref[...], preferred_element_type=F32)
    o_ref[...] = (acc + b_ref[...]).astype(o_ref.dtype)


def _inproj(hn, w, b, tm, tn):
    n, d = hn.shape
    zw = w.shape[1]
    return pl.pallas_call(
        _inproj_kernel,
        grid=(n // tm, zw // tn),
        in_specs=[pl.BlockSpec((tm, d), lambda i, j: (i, 0)),
                  pl.BlockSpec((d, tn), lambda i, j: (0, j)),
                  pl.BlockSpec((1, tn), lambda i, j: (0, j))],
        out_specs=pl.BlockSpec((tm, tn), lambda i, j: (i, j)),
        out_shape=jax.ShapeDtypeStruct((n, zw), BF16),
        compiler_params=_cparams(("parallel", "arbitrary")),
        name="inproj",
    )(hn, w, b)


def _forget_kernel(x_ref, w_ref, b_ref, o_ref, carry_ref):
    @pl.when(pl.program_id(1) == 0)
    def _():
        carry_ref[...] = jnp.zeros_like(carry_ref)

    fw = jnp.dot(x_ref[0], w_ref[...], preferred_element_type=F32)
    f = fw[:, :LANES] + fw[:, LANES:] + b_ref[...]
    ls = jnp.minimum(f, 0.0) - jnp.log(1.0 + jnp.exp(-jnp.abs(f)))
    hi = ls.astype(BF16)
    r1 = ls - hi.astype(F32)
    mid = r1.astype(BF16)
    lo = (r1 - mid.astype(F32)).astype(BF16)
    t = ls.shape[0]
    r = lax.broadcasted_iota(jnp.int32, (t, t), 0)
    c = lax.broadcasted_iota(jnp.int32, (t, t), 1)
    tril = jnp.where(c <= r, 1.0, 0.0).astype(BF16)
    parts = jnp.dot(tril, jnp.concatenate([hi, mid, lo], axis=-1), preferred_element_type=F32)
    cs = (parts[:, :LANES] + parts[:, LANES:2 * LANES]) + parts[:, 2 * LANES:] + carry_ref[...]
    o_ref[0] = cs
    carry_ref[...] = cs[t - 1:t, :]


def _forget(hn3, w_f, b_f, ts):
    b, lp, d = hn3.shape
    return pl.pallas_call(
        _forget_kernel,
        grid=(b, lp // ts),
        in_specs=[pl.BlockSpec((1, ts, d), lambda bi, t: (bi, t, 0)),
                  pl.BlockSpec((d, 2 * LANES), lambda bi, t: (0, 0)),
                  pl.BlockSpec((1, LANES), lambda bi, t: (0, 0))],
        out_specs=pl.BlockSpec((1, ts, LANES), lambda bi, t: (bi, t, 0)),
        out_shape=jax.ShapeDtypeStruct((b, lp, LANES), F32),
        scratch_shapes=[pltpu.VMEM((1, LANES), F32)],
        compiler_params=_cparams(("parallel", "arbitrary")),
        name="forget_cumsum",
    )(hn3, w_f, b_f)


def _pair_rms(x, first_head, gain, eps=1e-6):
    sq = x * x
    s0 = jnp.sum(jnp.where(first_head, sq, 0.0), axis=-1, keepdims=True)
    s1 = jnp.sum(jnp.where(first_head, 0.0, sq), axis=-1, keepdims=True)
    ms = jnp.where(first_head, s0, s1) * (1.0 / HEAD_DIM)
    return x * lax.rsqrt(ms + eps) * gain


def _attn_kernel(lo_ref, q_ref, k_ref, v_ref, fc_ref, fr_ref, qg_ref, kg_ref, o_ref, kn_ref, *, tq, tk):
    bi = pl.program_id(0)
    p = pl.program_id(1)
    i = pl.program_id(2)
    nq = pl.num_programs(2)
    lp = k_ref.shape[1]
    lane = lax.broadcasted_iota(jnp.int32, (1, LANES), 1)
    first_head = lane < HEAD_DIM

    @pl.when(i == 0)
    def _():
        def body(c, carry):
            off = pl.multiple_of(c * tk, tk)
            kk = k_ref[0, pl.ds(off, tk), :].astype(F32)
            kn_ref[pl.ds(off, tk), :] = _pair_rms(kk, first_head, kg_ref[...]).astype(BF16)
            return carry
        lax.fori_loop(0, lp // tk, body, 0)

    qn = _pair_rms(q_ref[0].astype(F32), first_head, qg_ref[...])
    fc = fc_ref[0]
    row_pos = i * tq + lax.broadcasted_iota(jnp.int32, (tq, 1), 0)

    heads = [2 * p, 2 * p + 1]
    qhs = [jnp.where(first_head, qn, 0.0).astype(BF16), jnp.where(first_head, 0.0, qn).astype(BF16)]
    fts = [jnp.sum(jnp.where(lane == hd, fc, 0.0), axis=-1, keepdims=True) for hd in heads]

    def update(off, width, carry, hh, masked):
        m, l, acc = carry
        ks = kn_ref[pl.ds(off, width), :]
        s = lax.dot_general(qhs[hh], ks, (((1,), (1,)), ((), ())), preferred_element_type=F32)
        fs = fr_ref[0, pl.ds(heads[hh], 1), pl.ds(off, width)]
        s = s + (fts[hh] - fs)
        if masked:
            col_pos = off + lax.broadcasted_iota(jnp.int32, (1, width), 1)
            s = jnp.where(col_pos <= row_pos, s, NEG_INF)
        m_new = jnp.maximum(m, jnp.max(s, axis=-1, keepdims=True))
        alpha = jnp.exp(m - m_new)
        pm = jnp.exp(s - m_new)
        l = alpha * l + jnp.sum(pm, axis=-1, keepdims=True)
        vs = v_ref[0, pl.ds(off, width), :]
        acc = alpha * acc + jnp.dot(pm.astype(BF16), vs, preferred_element_type=F32)
        return m_new, l, acc

    win = jnp.maximum(i - 1, 0)
    firsts = [lo_ref[(bi * ATT_HEADS + hd) * nq + i] for hd in heads]
    init = (jnp.full((tq, 1), NEG_INF, F32), jnp.zeros((tq, 1), F32), jnp.zeros((tq, LANES), F32))

    def far(j, carry):
        off = pl.multiple_of(j * tk, tk)
        return tuple(update(off, tk, carry[hh], hh, masked=False) for hh in range(2))

    carry = lax.fori_loop(jnp.minimum(firsts[0], firsts[1]), win, far, (init, init))
    off = pl.multiple_of(win * tk, tk)
    outs = []
    for hh in range(2):
        m, l, acc = update(off, 2 * tk, carry[hh], hh, masked=True)
        outs.append(acc / l)
    o_ref[0] = jnp.where(first_head, outs[0], outs[1]).astype(o_ref.dtype)


def _first_live_chunk(frow, qg, kg, tq, tk):
    b, nh, lp = frow.shape
    qk_bound = 1.02 * HEAD_DIM * jnp.max(jnp.abs(qg)) * jnp.max(jnp.abs(kg))
    f_first = frow[:, :, 0::tq]
    f_last = frow[:, :, tk - 1::tk]
    dead = (f_first[:, :, :, None] - f_last[:, :, None, :] + 2.0 * qk_bound) < -EXP_UNDERFLOW
    n_dead = jnp.sum(dead.astype(jnp.int32), axis=-1)
    n_full = (jnp.arange(lp // tq, dtype=jnp.int32) * tq) // tk
    return jnp.minimum(n_dead, n_full[None, None, :]).reshape(-1)


def _attention(z3, fcol, frow, qg, kg, tq, tk):
    b, lp, _ = z3.shape
    npairs = ATT_HEADS // 2
    qb, kb, vb = Z_Q // LANES, Z_K // LANES, Z_V // LANES
    assert tq == tk and lp >= 2 * tk, (tq, tk, lp)
    first = _first_live_chunk(frow, qg, kg, tq, tk)
    grid_spec = pltpu.PrefetchScalarGridSpec(
        num_scalar_prefetch=1,
        grid=(b, npairs, lp // tq),
        in_specs=[pl.BlockSpec((1, tq, LANES), lambda bi, p, i, lo: (bi, i, qb + p)),
                  pl.BlockSpec((1, lp, LANES), lambda bi, p, i, lo: (bi, 0, kb + p)),
                  pl.BlockSpec((1, lp, LANES), lambda bi, p, i, lo: (bi, 0, vb + p)),
                  pl.BlockSpec((1, tq, LANES), lambda bi, p, i, lo: (bi, i, 0)),
                  pl.BlockSpec((1, ATT_HEADS, lp), lambda bi, p, i, lo: (bi, 0, 0)),
                  pl.BlockSpec((1, LANES), lambda bi, p, i, lo: (0, 0)),
                  pl.BlockSpec((1, LANES), lambda bi, p, i, lo: (0, 0))],
        out_specs=pl.BlockSpec((1, tq, LANES), lambda bi, p, i, lo: (bi, i, p)),
        scratch_shapes=[pltpu.VMEM((lp, LANES), BF16)],
    )
    return pl.pallas_call(
        functools.partial(_attn_kernel, tq=tq, tk=tk),
        grid_spec=grid_spec,
        out_shape=jax.ShapeDtypeStruct((b, lp, ATT_W), BF16),
        compiler_params=_cparams(("parallel", "parallel", "arbitrary")),
        name="fox_attention",
    )(first, z3, z3, z3, fcol, frow, qg, kg)


def _convpool_kernel(a_ref, g_ref, p_ref, ah_ref, gh_ref, ph_ref, cw_ref, cb_ref, lg_ref, lb_ref,
                     pw_ref, ps_ref, u_ref, pm_ref, ext_ref, pext_ref, *, ts):
    i = pl.program_id(1)
    has_prev = i > 0

    u = a_ref[0].astype(F32) * _sigmoid(g_ref[0].astype(F32))
    uh = ah_ref[0].astype(F32) * _sigmoid(gh_ref[0].astype(F32))
    ext_ref[0:HALO, :] = jnp.where(has_prev, uh, 0.0)
    ext_ref[HALO:HALO + ts, :] = u
    acc = jnp.zeros((ts, CONV_W), F32) + cb_ref[...]
    for j in range(CONV_K):
        start = HALO - (CONV_K - 1) + j
        acc = acc + cw_ref[j:j + 1, :] * ext_ref[start:start + ts, :]
    mu = jnp.mean(acc, axis=-1, keepdims=True)
    cen = acc - mu
    var = jnp.mean(cen * cen, axis=-1, keepdims=True)
    y = cen * lax.rsqrt(var + 1e-5) * lg_ref[...] + lb_ref[...]
    u_ref[0] = _silu(y).astype(u_ref.dtype)

    x = p_ref[0].astype(F32)
    pext_ref[0:HALO, :] = jnp.where(has_prev, ph_ref[0].astype(F32), 0.0)
    pext_ref[HALO:HALO + ts, :] = x
    pos1 = (i * ts + 1 + lax.broadcasted_iota(jnp.int32, (ts, 1), 0)).astype(F32)
    lane = lax.broadcasted_iota(jnp.int32, (1, POOL_W), 1)
    group_w = POOL_W // len(POOL_WINDOWS)
    run = x
    pooled = jnp.zeros((ts, POOL_W), F32)
    for k in range(1, max(POOL_WINDOWS)):
        run = run + pext_ref[HALO - k:HALO - k + ts, :]
        if (k + 1) in POOL_WINDOWS:
            gi = POOL_WINDOWS.index(k + 1)
            mean = run / jnp.minimum(pos1, float(k + 1))
            in_group = (lane >= gi * group_w) & (lane < (gi + 1) * group_w)
            pooled = jnp.where(in_group, mean, pooled)
    pm = (pooled - x).astype(BF16)
    lin = jnp.dot(pm, pw_ref[...], preferred_element_type=F32) * ps_ref[...]
    pm_ref[0] = lin.astype(pm_ref.dtype)


def _convpool(z3, cw, cb, lg, lb, pw, ps, ts):
    b, lp, _ = z3.shape
    ab, gb, pb = Z_A // CONV_W, Z_G // CONV_W, Z_P // POOL_W
    hpt = ts // HALO
    main = lambda blk: pl.BlockSpec((1, ts, CONV_W), lambda bi, i: (bi, i, blk))
    halo = lambda blk: pl.BlockSpec((1, HALO, CONV_W), lambda bi, i: (bi, jnp.maximum(i * hpt - 1, 0), blk))
    const = lambda shape: pl.BlockSpec(shape, lambda bi, i: (0, 0))
    out = pl.BlockSpec((1, ts, CONV_W), lambda bi, i: (bi, i, 0))
    return pl.pallas_call(
        functools.partial(_convpool_kernel, ts=ts),
        grid=(b, lp // ts),
        in_specs=[main(ab), main(gb), main(pb), halo(ab), halo(gb), halo(pb),
                  const((HALO, CONV_W)), const((1, CONV_W)), const((1, CONV_W)), const((1, CONV_W)),
                  const((POOL_W, POOL_W)), const((1, POOL_W))],
        out_specs=[out, out],
        out_shape=[jax.ShapeDtypeStruct((b, lp, CONV_W), BF16), jax.ShapeDtypeStruct((b, lp, POOL_W), BF16)],
        scratch_shapes=[pltpu.VMEM((HALO + ts, CONV_W), F32), pltpu.VMEM((HALO + ts, POOL_W), F32)],
        compiler_params=_cparams(("parallel", "arbitrary")),
        name="conv_pool",
    )(z3, z3, z3, z3, z3, z3, cw, cb, lg, lb, pw, ps)


def _merge_kernel(att_ref, u_ref, pm_ref, g0_ref, g1_ref, g2_ref, h_ref, wa_ref, wc_ref, wp_ref, wo_ref,
                  ng_ref, h_out_ref, hn_out_ref):
    ya = jnp.dot(att_ref[...], wa_ref[...], preferred_element_type=F32)
    yc = jnp.dot(u_ref[...], wc_ref[...], preferred_element_type=F32)
    yp = jnp.dot(pm_ref[...], wp_ref[...], preferred_element_type=F32)
    m = (_sigmoid(g0_ref[...].astype(F32)) * ya + _sigmoid(g1_ref[...].astype(F32)) * yc
         + _sigmoid(g2_ref[...].astype(F32)) * yp)
    h = h_ref[...] + jnp.dot(m.astype(BF16), wo_ref[...], preferred_element_type=F32)
    h_out_ref[...] = h
    hn_out_ref[...] = _rms(h, ng_ref[...]).astype(hn_out_ref.dtype)


def _merge(att, u, pm, z, h, wa, wc, wp, wo, ng, tm):
    n, d = h.shape
    row = lambda w, blk=0: pl.BlockSpec((tm, w), lambda i: (i, blk))
    const = lambda shape: pl.BlockSpec(shape, lambda i: (0, 0))
    return pl.pallas_call(
        _merge_kernel,
        grid=(n // tm,),
        in_specs=[row(ATT_W), row(CONV_W), row(POOL_W), row(d, 0), row(d, 1), row(d, 2), row(d),
                  const(wa.shape), const(wc.shape), const(wp.shape), const(wo.shape), const((1, d))],
        out_specs=[row(d), row(d)],
        out_shape=[jax.ShapeDtypeStruct((n, d), F32), jax.ShapeDtypeStruct((n, d), BF16)],
        compiler_params=_cparams(("parallel",)),
        name="merge_outproj",
    )(att, u, pm, z, z, z, h, wa, wc, wp, wo, ng)


def _ffn_kernel(x_ref, h_ref, wg_ref, wu_ref, wd_ref, ng_ref, h_out_ref, hn_out_ref, *, tf):
    x = x_ref[...]
    h = h_ref[...]
    for c in range(wg_ref.shape[1] // tf):
        gt = jnp.dot(x, wg_ref[:, c * tf:(c + 1) * tf], preferred_element_type=F32)
        up = jnp.dot(x, wu_ref[:, c * tf:(c + 1) * tf], preferred_element_type=F32)
        h = h + jnp.dot((_silu(gt) * up).astype(BF16), wd_ref[c * tf:(c + 1) * tf, :],
                        preferred_element_type=F32)
    h_out_ref[...] = h
    hn_out_ref[...] = _rms(h, ng_ref[...]).astype(hn_out_ref.dtype)


def _ffn(hn, h, wg, wu, wd, ng, tm, tf):
    n, d = h.shape
    row = pl.BlockSpec((tm, d), lambda i: (i, 0))
    const = lambda shape: pl.BlockSpec(shape, lambda i: (0, 0), pipeline_mode=pl.Buffered(1))
    return pl.pallas_call(
        functools.partial(_ffn_kernel, tf=tf),
        grid=(n // tm,),
        in_specs=[row, row, const(wg.shape), const(wu.shape), const(wd.shape), const((1, d))],
        out_specs=[row, row],
        out_shape=[jax.ShapeDtypeStruct((n, d), F32), jax.ShapeDtypeStruct((n, d), BF16)],
        compiler_params=_cparams(("parallel",)),
        name="dense_swiglu",
    )(hn, h, wg, wu, wd, ng)


def _router_kernel(h_ref, ng_ref, wr_ref, br_ref, comb_ref, pos_ref, cnt_ref, *, sb):
    tm = h_ref.shape[0]
    hn = _rms(h_ref[...], ng_ref[...])
    logits = jnp.dot(hn, wr_ref[...], precision=HIGHEST, preferred_element_type=F32) + br_ref[...]
    lane = lax.broadcasted_iota(jnp.int32, (1, LANES), 1).astype(F32)
    lg = jnp.where(lane < N_EXPERTS, logits, -jnp.inf)
    m1 = jnp.max(lg, axis=-1, keepdims=True)
    i1 = jnp.min(jnp.where(lg == m1, lane, float(LANES)), axis=-1, keepdims=True)
    sel1 = lane == i1
    lg2 = jnp.where(sel1, -jnp.inf, lg)
    m2 = jnp.max(lg2, axis=-1, keepdims=True)
    i2 = jnp.min(jnp.where(lg2 == m2, lane, float(LANES)), axis=-1, keepdims=True)
    sel2 = lane == i2
    e = jnp.exp(m2 - m1)
    g1 = 1.0 / (1.0 + e)
    comb_ref[...] = jnp.where(sel1, g1, 0.0) + jnp.where(sel2, e * g1, 0.0)
    sel = jnp.where(sel1 | sel2, 1.0, 0.0)
    r = lax.broadcasted_iota(jnp.int32, (sb, sb), 0)
    c = lax.broadcasted_iota(jnp.int32, (sb, sb), 1)
    tril = jnp.where(c <= r, 1.0, 0.0).astype(BF16)
    carry = jnp.zeros((1, LANES), F32)
    for s in range(tm // sb):
        blk = slice(s * sb, (s + 1) * sb)
        incl = jnp.dot(tril, sel[blk].astype(BF16), preferred_element_type=F32) + carry
        pos_ref[blk, :] = jnp.where(sel[blk] > 0.0, incl - 1.0, -1.0)
        carry = incl[sb - 1:sb, :]
    cnt_ref[0] = jnp.broadcast_to(carry, (8, LANES)).astype(jnp.int32)


def _router(h, ng, wr, br, tm, sb):
    n, d = h.shape
    nt = n // tm
    return pl.pallas_call(
        functools.partial(_router_kernel, sb=sb),
        grid=(nt,),
        in_specs=[pl.BlockSpec((tm, d), lambda i: (i, 0)),
                  pl.BlockSpec((1, d), lambda i: (0, 0)),
                  pl.BlockSpec((d, LANES), lambda i: (0, 0)),
                  pl.BlockSpec((1, LANES), lambda i: (0, 0))],
        out_specs=[pl.BlockSpec((tm, LANES), lambda i: (i, 0)),
                   pl.BlockSpec((tm, LANES), lambda i: (i, 0)),
                   pl.BlockSpec((1, 8, LANES), lambda i: (i, 0, 0))],
        out_shape=[jax.ShapeDtypeStruct((n, LANES), F32), jax.ShapeDtypeStruct((n, LANES), F32),
                   jax.ShapeDtypeStruct((nt, 8, LANES), jnp.int32)],
        compiler_params=_cparams(("parallel",)),
        name="router_top2",
    )(h, ng, wr, br)


def _moe_kernel(cnt_ref, x_ref, post_ref, pos_ref, comb_ref, wg_ref, wu_ref, wd_ref,
                out_ref, xs_ref, ys_ref, *, sizes, sub):
    i = pl.program_id(0)
    e = pl.program_id(1)
    f = pl.program_id(2)
    nf = pl.num_programs(2)
    cnt = cnt_ref[i * N_EXPERTS + e]
    big = sizes[-1]

    def for_each_chunk(body):
        lo = 0
        for r in sizes:
            @pl.when((cnt > lo) & (cnt <= r))
            def _(r=r):
                body(0, r)
            lo = r

        @pl.when(cnt > big)
        def _():
            def step(c, carry):
                body(pl.multiple_of(c * big, big), big)
                return carry
            lax.fori_loop(0, (cnt + big - 1) // big, step, 0)

    @pl.when((e == 0) & (f == 0))
    def _():
        out_ref[...] = jnp.zeros_like(out_ref)

    @pl.when(f == 0)
    def _():
        prow = post_ref[0, pl.ds(e, 1), :]

        def gather(off, r):
            tgt = (off + lax.broadcasted_iota(jnp.int32, (r, 1), 0)).astype(F32)
            onehot = jnp.where(prow == tgt, 1.0, 0.0).astype(BF16)
            xs_ref[pl.ds(off, r), :] = jnp.dot(onehot, x_ref[...], preferred_element_type=F32).astype(BF16)
            ys_ref[pl.ds(off, r), :] = jnp.zeros((r, D_MODEL), F32)
        for_each_chunk(gather)

    def expert(off, r):
        xc = xs_ref[pl.ds(off, r), :]
        y = ys_ref[pl.ds(off, r), :]
        for s in range(wg_ref.shape[3] // sub):
            cols = slice(s * sub, (s + 1) * sub)
            gt = jnp.dot(xc, wg_ref[0, 0, :, cols], preferred_element_type=F32)
            up = jnp.dot(xc, wu_ref[0, 0, :, cols], preferred_element_type=F32)
            y = y + jnp.dot((_silu(gt) * up).astype(BF16), wd_ref[0, 0, cols, :], preferred_element_type=F32)
        ys_ref[pl.ds(off, r), :] = y
    for_each_chunk(expert)

    @pl.when(f == nf - 1)
    def _():
        lane = lax.broadcasted_iota(jnp.int32, (1, LANES), 1)
        pcol = jnp.sum(jnp.where(lane == e, pos_ref[...], 0.0), axis=-1, keepdims=True)
        gcol = jnp.sum(jnp.where(lane == e, comb_ref[...], 0.0), axis=-1, keepdims=True)

        def scatter(off, r):
            tgt = (off + lax.broadcasted_iota(jnp.int32, (1, r), 1)).astype(F32)
            onehot_t = jnp.where(pcol == tgt, 1.0, 0.0).astype(BF16)
            y = ys_ref[pl.ds(off, r), :].astype(BF16)
            out_ref[...] += gcol * jnp.dot(onehot_t, y, preferred_element_type=F32)
        for_each_chunk(scatter)


def _moe(cnt, hn, post, pos, comb, wg, wu, wd, tm, sizes, sub):
    n, d = hn.shape
    nf, tf = wg.shape[1], wg.shape[3]
    cap = -(-tm // sizes[-1]) * sizes[-1]
    once = pl.Buffered(1)
    row = lambda w: pl.BlockSpec((tm, w), lambda i, e, f, cnt: (i, 0), pipeline_mode=once)
    grid_spec = pltpu.PrefetchScalarGridSpec(
        num_scalar_prefetch=1,
        grid=(n // tm, N_EXPERTS, nf),
        in_specs=[row(d),
                  pl.BlockSpec((1, POST_ROWS, tm), lambda i, e, f, cnt: (i, 0, 0), pipeline_mode=once),
                  row(LANES), row(LANES),
                  pl.BlockSpec((1, 1, d, tf), lambda i, e, f, cnt: (e, f, 0, 0)),
                  pl.BlockSpec((1, 1, d, tf), lambda i, e, f, cnt: (e, f, 0, 0)),
                  pl.BlockSpec((1, 1, tf, d), lambda i, e, f, cnt: (e, f, 0, 0))],
        out_specs=pl.BlockSpec((tm, d), lambda i, e, f, cnt: (i, 0)),
        scratch_shapes=[pltpu.VMEM((cap, d), BF16), pltpu.VMEM((cap, d), F32)],
    )
    return pl.pallas_call(
        functools.partial(_moe_kernel, sizes=sizes, sub=sub),
        grid_spec=grid_spec,
        out_shape=jax.ShapeDtypeStruct((n, d), F32),
        compiler_params=pltpu.CompilerParams(dimension_semantics=("parallel", "arbitrary", "arbitrary"),
                                             vmem_limit_bytes=MOE_VMEM_LIMIT),
        name="expert_swiglu",
    )(cnt, hn, post, pos, comb, wg, wu, wd)


def _add_norm_kernel(h_ref, d_ref, g_ref, h_out_ref, hn_out_ref):
    h = h_ref[...] + d_ref[...]
    h_out_ref[...] = h
    hn_out_ref[...] = _rms(h, g_ref[...]).astype(hn_out_ref.dtype)


def _add_norm(h, delta, g, tm):
    n, d = h.shape
    row = pl.BlockSpec((tm, d), lambda i: (i, 0))
    return pl.pallas_call(
        _add_norm_kernel,
        grid=(n // tm,),
        in_specs=[row, row, pl.BlockSpec((1, d), lambda i: (0, 0))],
        out_specs=[row, row],
        out_shape=[jax.ShapeDtypeStruct((n, d), F32), jax.ShapeDtypeStruct((n, d), BF16)],
        compiler_params=_cparams(("parallel",)),
        name="residual_norm",
    )(h, delta, g)


def _pad_lanes(a, width=LANES):
    return jnp.pad(a, ((0, 0), (0, width - a.shape[-1])))


def kernel(x, meta, norm_mix, w_in, b_in, q_norm, k_norm, w_attn_o, conv_w, conv_b, conv_ln_g, conv_ln_b,
           w_conv_o, pool_w, pool_scale, w_pool_o, w_out, norm_ffn, w_ff_gate, w_ff_up, w_ff_down, w_router,
           b_router, w_e_gate, w_e_up, w_e_down):
    bsz, seq, d = x.shape
    depth = w_in.shape[0]
    length = seq + N_META
    lp = -(-length // SEQ_ALIGN) * SEQ_ALIGN
    n = bsz * lp

    tm = _pick(n, (1536, 768, 512, 256))
    tm_mid = _pick(n, (768, 512, 256))
    tm_ffn = _pick(n, (768, 512, 256))
    tm_moe = _pick(n, (1408, 768, 512, 256))
    sb_router = _pick(tm_moe, (704, 768, 512, 256))
    tn = _pick(Z_W, (1792, 768, 256))
    ts = _pick(lp, (768, 512, 256))
    ts_f = ts
    tq = _pick(lp, (256,))
    tk = 256
    tf_dense = _pick(w_ff_gate.shape[2], (256,))
    tf_moe = _pick(w_e_gate.shape[3], (1792, 512, 256))
    sub_moe = _pick(tf_moe, (256,))
    quarter = tm_moe // 4
    moe_sizes = tuple(range(max(quarter - 96, 64) // 64 * 64, quarter + 161, 64))

    h = jnp.concatenate([jnp.broadcast_to(meta[None].astype(x.dtype), (bsz, N_META, d)), x], axis=1)
    h = jnp.pad(h, ((0, 0), (0, lp - length), (0, 0))).reshape(n, d)
    hn = _norm(h, norm_mix[0][None], tm)

    att_scale = HEAD_DIM ** -0.5
    gate_lo = 3 * ATT_W + ATT_HEADS + 2 * CONV_W + POOL_W
    f_lo = 3 * ATT_W
    for l in range(depth):
        wl = w_in[l]
        w_main = jnp.concatenate([wl[:, gate_lo:], wl[:, :f_lo], wl[:, f_lo + ATT_HEADS:gate_lo]], axis=1).astype(BF16)
        bl = b_in[l]
        b_main = jnp.concatenate([bl[gate_lo:], bl[:f_lo], bl[f_lo + ATT_HEADS:gate_lo]])[None]
        w_f = _pad_lanes(wl[:, f_lo:f_lo + ATT_HEADS])
        w_f_hi = w_f.astype(BF16)
        w_f = jnp.concatenate([w_f_hi, (w_f - w_f_hi.astype(F32)).astype(BF16)], axis=1)
        b_f = _pad_lanes(bl[None, f_lo:f_lo + ATT_HEADS])

        z = _inproj(hn, w_main, b_main, tm, tn)
        z3 = z.reshape(bsz, lp, Z_W)
        fcol = _forget(hn.reshape(bsz, lp, d), w_f, b_f, ts_f)
        frow = jnp.transpose(fcol[:, :, :ATT_HEADS], (0, 2, 1))
        qg = jnp.tile(q_norm[l], 2)[None] * att_scale
        kg = jnp.tile(k_norm[l], 2)[None]
        att = _attention(z3, fcol, frow, qg, kg, tq, tk)

        cw = jnp.pad(conv_w[l], ((0, HALO - CONV_K), (0, 0)))
        pw = jax.scipy.linalg.block_diag(*[pool_w[l, g] for g in range(pool_w.shape[1])]).astype(BF16)
        u, pm = _convpool(z3, cw, conv_b[l][None], conv_ln_g[l][None], conv_ln_b[l][None], pw,
                          pool_scale[l][None], ts)

        h, hn = _merge(att.reshape(n, ATT_W), u.reshape(n, CONV_W), pm.reshape(n, POOL_W), z, h,
                       w_attn_o[l].astype(BF16), w_conv_o[l].astype(BF16), w_pool_o[l].astype(BF16),
                       w_out[l].astype(BF16), norm_ffn[l][None], tm_mid)

        ng_next = norm_mix[min(l + 1, depth - 1)][None]
        i = l // 2
        if l % 2 == 0:
            h, hn = _ffn(hn, h, w_ff_gate[i].astype(BF16), w_ff_up[i].astype(BF16), w_ff_down[i].astype(BF16),
                         ng_next, tm_ffn, tf_dense)
        else:
            comb, pos, cnt = _router(h, norm_ffn[l][None], _pad_lanes(w_router[i]), _pad_lanes(b_router[i][None]),
                                     tm_moe, sb_router)
            post = jnp.transpose(pos[:, :POST_ROWS].reshape(n // tm_moe, tm_moe, POST_ROWS), (0, 2, 1))
            cnt_flat = cnt[:, 0, :N_EXPERTS].reshape(-1)
            ne, _, dffe = w_e_gate[i].shape
            nf = dffe // tf_moe
            wg = w_e_gate[i].reshape(ne, d, nf, tf_moe).transpose(0, 2, 1, 3).astype(BF16)
            wu = w_e_up[i].reshape(ne, d, nf, tf_moe).transpose(0, 2, 1, 3).astype(BF16)
            wd = w_e_down[i].reshape(ne, nf, tf_moe, d).astype(BF16)
            delta = _moe(cnt_flat, hn, post, pos, comb, wg, wu, wd, tm_moe, moe_sizes, sub_moe)
            h, hn = _add_norm(h, delta, ng_next, tm)

    return h.reshape(bsz, lp, d)[:, N_META:length]
```

```python
import functools

import jax
import jax.numpy as jnp
from jax import lax
from jax.experimental import pallas as pl
from jax.experimental.pallas import tpu as pltpu

F32 = jnp.float32
BF16 = jnp.bfloat16
HIGHEST = lax.Precision.HIGHEST

D_MODEL = 1024
N_META = 16
HEAD_DIM = 64
ATT_W = 512
ATT_HEADS = 8
CONV_W = 256
CONV_K = 31
POOL_W = 256
POOL_WINDOWS = (2, 4, 8, 16)
N_EXPERTS = 8
LANES = 128
HALO = 32
NEG_INF = -1e30
EXP_UNDERFLOW = 110.0
SEQ_ALIGN = 256
VMEM_LIMIT = 56 * 1024 * 1024
MOE_VMEM_LIMIT = 60 * 1024 * 1024
POST_ROWS = 32

Z_GATE = 0
Z_Q = 3072
Z_K = 3584
Z_V = 4096
Z_A = 4608
Z_G = 4864
Z_P = 5120
Z_W = 5376


def _pick(n, candidates):
    for c in candidates:
        if n % c == 0:
            return c
    raise ValueError(f"no tile for {n} in {candidates}")


def _cparams(sem):
    return pltpu.CompilerParams(dimension_semantics=sem, vmem_limit_bytes=VMEM_LIMIT)


def _rms(x, g, eps=1e-6):
    return x * lax.rsqrt(jnp.mean(x * x, axis=-1, keepdims=True) + eps) * g


def _sigmoid(x):
    return 1.0 / (1.0 + jnp.exp(-x))


def _silu(x):
    return x * _sigmoid(x)


def _norm_kernel(h_ref, g_ref, o_ref):
    o_ref[...] = _rms(h_ref[...], g_ref[...]).astype(o_ref.dtype)


def _norm(h, g, tm):
    n, d = h.shape
    return pl.pallas_call(
        _norm_kernel,
        grid=(n // tm,),
        in_specs=[pl.BlockSpec((tm, d), lambda i: (i, 0)), pl.BlockSpec((1, d), lambda i: (0, 0))],
        out_specs=pl.BlockSpec((tm, d), lambda i: (i, 0)),
        out_shape=jax.ShapeDtypeStruct((n, d), BF16),
        compiler_params=_cparams(("parallel",)),
        name="rmsnorm",
    )(h, g)


def _inproj_kernel(x_ref, w_ref, b_ref, o_ref):
    acc = jnp.dot(x_ref[...], w_ref[...], preferred_element_type=F32)
    o_ref[...] = (acc + b_ref[...]).astype(o_ref.dtype)


def _inproj(hn, w, b, tm, tn):
    n, d = hn.shape
    zw = w.shape[1]
    return pl.pallas_call(
        _inproj_kernel,
        grid=(n // tm, zw // tn),
        in_specs=[pl.BlockSpec((tm, d), lambda i, j: (i, 0)),
                  pl.BlockSpec((d, tn), lambda i, j: (0, j)),
                  pl.BlockSpec((1, tn), lambda i, j: (0, j))],
        out_specs=pl.BlockSpec((tm, tn), lambda i, j: (i, j)),
        out_shape=jax.ShapeDtypeStruct((n, zw), BF16),
        compiler_params=_cparams(("parallel", "arbitrary")),
        name="inproj",
    )(hn, w, b)


def _forget_kernel(x_ref, w_ref, b_ref, o_ref, carry_ref):
    @pl.when(pl.program_id(1) == 0)
    def _():
        carry_ref[...] = jnp.zeros_like(carry_ref)

    fw = jnp.dot(x_ref[0], w_ref[...], preferred_element_type=F32)
    f = fw[:, :LANES] + fw[:, LANES:] + b_ref[...]
    ls = jnp.minimum(f, 0.0) - jnp.log(1.0 + jnp.exp(-jnp.abs(f)))
    hi = ls.astype(BF16)
    r1 = ls - hi.astype(F32)
    mid = r1.astype(BF16)
    lo = (r1 - mid.astype(F32)).astype(BF16)
    t = ls.shape[0]
    r = lax.broadcasted_iota(jnp.int32, (t, t), 0)
    c = lax.broadcasted_iota(jnp.int32, (t, t), 1)
    tril = jnp.where(c <= r, 1.0, 0.0).astype(BF16)
    parts = jnp.dot(tril, jnp.concatenate([hi, mid, lo], axis=-1), preferred_element_type=F32)
    cs = (parts[:, :LANES] + parts[:, LANES:2 * LANES]) + parts[:, 2 * LANES:] + carry_ref[...]
    o_ref[0] = cs
    carry_ref[...] = cs[t - 1:t, :]


def _forget(hn3, w_f, b_f, ts):
    b, lp, d = hn3.shape
    return pl.pallas_call(
        _forget_kernel,
        grid=(b, lp // ts),
        in_specs=[pl.BlockSpec((1, ts, d), lambda bi, t: (bi, t, 0)),
                  pl.BlockSpec((d, 2 * LANES), lambda bi, t: (0, 0)),
                  pl.BlockSpec((1, LANES), lambda bi, t: (0, 0))],
        out_specs=pl.BlockSpec((1, ts, LANES), lambda bi, t: (bi, t, 0)),
        out_shape=jax.ShapeDtypeStruct((b, lp, LANES), F32),
        scratch_shapes=[pltpu.VMEM((1, LANES), F32)],
        compiler_params=_cparams(("parallel", "arbitrary")),
        name="forget_cumsum",
    )(hn3, w_f, b_f)


def _pair_rms(x, first_head, gain, eps=1e-6):
    sq = x * x
    s0 = jnp.sum(jnp.where(first_head, sq, 0.0), axis=-1, keepdims=True)
    s1 = jnp.sum(jnp.where(first_head, 0.0, sq), axis=-1, keepdims=True)
    ms = jnp.where(first_head, s0, s1) * (1.0 / HEAD_DIM)
    return x * lax.rsqrt(ms + eps) * gain


def _attn_kernel(lo_ref, q_ref, k_ref, v_ref, fc_ref, fr_ref, qg_ref, kg_ref, o_ref, kn_ref, *, tq, tk):
    bi = pl.program_id(0)
    p = pl.program_id(1)
    i = pl.program_id(2)
    nq = pl.num_programs(2)
    lp = k_ref.shape[1]
    lane = lax.broadcasted_iota(jnp.int32, (1, LANES), 1)
    first_head = lane < HEAD_DIM

    @pl.when(i == 0)
    def _():
        def body(c, carry):
            off = pl.multiple_of(c * tk, tk)
            kk = k_ref[0, pl.ds(off, tk), :].astype(F32)
            kn_ref[pl.ds(off, tk), :] = _pair_rms(kk, first_head, kg_ref[...]).astype(BF16)
            return carry
        lax.fori_loop(0, lp // tk, body, 0)

    qn = _pair_rms(q_ref[0].astype(F32), first_head, qg_ref[...])
    fc = fc_ref[0]
    row_pos = i * tq + lax.broadcasted_iota(jnp.int32, (tq, 1), 0)

    heads = [2 * p, 2 * p + 1]
    qhs = [jnp.where(first_head, qn, 0.0).astype(BF16), jnp.where(first_head, 0.0, qn).astype(BF16)]
    fts = [jnp.sum(jnp.where(lane == hd, fc, 0.0), axis=-1, keepdims=True) for hd in heads]

    def update(off, width, carry, hh, masked):
        m, l, acc = carry
        ks = kn_ref[pl.ds(off, width), :]
        s = lax.dot_general(qhs[hh], ks, (((1,), (1,)), ((), ())), preferred_element_type=F32)
        fs = fr_ref[0, pl.ds(heads[hh], 1), pl.ds(off, width)]
        s = s + (fts[hh] - fs)
        if masked:
            col_pos = off + lax.broadcasted_iota(jnp.int32, (1, width), 1)
            s = jnp.where(col_pos <= row_pos, s, NEG_INF)
        m_new = jnp.maximum(m, jnp.max(s, axis=-1, keepdims=True))
        alpha = jnp.exp(m - m_new)
        pm = jnp.exp(s - m_new)
        l = alpha * l + jnp.sum(pm, axis=-1, keepdims=True)
        vs = v_ref[0, pl.ds(off, width), :]
        acc = alpha * acc + jnp.dot(pm.astype(BF16), vs, preferred_element_type=F32)
        return m_new, l, acc

    win = jnp.maximum(i * (tq // tk) - 1, 0)
    firsts = [lo_ref[(bi * ATT_HEADS + hd) * nq + i] for hd in heads]
    init = (jnp.full((tq, 1), NEG_INF, F32), jnp.zeros((tq, 1), F32), jnp.zeros((tq, LANES), F32))

    def far(j, carry):
        off = pl.multiple_of(j * tk, tk)
        return tuple(update(off, tk, carry[hh], hh, masked=False) for hh in range(2))

    carry = lax.fori_loop(jnp.minimum(firsts[0], firsts[1]), win, far, (init, init))
    off = pl.multiple_of(win * tk, tk)
    outs = []
    for hh in range(2):
        m, l, acc = update(off, tq + tk, carry[hh], hh, masked=True)
        outs.append(acc / l)
    o_ref[0] = jnp.where(first_head, outs[0], outs[1]).astype(o_ref.dtype)


def _first_live_chunk(frow, qg, kg, tq, tk):
    b, nh, lp = frow.shape
    qk_bound = 1.02 * HEAD_DIM * jnp.max(jnp.abs(qg)) * jnp.max(jnp.abs(kg))
    f_first = frow[:, :, 0::tq]
    f_last = frow[:, :, tk - 1::tk]
    dead = (f_first[:, :, :, None] - f_last[:, :, None, :] + 2.0 * qk_bound) < -EXP_UNDERFLOW
    n_dead = jnp.sum(dead.astype(jnp.int32), axis=-1)
    n_full = (jnp.arange(lp // tq, dtype=jnp.int32) * tq) // tk
    return jnp.minimum(n_dead, n_full[None, None, :]).reshape(-1)


def _attention(z3, fcol, frow, qg, kg, tq, tk):
    b, lp, _ = z3.shape
    npairs = ATT_HEADS // 2
    qb, kb, vb = Z_Q // LANES, Z_K // LANES, Z_V // LANES
    assert tq % tk == 0 and lp >= tq + tk, (tq, tk, lp)
    first = _first_live_chunk(frow, qg, kg, tq, tk)
    grid_spec = pltpu.PrefetchScalarGridSpec(
        num_scalar_prefetch=1,
        grid=(b, npairs, lp // tq),
        in_specs=[pl.BlockSpec((1, tq, LANES), lambda bi, p, i, lo: (bi, i, qb + p)),
                  pl.BlockSpec((1, lp, LANES), lambda bi, p, i, lo: (bi, 0, kb + p)),
                  pl.BlockSpec((1, lp, LANES), lambda bi, p, i, lo: (bi, 0, vb + p)),
                  pl.BlockSpec((1, tq, LANES), lambda bi, p, i, lo: (bi, i, 0)),
                  pl.BlockSpec((1, ATT_HEADS, lp), lambda bi, p, i, lo: (bi, 0, 0)),
                  pl.BlockSpec((1, LANES), lambda bi, p, i, lo: (0, 0)),
                  pl.BlockSpec((1, LANES), lambda bi, p, i, lo: (0, 0))],
        out_specs=pl.BlockSpec((1, tq, LANES), lambda bi, p, i, lo: (bi, i, p)),
        scratch_shapes=[pltpu.VMEM((lp, LANES), BF16)],
    )
    return pl.pallas_call(
        functools.partial(_attn_kernel, tq=tq, tk=tk),
        grid_spec=grid_spec,
        out_shape=jax.ShapeDtypeStruct((b, lp, ATT_W), BF16),
        compiler_params=_cparams(("parallel", "parallel", "arbitrary")),
        name="fox_attention",
    )(first, z3, z3, z3, fcol, frow, qg, kg)


def _convpool_kernel(a_ref, g_ref, p_ref, ah_ref, gh_ref, ph_ref, cw_ref, cb_ref, lg_ref, lb_ref,
                     pw_ref, ps_ref, u_ref, pm_ref, ext_ref, pext_ref, *, ts):
    i = pl.program_id(1)
    has_prev = i > 0

    u = a_ref[0].astype(F32) * _sigmoid(g_ref[0].astype(F32))
    uh = ah_ref[0].astype(F32) * _sigmoid(gh_ref[0].astype(F32))
    ext_ref[0:HALO, :] = jnp.where(has_prev, uh, 0.0)
    ext_ref[HALO:HALO + ts, :] = u
    acc = jnp.zeros((ts, CONV_W), F32) + cb_ref[...]
    for j in range(CONV_K):
        start = HALO - (CONV_K - 1) + j
        acc = acc + cw_ref[j:j + 1, :] * ext_ref[start:start + ts, :]
    mu = jnp.mean(acc, axis=-1, keepdims=True)
    cen = acc - mu
    var = jnp.mean(cen * cen, axis=-1, keepdims=True)
    y = cen * lax.rsqrt(var + 1e-5) * lg_ref[...] + lb_ref[...]
    u_ref[0] = _silu(y).astype(u_ref.dtype)

    x = p_ref[0].astype(F32)
    pext_ref[0:HALO, :] = jnp.where(has_prev, ph_ref[0].astype(F32), 0.0)
    pext_ref[HALO:HALO + ts, :] = x
    pos1 = (i * ts + 1 + lax.broadcasted_iota(jnp.int32, (ts, 1), 0)).astype(F32)
    lane = lax.broadcasted_iota(jnp.int32, (1, POOL_W), 1)
    group_w = POOL_W // len(POOL_WINDOWS)
    run = x
    pooled = jnp.zeros((ts, POOL_W), F32)
    for k in range(1, max(POOL_WINDOWS)):
        run = run + pext_ref[HALO - k:HALO - k + ts, :]
        if (k + 1) in POOL_WINDOWS:
            gi = POOL_WINDOWS.index(k + 1)
            mean = run / jnp.minimum(pos1, float(k + 1))
            in_group = (lane >= gi * group_w) & (lane < (gi + 1) * group_w)
            pooled = jnp.where(in_group, mean, pooled)
    pm = (pooled - x).astype(BF16)
    lin = jnp.dot(pm, pw_ref[...], preferred_element_type=F32) * ps_ref[...]
    pm_ref[0] = lin.astype(pm_ref.dtype)


def _convpool(z3, cw, cb, lg, lb, pw, ps, ts):
    b, lp, _ = z3.shape
    ab, gb, pb = Z_A // CONV_W, Z_G // CONV_W, Z_P // POOL_W
    hpt = ts // HALO
    main = lambda blk: pl.BlockSpec((1, ts, CONV_W), lambda bi, i: (bi, i, blk))
    halo = lambda blk: pl.BlockSpec((1, HALO, CONV_W), lambda bi, i: (bi, jnp.maximum(i * hpt - 1, 0), blk))
    const = lambda shape: pl.BlockSpec(shape, lambda bi, i: (0, 0))
    out = pl.BlockSpec((1, ts, CONV_W), lambda bi, i: (bi, i, 0))
    return pl.pallas_call(
        functools.partial(_convpool_kernel, ts=ts),
        grid=(b, lp // ts),
        in_specs=[main(ab), main(gb), main(pb), halo(ab), halo(gb), halo(pb),
                  const((HALO, CONV_W)), const((1, CONV_W)), const((1, CONV_W)), const((1, CONV_W)),
                  const((POOL_W, POOL_W)), const((1, POOL_W))],
        out_specs=[out, out],
        out_shape=[jax.ShapeDtypeStruct((b, lp, CONV_W), BF16), jax.ShapeDtypeStruct((b, lp, POOL_W), BF16)],
        scratch_shapes=[pltpu.VMEM((HALO + ts, CONV_W), F32), pltpu.VMEM((HALO + ts, POOL_W), F32)],
        compiler_params=_cparams(("parallel", "arbitrary")),
        name="conv_pool",
    )(z3, z3, z3, z3, z3, z3, cw, cb, lg, lb, pw, ps)


def _merge_kernel(att_ref, u_ref, pm_ref, g0_ref, g1_ref, g2_ref, h_ref, wa_ref, wc_ref, wp_ref, wo_ref,
                  ng_ref, h_out_ref, hn_out_ref):
    ya = jnp.dot(att_ref[...], wa_ref[...], preferred_element_type=F32)
    yc = jnp.dot(u_ref[...], wc_ref[...], preferred_element_type=F32)
    yp = jnp.dot(pm_ref[...], wp_ref[...], preferred_element_type=F32)
    m = (_sigmoid(g0_ref[...].astype(F32)) * ya + _sigmoid(g1_ref[...].astype(F32)) * yc
         + _sigmoid(g2_ref[...].astype(F32)) * yp)
    h = h_ref[...] + jnp.dot(m.astype(BF16), wo_ref[...], preferred_element_type=F32)
    h_out_ref[...] = h
    hn_out_ref[...] = _rms(h, ng_ref[...]).astype(hn_out_ref.dtype)


def _merge(att, u, pm, z, h, wa, wc, wp, wo, ng, tm):
    n, d = h.shape
    row = lambda w, blk=0: pl.BlockSpec((tm, w), lambda i: (i, blk))
    const = lambda shape: pl.BlockSpec(shape, lambda i: (0, 0))
    return pl.pallas_call(
        _merge_kernel,
        grid=(n // tm,),
        in_specs=[row(ATT_W), row(CONV_W), row(POOL_W), row(d, 0), row(d, 1), row(d, 2), row(d),
                  const(wa.shape), const(wc.shape), const(wp.shape), const(wo.shape), const((1, d))],
        out_specs=[row(d), row(d)],
        out_shape=[jax.ShapeDtypeStruct((n, d), F32), jax.ShapeDtypeStruct((n, d), BF16)],
        compiler_params=_cparams(("parallel",)),
        name="merge_outproj",
    )(att, u, pm, z, z, z, h, wa, wc, wp, wo, ng)


def _ffn_kernel(x_ref, h_ref, wg_ref, wu_ref, wd_ref, ng_ref, h_out_ref, hn_out_ref, *, tf):
    x = x_ref[...]
    h = h_ref[...]
    for c in range(wg_ref.shape[1] // tf):
        gt = jnp.dot(x, wg_ref[:, c * tf:(c + 1) * tf], preferred_element_type=F32)
        up = jnp.dot(x, wu_ref[:, c * tf:(c + 1) * tf], preferred_element_type=F32)
        h = h + jnp.dot((_silu(gt) * up).astype(BF16), wd_ref[c * tf:(c + 1) * tf, :],
                        preferred_element_type=F32)
    h_out_ref[...] = h
    hn_out_ref[...] = _rms(h, ng_ref[...]).astype(hn_out_ref.dtype)


def _ffn(hn, h, wg, wu, wd, ng, tm, tf):
    n, d = h.shape
    row = pl.BlockSpec((tm, d), lambda i: (i, 0))
    const = lambda shape: pl.BlockSpec(shape, lambda i: (0, 0), pipeline_mode=pl.Buffered(1))
    return pl.pallas_call(
        functools.partial(_ffn_kernel, tf=tf),
        grid=(n // tm,),
        in_specs=[row, row, const(wg.shape), const(wu.shape), const(wd.shape), const((1, d))],
        out_specs=[row, row],
        out_shape=[jax.ShapeDtypeStruct((n, d), F32), jax.ShapeDtypeStruct((n, d), BF16)],
        compiler_params=_cparams(("parallel",)),
        name="dense_swiglu",
    )(hn, h, wg, wu, wd, ng)


def _router_kernel(h_ref, ng_ref, wr_ref, br_ref, comb_ref, pos_ref, cnt_ref, *, sb):
    tm = h_ref.shape[0]
    hn = _rms(h_ref[...], ng_ref[...])
    logits = jnp.dot(hn, wr_ref[...], precision=HIGHEST, preferred_element_type=F32) + br_ref[...]
    lane = lax.broadcasted_iota(jnp.int32, (1, LANES), 1).astype(F32)
    lg = jnp.where(lane < N_EXPERTS, logits, -jnp.inf)
    m1 = jnp.max(lg, axis=-1, keepdims=True)
    i1 = jnp.min(jnp.where(lg == m1, lane, float(LANES)), axis=-1, keepdims=True)
    sel1 = lane == i1
    lg2 = jnp.where(sel1, -jnp.inf, lg)
    m2 = jnp.max(lg2, axis=-1, keepdims=True)
    i2 = jnp.min(jnp.where(lg2 == m2, lane, float(LANES)), axis=-1, keepdims=True)
    sel2 = lane == i2
    e = jnp.exp(m2 - m1)
    g1 = 1.0 / (1.0 + e)
    comb_ref[...] = jnp.where(sel1, g1, 0.0) + jnp.where(sel2, e * g1, 0.0)
    sel = jnp.where(sel1 | sel2, 1.0, 0.0)
    r = lax.broadcasted_iota(jnp.int32, (sb, sb), 0)
    c = lax.broadcasted_iota(jnp.int32, (sb, sb), 1)
    tril = jnp.where(c <= r, 1.0, 0.0).astype(BF16)
    carry = jnp.zeros((1, LANES), F32)
    for s in range(tm // sb):
        blk = slice(s * sb, (s + 1) * sb)
        incl = jnp.dot(tril, sel[blk].astype(BF16), preferred_element_type=F32) + carry
        pos_ref[blk, :] = jnp.where(sel[blk] > 0.0, incl - 1.0, -1.0)
        carry = incl[sb - 1:sb, :]
    cnt_ref[0] = jnp.broadcast_to(carry, (8, LANES)).astype(jnp.int32)


def _router(h, ng, wr, br, tm, sb):
    n, d = h.shape
    nt = n // tm
    return pl.pallas_call(
        functools.partial(_router_kernel, sb=sb),
        grid=(nt,),
        in_specs=[pl.BlockSpec((tm, d), lambda i: (i, 0)),
                  pl.BlockSpec((1, d), lambda i: (0, 0)),
                  pl.BlockSpec((d, LANES), lambda i: (0, 0)),
                  pl.BlockSpec((1, LANES), lambda i: (0, 0))],
        out_specs=[pl.BlockSpec((tm, LANES), lambda i: (i, 0)),
                   pl.BlockSpec((tm, LANES), lambda i: (i, 0)),
                   pl.BlockSpec((1, 8, LANES), lambda i: (i, 0, 0))],
        out_shape=[jax.ShapeDtypeStruct((n, LANES), F32), jax.ShapeDtypeStruct((n, LANES), F32),
                   jax.ShapeDtypeStruct((nt, 8, LANES), jnp.int32)],
        compiler_params=_cparams(("parallel",)),
        name="router_top2",
    )(h, ng, wr, br)


def _moe_kernel(cnt_ref, x_ref, post_ref, pos_ref, comb_ref, wg_ref, wu_ref, wd_ref,
                out_ref, xs_ref, ys_ref, *, sizes, sub):
    i = pl.program_id(0)
    e = pl.program_id(1)
    f = pl.program_id(2)
    nf = pl.num_programs(2)
    cnt = cnt_ref[i * N_EXPERTS + e]
    big = sizes[-1]

    def for_each_chunk(body):
        lo = 0
        for r in sizes:
            @pl.when((cnt > lo) & (cnt <= r))
            def _(r=r):
                body(0, r)
            lo = r

        @pl.when(cnt > big)
        def _():
            def step(c, carry):
                body(pl.multiple_of(c * big, big), big)
                return carry
            lax.fori_loop(0, (cnt + big - 1) // big, step, 0)

    @pl.when((e == 0) & (f == 0))
    def _():
        out_ref[...] = jnp.zeros_like(out_ref)

    @pl.when(f == 0)
    def _():
        prow = post_ref[0, pl.ds(e, 1), :]

        def gather(off, r):
            tgt = (off + lax.broadcasted_iota(jnp.int32, (r, 1), 0)).astype(F32)
            onehot = jnp.where(prow == tgt, 1.0, 0.0).astype(BF16)
            xs_ref[pl.ds(off, r), :] = jnp.dot(onehot, x_ref[...], preferred_element_type=F32).astype(BF16)
            ys_ref[pl.ds(off, r), :] = jnp.zeros((r, D_MODEL), F32)
        for_each_chunk(gather)

    def expert(off, r):
        xc = xs_ref[pl.ds(off, r), :]
        y = ys_ref[pl.ds(off, r), :]
        for s in range(wg_ref.shape[2] // sub):
            cols = slice(s * sub, (s + 1) * sub)
            gt = jnp.dot(xc, wg_ref[0, :, cols], preferred_element_type=F32)
            up = jnp.dot(xc, wu_ref[0, :, cols], preferred_element_type=F32)
            y = y + jnp.dot((_silu(gt) * up).astype(BF16), wd_ref[0, cols, :], preferred_element_type=F32)
        ys_ref[pl.ds(off, r), :] = y
    for_each_chunk(expert)

    @pl.when(f == nf - 1)
    def _():
        lane = lax.broadcasted_iota(jnp.int32, (1, LANES), 1)
        pcol = jnp.sum(jnp.where(lane == e, pos_ref[...], 0.0), axis=-1, keepdims=True)
        gcol = jnp.sum(jnp.where(lane == e, comb_ref[...], 0.0), axis=-1, keepdims=True)

        def scatter(off, r):
            tgt = (off + lax.broadcasted_iota(jnp.int32, (1, r), 1)).astype(F32)
            onehot_t = jnp.where(pcol == tgt, 1.0, 0.0).astype(BF16)
            y = ys_ref[pl.ds(off, r), :].astype(BF16)
            out_ref[...] += gcol * jnp.dot(onehot_t, y, preferred_element_type=F32)
        for_each_chunk(scatter)


def _moe(cnt, hn, post, pos, comb, wg, wu, wd, tm, tf, sizes, sub):
    n, d = hn.shape
    nf = wg.shape[2] // tf
    cap = -(-tm // sizes[-1]) * sizes[-1]
    once = pl.Buffered(1)
    row = lambda w: pl.BlockSpec((tm, w), lambda i, e, f, cnt: (i, 0), pipeline_mode=once)
    grid_spec = pltpu.PrefetchScalarGridSpec(
        num_scalar_prefetch=1,
        grid=(n // tm, N_EXPERTS, nf),
        in_specs=[row(d),
                  pl.BlockSpec((1, POST_ROWS, tm), lambda i, e, f, cnt: (i, 0, 0), pipeline_mode=once),
                  row(LANES), row(LANES),
                  pl.BlockSpec((1, d, tf), lambda i, e, f, cnt: (e, 0, f)),
                  pl.BlockSpec((1, d, tf), lambda i, e, f, cnt: (e, 0, f)),
                  pl.BlockSpec((1, tf, d), lambda i, e, f, cnt: (e, f, 0))],
        out_specs=pl.BlockSpec((tm, d), lambda i, e, f, cnt: (i, 0)),
        scratch_shapes=[pltpu.VMEM((cap, d), BF16), pltpu.VMEM((cap, d), F32)],
    )
    return pl.pallas_call(
        functools.partial(_moe_kernel, sizes=sizes, sub=sub),
        grid_spec=grid_spec,
        out_shape=jax.ShapeDtypeStruct((n, d), F32),
        compiler_params=pltpu.CompilerParams(dimension_semantics=("parallel", "arbitrary", "arbitrary"),
                                             vmem_limit_bytes=MOE_VMEM_LIMIT),
        name="expert_swiglu",
    )(cnt, hn, post, pos, comb, wg, wu, wd)


def _add_norm_kernel(h_ref, d_ref, g_ref, h_out_ref, hn_out_ref):
    h = h_ref[...] + d_ref[...]
    h_out_ref[...] = h
    hn_out_ref[...] = _rms(h, g_ref[...]).astype(hn_out_ref.dtype)


def _add_norm(h, delta, g, tm):
    n, d = h.shape
    row = pl.BlockSpec((tm, d), lambda i: (i, 0))
    return pl.pallas_call(
        _add_norm_kernel,
        grid=(n // tm,),
        in_specs=[row, row, pl.BlockSpec((1, d), lambda i: (0, 0))],
        out_specs=[row, row],
        out_shape=[jax.ShapeDtypeStruct((n, d), F32), jax.ShapeDtypeStruct((n, d), BF16)],
        compiler_params=_cparams(("parallel",)),
        name="residual_norm",
    )(h, delta, g)


def _pad_lanes(a, width=LANES):
    return jnp.pad(a, ((0, 0), (0, width - a.shape[-1])))


def kernel(x, meta, norm_mix, w_in, b_in, q_norm, k_norm, w_attn_o, conv_w, conv_b, conv_ln_g, conv_ln_b,
           w_conv_o, pool_w, pool_scale, w_pool_o, w_out, norm_ffn, w_ff_gate, w_ff_up, w_ff_down, w_router,
           b_router, w_e_gate, w_e_up, w_e_down):
    bsz, seq, d = x.shape
    depth = w_in.shape[0]
    length = seq + N_META
    lp = -(-length // SEQ_ALIGN) * SEQ_ALIGN
    n = bsz * lp

    tm = _pick(n, (1536, 768, 512, 256))
    tm_mid = _pick(n, (768, 512, 256))
    tm_ffn = _pick(n, (768, 512, 256))
    tm_moe = _pick(n, (1408, 768, 512, 256))
    sb_router = _pick(tm_moe, (704, 768, 512, 256))
    tn = _pick(Z_W, (1792, 768, 256))
    ts = _pick(lp, (768, 512, 256))
    ts_f = ts
    tk = 256
    tq = _pick(lp, (768, 256)) if lp >= 1024 else 256
    tf_dense = _pick(w_ff_gate.shape[2], (256,))
    tf_moe = _pick(w_e_gate.shape[3], (1792, 512, 256))
    sub_moe = _pick(tf_moe, (256,))
    quarter = tm_moe // 4
    moe_sizes = tuple(range(max(quarter - 96, 64) // 64 * 64, quarter + 161, 64))

    h = jnp.concatenate([jnp.broadcast_to(meta[None].astype(x.dtype), (bsz, N_META, d)), x], axis=1)
    h = jnp.pad(h, ((0, 0), (0, lp - length), (0, 0))).reshape(n, d)
    hn = _norm(h, norm_mix[0][None], tm)

    att_scale = HEAD_DIM ** -0.5
    gate_lo = 3 * ATT_W + ATT_HEADS + 2 * CONV_W + POOL_W
    f_lo = 3 * ATT_W
    for l in range(depth):
        wl = w_in[l]
        w_main = jnp.concatenate([wl[:, gate_lo:], wl[:, :f_lo], wl[:, f_lo + ATT_HEADS:gate_lo]], axis=1).astype(BF16)
        bl = b_in[l]
        b_main = jnp.concatenate([bl[gate_lo:], bl[:f_lo], bl[f_lo + ATT_HEADS:gate_lo]])[None]
        w_f = _pad_lanes(wl[:, f_lo:f_lo + ATT_HEADS])
        w_f_hi = w_f.astype(BF16)
        w_f = jnp.concatenate([w_f_hi, (w_f - w_f_hi.astype(F32)).astype(BF16)], axis=1)
        b_f = _pad_lanes(bl[None, f_lo:f_lo + ATT_HEADS])

        z = _inproj(hn, w_main, b_main, tm, tn)
        z3 = z.reshape(bsz, lp, Z_W)
        fcol = _forget(hn.reshape(bsz, lp, d), w_f, b_f, ts_f)
        frow = jnp.transpose(fcol[:, :, :ATT_HEADS], (0, 2, 1))
        qg = jnp.tile(q_norm[l], 2)[None] * att_scale
        kg = jnp.tile(k_norm[l], 2)[None]
        att = _attention(z3, fcol, frow, qg, kg, tq, tk)

        cw = jnp.pad(conv_w[l], ((0, HALO - CONV_K), (0, 0)))
        pw = jax.scipy.linalg.block_diag(*[pool_w[l, g] for g in range(pool_w.shape[1])]).astype(BF16)
        u, pm = _convpool(z3, cw, conv_b[l][None], conv_ln_g[l][None], conv_ln_b[l][None], pw,
                          pool_scale[l][None], ts)

        h, hn = _merge(att.reshape(n, ATT_W), u.reshape(n, CONV_W), pm.reshape(n, POOL_W), z, h,
                       w_attn_o[l].astype(BF16), w_conv_o[l].astype(BF16), w_pool_o[l].astype(BF16),
                       w_out[l].astype(BF16), norm_ffn[l][None], tm_mid)

        ng_next = norm_mix[min(l + 1, depth - 1)][None]
        i = l // 2
        if l % 2 == 0:
            h, hn = _ffn(hn, h, w_ff_gate[i].astype(BF16), w_ff_up[i].astype(BF16), w_ff_down[i].astype(BF16),
                         ng_next, tm_ffn, tf_dense)
        else:
            comb, pos, cnt = _router(h, norm_ffn[l][None], _pad_lanes(w_router[i]), _pad_lanes(b_router[i][None]),
                                     tm_moe, sb_router)
            post = jnp.transpose(pos[:, :POST_ROWS].reshape(n // tm_moe, tm_moe, POST_ROWS), (0, 2, 1))
            cnt_flat = cnt[:, 0, :N_EXPERTS].reshape(-1)
            delta = _moe(cnt_flat, hn, post, pos, comb, w_e_gate[i].astype(BF16), w_e_up[i].astype(BF16),
                         w_e_down[i].astype(BF16), tm_moe, tf_moe, moe_sizes, sub_moe)
            h, hn = _add_norm(h, delta, ng_next, tm)

    return h.reshape(bsz, lp, d)[:, N_META:length]
```

```python
import functools

import jax
import jax.numpy as jnp
from jax import lax
from jax.experimental import pallas as pl
from jax.experimental.pallas import tpu as pltpu

F32 = jnp.float32
BF16 = jnp.bfloat16
HIGHEST = lax.Precision.HIGHEST

D_MODEL = 1024
N_META = 16
HEAD_DIM = 64
ATT_W = 512
ATT_HEADS = 8
CONV_W = 256
CONV_K = 31
POOL_W = 256
POOL_WINDOWS = (2, 4, 8, 16)
N_EXPERTS = 8
LANES = 128
HALO = 32
NEG_INF = -1e30
LOG2E = 1.4426950408889634
EXP_UNDERFLOW = 110.0
SEQ_ALIGN = 256
VMEM_LIMIT = 56 * 1024 * 1024
MOE_VMEM_LIMIT = 60 * 1024 * 1024
POST_ROWS = 32

Z_GATE = 0
Z_Q = 3072
Z_K = 3584
Z_V = 4096
Z_A = 4608
Z_G = 4864
Z_P = 5120
Z_W = 5376


def _pick(n, candidates):
    for c in candidates:
        if n % c == 0:
            return c
    raise ValueError(f"no tile for {n} in {candidates}")


def _cparams(sem):
    return pltpu.CompilerParams(dimension_semantics=sem, vmem_limit_bytes=VMEM_LIMIT)


def _rms(x, g, eps=1e-6):
    return x * lax.rsqrt(jnp.mean(x * x, axis=-1, keepdims=True) + eps) * g


def _sigmoid(x):
    return 1.0 / (1.0 + jnp.exp(-x))


def _silu(x):
    return x * _sigmoid(x)


def _norm_kernel(h_ref, g_ref, o_ref):
    o_ref[...] = _rms(h_ref[...], g_ref[...]).astype(o_ref.dtype)


def _norm(h, g, tm):
    n, d = h.shape
    return pl.pallas_call(
        _norm_kernel,
        grid=(n // tm,),
        in_specs=[pl.BlockSpec((tm, d), lambda i: (i, 0)), pl.BlockSpec((1, d), lambda i: (0, 0))],
        out_specs=pl.BlockSpec((tm, d), lambda i: (i, 0)),
        out_shape=jax.ShapeDtypeStruct((n, d), BF16),
        compiler_params=_cparams(("parallel",)),
        name="rmsnorm",
    )(h, g)


def _inproj_kernel(x_ref, w_ref, b_ref, o_ref):
    acc = jnp.dot(x_ref[...], w_ref[...], preferred_element_type=F32)
    o_ref[...] = (acc + b_ref[...]).astype(o_ref.dtype)


def _inproj(hn, w, b, tm, tn):
    n, d = hn.shape
    zw = w.shape[1]
    return pl.pallas_call(
        _inproj_kernel,
        grid=(n // tm, zw // tn),
        in_specs=[pl.BlockSpec((tm, d), lambda i, j: (i, 0)),
                  pl.BlockSpec((d, tn), lambda i, j: (0, j)),
                  pl.BlockSpec((1, tn), lambda i, j: (0, j))],
        out_specs=pl.BlockSpec((tm, tn), lambda i, j: (i, j)),
        out_shape=jax.ShapeDtypeStruct((n, zw), BF16),
        compiler_params=_cparams(("parallel", "arbitrary")),
        name="inproj",
    )(hn, w, b)


def _forget_kernel(x_ref, w_ref, b_ref, o_ref, carry_ref):
    @pl.when(pl.program_id(1) == 0)
    def _():
        carry_ref[...] = jnp.zeros_like(carry_ref)

    fw = jnp.dot(x_ref[0], w_ref[...], preferred_element_type=F32)
    f = fw[:, :LANES] + fw[:, LANES:] + b_ref[...]
    ls = jnp.minimum(f, 0.0) - jnp.log(1.0 + jnp.exp(-jnp.abs(f)))
    hi = ls.astype(BF16)
    r1 = ls - hi.astype(F32)
    mid = r1.astype(BF16)
    lo = (r1 - mid.astype(F32)).astype(BF16)
    t = ls.shape[0]
    r = lax.broadcasted_iota(jnp.int32, (t, t), 0)
    c = lax.broadcasted_iota(jnp.int32, (t, t), 1)
    tril = jnp.where(c <= r, 1.0, 0.0).astype(BF16)
    parts = jnp.dot(tril, jnp.concatenate([hi, mid, lo], axis=-1), preferred_element_type=F32)
    cs = (parts[:, :LANES] + parts[:, LANES:2 * LANES]) + parts[:, 2 * LANES:] + carry_ref[...]
    o_ref[0] = cs
    carry_ref[...] = cs[t - 1:t, :]


def _forget(hn3, w_f, b_f, ts):
    b, lp, d = hn3.shape
    return pl.pallas_call(
        _forget_kernel,
        grid=(b, lp // ts),
        in_specs=[pl.BlockSpec((1, ts, d), lambda bi, t: (bi, t, 0)),
                  pl.BlockSpec((d, 2 * LANES), lambda bi, t: (0, 0)),
                  pl.BlockSpec((1, LANES), lambda bi, t: (0, 0))],
        out_specs=pl.BlockSpec((1, ts, LANES), lambda bi, t: (bi, t, 0)),
        out_shape=jax.ShapeDtypeStruct((b, lp, LANES), F32),
        scratch_shapes=[pltpu.VMEM((1, LANES), F32)],
        compiler_params=_cparams(("parallel", "arbitrary")),
        name="forget_cumsum",
    )(hn3, w_f, b_f)


def _pair_rms(x, first_head, gain, eps=1e-6):
    sq = x * x
    s0 = jnp.sum(jnp.where(first_head, sq, 0.0), axis=-1, keepdims=True)
    s1 = jnp.sum(jnp.where(first_head, 0.0, sq), axis=-1, keepdims=True)
    ms = jnp.where(first_head, s0, s1) * (1.0 / HEAD_DIM)
    return x * lax.rsqrt(ms + eps) * gain


def _split3(x):
    hi = x.astype(BF16).astype(F32)
    mid = (x - hi).astype(BF16).astype(F32)
    lo = ((x - hi) - mid).astype(BF16).astype(F32)
    return [hi, mid, lo]


def _forget_lanes(f_heads, lane, slots, key_side):
    out = jnp.zeros((f_heads[0].shape[0], LANES), F32)
    for f, s in zip(f_heads, slots):
        parts = _split3(f)
        vals = [1.0, 1.0, 1.0] + [-x for x in parts] if key_side else parts + [1.0, 1.0, 1.0]
        for j, v in enumerate(vals):
            out = jnp.where(lane == 6 * s + j, v, out)
    return out.astype(BF16)


def _attn_kernel(lo_ref, q_ref, k_ref, v_ref, fc_ref, fk_ref, qg_ref, kg_ref, o_ref, kn_ref, *, tq, tk):
    bi = pl.program_id(0)
    p = pl.program_id(1)
    i = pl.program_id(2)
    nq = pl.num_programs(2)
    lp = k_ref.shape[1]
    lane = lax.broadcasted_iota(jnp.int32, (1, LANES), 1)
    first_head = lane < HEAD_DIM
    heads = [2 * p, 2 * p + 1]

    def head_f(fc):
        return [jnp.sum(jnp.where(lane == hd, fc, 0.0), axis=-1, keepdims=True) * LOG2E for hd in heads]

    @pl.when(i == 0)
    def _():
        def body(c, carry):
            off = pl.multiple_of(c * tk, tk)
            kk = k_ref[0, pl.ds(off, tk), :].astype(F32)
            kn_ref[pl.ds(off, tk), 0:LANES] = _pair_rms(kk, first_head, kg_ref[...]).astype(BF16)
            kn_ref[pl.ds(off, tk), LANES:2 * LANES] = _forget_lanes(
                head_f(fk_ref[0, pl.ds(off, tk), :]), lane, (0, 1), key_side=True)
            return carry
        lax.fori_loop(0, lp // tk, body, 0)

    qn = _pair_rms(q_ref[0].astype(F32), first_head, qg_ref[...])
    fts = head_f(fc_ref[0])
    qas = [jnp.concatenate([jnp.where(first_head, qn, 0.0).astype(BF16),
                            _forget_lanes([fts[0]], lane, (0,), key_side=False)], axis=1),
           jnp.concatenate([jnp.where(first_head, 0.0, qn).astype(BF16),
                            _forget_lanes([fts[1]], lane, (1,), key_side=False)], axis=1)]
    causal = (lax.broadcasted_iota(jnp.int32, (tq, tq), 1) <= lax.broadcasted_iota(jnp.int32, (tq, tq), 0))

    def update(off, lead, width, carry, hh):
        m, l, acc = carry
        s = lax.dot_general(qas[hh], kn_ref[pl.ds(off, width), :], (((1,), (1,)), ((), ())),
                            preferred_element_type=F32)
        if lead < width:
            own = jnp.where(causal, s[:, lead:], NEG_INF)
            s = own if lead == 0 else jnp.concatenate([s[:, :lead], own], axis=1)
        m_new = jnp.maximum(m, jnp.max(s, axis=-1, keepdims=True))
        alpha = jnp.exp2(m - m_new)
        pm = jnp.exp2(s - m_new)
        l = alpha * l + jnp.sum(pm, axis=-1, keepdims=True)
        vs = v_ref[0, pl.ds(off, width), :]
        acc = alpha * acc + jnp.dot(pm.astype(BF16), vs, preferred_element_type=F32)
        return m_new, l, acc

    init = (jnp.full((tq, 1), NEG_INF, F32), jnp.zeros((tq, 1), F32), jnp.zeros((tq, LANES), F32))

    def finish(carry, off, lead):
        outs = []
        for hh in range(2):
            m, l, acc = update(off, lead, lead + tq, carry[hh], hh)
            outs.append(acc / l)
        o_ref[0] = jnp.where(first_head, outs[0], outs[1]).astype(o_ref.dtype)

    @pl.when(i == 0)
    def _():
        finish((init, init), 0, 0)

    @pl.when(i > 0)
    def _():
        win = i * (tq // tk) - 1
        firsts = [lo_ref[(bi * ATT_HEADS + hd) * nq + i] for hd in heads]

        def far(j, carry):
            off = pl.multiple_of(j * tk, tk)
            return tuple(update(off, tk, tk, carry[hh], hh) for hh in range(2))

        carry = lax.fori_loop(jnp.minimum(firsts[0], firsts[1]), win, far, (init, init))
        finish(carry, pl.multiple_of(win * tk, tk), tk)


def _first_live_chunk(frow, qg, kg, tq, tk):
    b, nh, lp = frow.shape
    qk_bound = 1.02 * HEAD_DIM * jnp.max(jnp.abs(qg)) * jnp.max(jnp.abs(kg))
    f_first = frow[:, :, 0::tq]
    f_last = frow[:, :, tk - 1::tk]
    dead = (f_first[:, :, :, None] - f_last[:, :, None, :] + 2.0 * qk_bound) < -EXP_UNDERFLOW
    n_dead = jnp.sum(dead.astype(jnp.int32), axis=-1)
    n_full = (jnp.arange(lp // tq, dtype=jnp.int32) * tq) // tk
    return jnp.minimum(n_dead, n_full[None, None, :]).reshape(-1)


def _attention(z3, fcol, frow, qg, kg, tq, tk):
    b, lp, _ = z3.shape
    npairs = ATT_HEADS // 2
    qb, kb, vb = Z_Q // LANES, Z_K // LANES, Z_V // LANES
    assert tq % tk == 0 and lp >= tq + tk, (tq, tk, lp)
    first = _first_live_chunk(frow, qg, kg, tq, tk)
    grid_spec = pltpu.PrefetchScalarGridSpec(
        num_scalar_prefetch=1,
        grid=(b, npairs, lp // tq),
        in_specs=[pl.BlockSpec((1, tq, LANES), lambda bi, p, i, lo: (bi, i, qb + p)),
                  pl.BlockSpec((1, lp, LANES), lambda bi, p, i, lo: (bi, 0, kb + p)),
                  pl.BlockSpec((1, lp, LANES), lambda bi, p, i, lo: (bi, 0, vb + p)),
                  pl.BlockSpec((1, tq, LANES), lambda bi, p, i, lo: (bi, i, 0)),
                  pl.BlockSpec((1, lp, LANES), lambda bi, p, i, lo: (bi, 0, 0)),
                  pl.BlockSpec((1, LANES), lambda bi, p, i, lo: (0, 0)),
                  pl.BlockSpec((1, LANES), lambda bi, p, i, lo: (0, 0))],
        out_specs=pl.BlockSpec((1, tq, LANES), lambda bi, p, i, lo: (bi, i, p)),
        scratch_shapes=[pltpu.VMEM((lp, 2 * LANES), BF16)],
    )
    return pl.pallas_call(
        functools.partial(_attn_kernel, tq=tq, tk=tk),
        grid_spec=grid_spec,
        out_shape=jax.ShapeDtypeStruct((b, lp, ATT_W), BF16),
        compiler_params=_cparams(("parallel", "parallel", "arbitrary")),
        name="fox_attention",
    )(first, z3, z3, z3, fcol, fcol, qg * LOG2E, kg)


def _convpool_kernel(a_ref, g_ref, p_ref, ah_ref, gh_ref, ph_ref, cw_ref, cb_ref, lg_ref, lb_ref,
                     pw_ref, ps_ref, u_ref, pm_ref, ext_ref, pext_ref, *, ts):
    i = pl.program_id(1)
    has_prev = i > 0

    u = a_ref[0].astype(F32) * _sigmoid(g_ref[0].astype(F32))
    uh = ah_ref[0].astype(F32) * _sigmoid(gh_ref[0].astype(F32))
    ext_ref[0:HALO, :] = jnp.where(has_prev, uh, 0.0)
    ext_ref[HALO:HALO + ts, :] = u
    acc = jnp.zeros((ts, CONV_W), F32) + cb_ref[...]
    for j in range(CONV_K):
        start = HALO - (CONV_K - 1) + j
        acc = acc + cw_ref[j:j + 1, :] * ext_ref[start:start + ts, :]
    mu = jnp.mean(acc, axis=-1, keepdims=True)
    cen = acc - mu
    var = jnp.mean(cen * cen, axis=-1, keepdims=True)
    y = cen * lax.rsqrt(var + 1e-5) * lg_ref[...] + lb_ref[...]
    u_ref[0] = _silu(y).astype(u_ref.dtype)

    x = p_ref[0].astype(F32)
    pext_ref[0:HALO, :] = jnp.where(has_prev, ph_ref[0].astype(F32), 0.0)
    pext_ref[HALO:HALO + ts, :] = x
    pos1 = (i * ts + 1 + lax.broadcasted_iota(jnp.int32, (ts, 1), 0)).astype(F32)
    lane = lax.broadcasted_iota(jnp.int32, (1, POOL_W), 1)
    group_w = POOL_W // len(POOL_WINDOWS)
    run = x
    pooled = jnp.zeros((ts, POOL_W), F32)
    for k in range(1, max(POOL_WINDOWS)):
        run = run + pext_ref[HALO - k:HALO - k + ts, :]
        if (k + 1) in POOL_WINDOWS:
            gi = POOL_WINDOWS.index(k + 1)
            mean = run / jnp.minimum(pos1, float(k + 1))
            in_group = (lane >= gi * group_w) & (lane < (gi + 1) * group_w)
            pooled = jnp.where(in_group, mean, pooled)
    pm = (pooled - x).astype(BF16)
    lin = jnp.dot(pm, pw_ref[...], preferred_element_type=F32) * ps_ref[...]
    pm_ref[0] = lin.astype(pm_ref.dtype)


def _convpool(z3, cw, cb, lg, lb, pw, ps, ts):
    b, lp, _ = z3.shape
    ab, gb, pb = Z_A // CONV_W, Z_G // CONV_W, Z_P // POOL_W
    hpt = ts // HALO
    main = lambda blk: pl.BlockSpec((1, ts, CONV_W), lambda bi, i: (bi, i, blk))
    halo = lambda blk: pl.BlockSpec((1, HALO, CONV_W), lambda bi, i: (bi, jnp.maximum(i * hpt - 1, 0), blk))
    const = lambda shape: pl.BlockSpec(shape, lambda bi, i: (0, 0))
    out = pl.BlockSpec((1, ts, CONV_W), lambda bi, i: (bi, i, 0))
    return pl.pallas_call(
        functools.partial(_convpool_kernel, ts=ts),
        grid=(b, lp // ts),
        in_specs=[main(ab), main(gb), main(pb), halo(ab), halo(gb), halo(pb),
                  const((HALO, CONV_W)), const((1, CONV_W)), const((1, CONV_W)), const((1, CONV_W)),
                  const((POOL_W, POOL_W)), const((1, POOL_W))],
        out_specs=[out, out],
        out_shape=[jax.ShapeDtypeStruct((b, lp, CONV_W), BF16), jax.ShapeDtypeStruct((b, lp, POOL_W), BF16)],
        scratch_shapes=[pltpu.VMEM((HALO + ts, CONV_W), F32), pltpu.VMEM((HALO + ts, POOL_W), F32)],
        compiler_params=_cparams(("parallel", "arbitrary")),
        name="conv_pool",
    )(z3, z3, z3, z3, z3, z3, cw, cb, lg, lb, pw, ps)


def _merge_kernel(att_ref, u_ref, pm_ref, g0_ref, g1_ref, g2_ref, h_ref, wa_ref, wc_ref, wp_ref, wo_ref,
                  ng_ref, h_out_ref, hn_out_ref):
    ya = jnp.dot(att_ref[...], wa_ref[...], preferred_element_type=F32)
    yc = jnp.dot(u_ref[...], wc_ref[...], preferred_element_type=F32)
    yp = jnp.dot(pm_ref[...], wp_ref[...], preferred_element_type=F32)
    m = (_sigmoid(g0_ref[...].astype(F32)) * ya + _sigmoid(g1_ref[...].astype(F32)) * yc
         + _sigmoid(g2_ref[...].astype(F32)) * yp)
    h = h_ref[...] + jnp.dot(m.astype(BF16), wo_ref[...], preferred_element_type=F32)
    h_out_ref[...] = h
    hn_out_ref[...] = _rms(h, ng_ref[...]).astype(hn_out_ref.dtype)


def _merge(att, u, pm, z, h, wa, wc, wp, wo, ng, tm):
    n, d = h.shape
    row = lambda w, blk=0: pl.BlockSpec((tm, w), lambda i: (i, blk))
    const = lambda shape: pl.BlockSpec(shape, lambda i: (0, 0))
    return pl.pallas_call(
        _merge_kernel,
        grid=(n // tm,),
        in_specs=[row(ATT_W), row(CONV_W), row(POOL_W), row(d, 0), row(d, 1), row(d, 2), row(d),
                  const(wa.shape), const(wc.shape), const(wp.shape), const(wo.shape), const((1, d))],
        out_specs=[row(d), row(d)],
        out_shape=[jax.ShapeDtypeStruct((n, d), F32), jax.ShapeDtypeStruct((n, d), BF16)],
        compiler_params=_cparams(("parallel",)),
        name="merge_outproj",
    )(att, u, pm, z, z, z, h, wa, wc, wp, wo, ng)


def _ffn_kernel(x_ref, h_ref, wg_ref, wu_ref, wd_ref, ng_ref, h_out_ref, hn_out_ref, *, tf):
    x = x_ref[...]
    h = h_ref[...]
    for c in range(wg_ref.shape[1] // tf):
        gt = jnp.dot(x, wg_ref[:, c * tf:(c + 1) * tf], preferred_element_type=F32)
        up = jnp.dot(x, wu_ref[:, c * tf:(c + 1) * tf], preferred_element_type=F32)
        h = h + jnp.dot((_silu(gt) * up).astype(BF16), wd_ref[c * tf:(c + 1) * tf, :],
                        preferred_element_type=F32)
    h_out_ref[...] = h
    hn_out_ref[...] = _rms(h, ng_ref[...]).astype(hn_out_ref.dtype)


def _ffn(hn, h, wg, wu, wd, ng, tm, tf):
    n, d = h.shape
    row = pl.BlockSpec((tm, d), lambda i: (i, 0))
    const = lambda shape: pl.BlockSpec(shape, lambda i: (0, 0), pipeline_mode=pl.Buffered(1))
    return pl.pallas_call(
        functools.partial(_ffn_kernel, tf=tf),
        grid=(n // tm,),
        in_specs=[row, row, const(wg.shape), const(wu.shape), const(wd.shape), const((1, d))],
        out_specs=[row, row],
        out_shape=[jax.ShapeDtypeStruct((n, d), F32), jax.ShapeDtypeStruct((n, d), BF16)],
        compiler_params=_cparams(("parallel",)),
        name="dense_swiglu",
    )(hn, h, wg, wu, wd, ng)


def _router_kernel(h_ref, ng_ref, wr_ref, br_ref, comb_ref, pos_ref, cnt_ref, *, sb):
    tm = h_ref.shape[0]
    hn = _rms(h_ref[...], ng_ref[...])
    logits = jnp.dot(hn, wr_ref[...], precision=HIGHEST, preferred_element_type=F32) + br_ref[...]
    lane = lax.broadcasted_iota(jnp.int32, (1, LANES), 1).astype(F32)
    lg = jnp.where(lane < N_EXPERTS, logits, -jnp.inf)
    m1 = jnp.max(lg, axis=-1, keepdims=True)
    i1 = jnp.min(jnp.where(lg == m1, lane, float(LANES)), axis=-1, keepdims=True)
    sel1 = lane == i1
    lg2 = jnp.where(sel1, -jnp.inf, lg)
    m2 = jnp.max(lg2, axis=-1, keepdims=True)
    i2 = jnp.min(jnp.where(lg2 == m2, lane, float(LANES)), axis=-1, keepdims=True)
    sel2 = lane == i2
    e = jnp.exp(m2 - m1)
    g1 = 1.0 / (1.0 + e)
    comb_ref[...] = jnp.where(sel1, g1, 0.0) + jnp.where(sel2, e * g1, 0.0)
    sel = jnp.where(sel1 | sel2, 1.0, 0.0)
    r = lax.broadcasted_iota(jnp.int32, (sb, sb), 0)
    c = lax.broadcasted_iota(jnp.int32, (sb, sb), 1)
    tril = jnp.where(c <= r, 1.0, 0.0).astype(BF16)
    carry = jnp.zeros((1, LANES), F32)
    for s in range(tm // sb):
        blk = slice(s * sb, (s + 1) * sb)
        incl = jnp.dot(tril, sel[blk].astype(BF16), preferred_element_type=F32) + carry
        pos_ref[blk, :] = jnp.where(sel[blk] > 0.0, incl - 1.0, -1.0)
        carry = incl[sb - 1:sb, :]
    cnt_ref[0] = jnp.broadcast_to(carry, (8, LANES)).astype(jnp.int32)


def _router(h, ng, wr, br, tm, sb):
    n, d = h.shape
    nt = n // tm
    return pl.pallas_call(
        functools.partial(_router_kernel, sb=sb),
        grid=(nt,),
        in_specs=[pl.BlockSpec((tm, d), lambda i: (i, 0)),
                  pl.BlockSpec((1, d), lambda i: (0, 0)),
                  pl.BlockSpec((d, LANES), lambda i: (0, 0)),
                  pl.BlockSpec((1, LANES), lambda i: (0, 0))],
        out_specs=[pl.BlockSpec((tm, LANES), lambda i: (i, 0)),
                   pl.BlockSpec((tm, LANES), lambda i: (i, 0)),
                   pl.BlockSpec((1, 8, LANES), lambda i: (i, 0, 0))],
        out_shape=[jax.ShapeDtypeStruct((n, LANES), F32), jax.ShapeDtypeStruct((n, LANES), F32),
                   jax.ShapeDtypeStruct((nt, 8, LANES), jnp.int32)],
        compiler_params=_cparams(("parallel",)),
        name="router_top2",
    )(h, ng, wr, br)


def _moe_kernel(cnt_ref, x_ref, post_ref, pos_ref, comb_ref, wg_ref, wu_ref, wd_ref,
                out_ref, xs_ref, ys_ref, *, sizes, sub):
    i = pl.program_id(0)
    e = pl.program_id(1)
    f = pl.program_id(2)
    nf = pl.num_programs(2)
    cnt = cnt_ref[i * N_EXPERTS + e]
    big = sizes[-1]

    def for_each_chunk(body):
        lo = 0
        for r in sizes:
            @pl.when((cnt > lo) & (cnt <= r))
            def _(r=r):
                body(0, r)
            lo = r

        @pl.when(cnt > big)
        def _():
            def step(c, carry):
                body(pl.multiple_of(c * big, big), big)
                return carry
            lax.fori_loop(0, (cnt + big - 1) // big, step, 0)

    @pl.when((e == 0) & (f == 0))
    def _():
        out_ref[...] = jnp.zeros_like(out_ref)

    @pl.when(f == 0)
    def _():
        prow = post_ref[0, pl.ds(e, 1), :]

        def gather(off, r):
            tgt = (off + lax.broadcasted_iota(jnp.int32, (r, 1), 0)).astype(F32)
            onehot = jnp.where(prow == tgt, 1.0, 0.0).astype(BF16)
            xs_ref[pl.ds(off, r), :] = jnp.dot(onehot, x_ref[...], preferred_element_type=F32).astype(BF16)
            ys_ref[pl.ds(off, r), :] = jnp.zeros((r, D_MODEL), F32)
        for_each_chunk(gather)

    def expert(off, r):
        xc = xs_ref[pl.ds(off, r), :]
        y = ys_ref[pl.ds(off, r), :]
        for s in range(wg_ref.shape[2] // sub):
            cols = slice(s * sub, (s + 1) * sub)
            gt = jnp.dot(xc, wg_ref[0, :, cols], preferred_element_type=F32)
            up = jnp.dot(xc, wu_ref[0, :, cols], preferred_element_type=F32)
            y = y + jnp.dot((_silu(gt) * up).astype(BF16), wd_ref[0, cols, :], preferred_element_type=F32)
        ys_ref[pl.ds(off, r), :] = y
    for_each_chunk(expert)

    @pl.when(f == nf - 1)
    def _():
        lane = lax.broadcasted_iota(jnp.int32, (1, LANES), 1)
        pcol = jnp.sum(jnp.where(lane == e, pos_ref[...], 0.0), axis=-1, keepdims=True)
        gcol = jnp.sum(jnp.where(lane == e, comb_ref[...], 0.0), axis=-1, keepdims=True)

        def scatter(off, r):
            tgt = (off + lax.broadcasted_iota(jnp.int32, (1, r), 1)).astype(F32)
            onehot_t = jnp.where(pcol == tgt, 1.0, 0.0).astype(BF16)
            y = ys_ref[pl.ds(off, r), :].astype(BF16)
            out_ref[...] += gcol * jnp.dot(onehot_t, y, preferred_element_type=F32)
        for_each_chunk(scatter)


def _moe(cnt, hn, post, pos, comb, wg, wu, wd, tm, tf, sizes, sub):
    n, d = hn.shape
    nf = wg.shape[2] // tf
    cap = -(-tm // sizes[-1]) * sizes[-1]
    once = pl.Buffered(1)
    row = lambda w: pl.BlockSpec((tm, w), lambda i, e, f, cnt: (i, 0), pipeline_mode=once)
    grid_spec = pltpu.PrefetchScalarGridSpec(
        num_scalar_prefetch=1,
        grid=(n // tm, N_EXPERTS, nf),
        in_specs=[row(d),
                  pl.BlockSpec((1, POST_ROWS, tm), lambda i, e, f, cnt: (i, 0, 0), pipeline_mode=once),
                  row(LANES), row(LANES),
                  pl.BlockSpec((1, d, tf), lambda i, e, f, cnt: (e, 0, f)),
                  pl.BlockSpec((1, d, tf), lambda i, e, f, cnt: (e, 0, f)),
                  pl.BlockSpec((1, tf, d), lambda i, e, f, cnt: (e, f, 0))],
        out_specs=pl.BlockSpec((tm, d), lambda i, e, f, cnt: (i, 0)),
        scratch_shapes=[pltpu.VMEM((cap, d), BF16), pltpu.VMEM((cap, d), F32)],
    )
    return pl.pallas_call(
        functools.partial(_moe_kernel, sizes=sizes, sub=sub),
        grid_spec=grid_spec,
        out_shape=jax.ShapeDtypeStruct((n, d), F32),
        compiler_params=pltpu.CompilerParams(dimension_semantics=("parallel", "arbitrary", "arbitrary"),
                                             vmem_limit_bytes=MOE_VMEM_LIMIT),
        name="expert_swiglu",
    )(cnt, hn, post, pos, comb, wg, wu, wd)


def _add_norm_kernel(h_ref, d_ref, g_ref, h_out_ref, hn_out_ref):
    h = h_ref[...] + d_ref[...]
    h_out_ref[...] = h
    hn_out_ref[...] = _rms(h, g_ref[...]).astype(hn_out_ref.dtype)


def _add_norm(h, delta, g, tm):
    n, d = h.shape
    row = pl.BlockSpec((tm, d), lambda i: (i, 0))
    return pl.pallas_call(
        _add_norm_kernel,
        grid=(n // tm,),
        in_specs=[row, row, pl.BlockSpec((1, d), lambda i: (0, 0))],
        out_specs=[row, row],
        out_shape=[jax.ShapeDtypeStruct((n, d), F32), jax.ShapeDtypeStruct((n, d), BF16)],
        compiler_params=_cparams(("parallel",)),
        name="residual_norm",
    )(h, delta, g)


def _pad_lanes(a, width=LANES):
    return jnp.pad(a, ((0, 0), (0, width - a.shape[-1])))


def kernel(x, meta, norm_mix, w_in, b_in, q_norm, k_norm, w_attn_o, conv_w, conv_b, conv_ln_g, conv_ln_b,
           w_conv_o, pool_w, pool_scale, w_pool_o, w_out, norm_ffn, w_ff_gate, w_ff_up, w_ff_down, w_router,
           b_router, w_e_gate, w_e_up, w_e_down):
    bsz, seq, d = x.shape
    depth = w_in.shape[0]
    length = seq + N_META
    lp = -(-length // SEQ_ALIGN) * SEQ_ALIGN
    n = bsz * lp

    tm = _pick(n, (1536, 768, 512, 256))
    tm_mid = _pick(n, (768, 512, 256))
    tm_ffn = _pick(n, (768, 512, 256))
    tm_moe = _pick(n, (1408, 768, 512, 256))
    sb_router = _pick(tm_moe, (704, 768, 512, 256))
    tn = _pick(Z_W, (1792, 768, 256))
    ts = _pick(lp, (768, 512, 256))
    ts_f = ts
    tk = 256
    tq = _pick(lp, (768, 256)) if lp >= 1024 else 256
    tf_dense = _pick(w_ff_gate.shape[2], (256,))
    tf_moe = _pick(w_e_gate.shape[3], (1792, 512, 256))
    sub_moe = _pick(tf_moe, (256,))
    quarter = tm_moe // 4
    moe_sizes = tuple(range(max(quarter - 96, 64) // 64 * 64, quarter + 161, 64))

    h = jnp.concatenate([jnp.broadcast_to(meta[None].astype(x.dtype), (bsz, N_META, d)), x], axis=1)
    h = jnp.pad(h, ((0, 0), (0, lp - length), (0, 0))).reshape(n, d)
    hn = _norm(h, norm_mix[0][None], tm)

    att_scale = HEAD_DIM ** -0.5
    gate_lo = 3 * ATT_W + ATT_HEADS + 2 * CONV_W + POOL_W
    f_lo = 3 * ATT_W
    for l in range(depth):
        wl = w_in[l]
        w_main = jnp.concatenate([wl[:, gate_lo:], wl[:, :f_lo], wl[:, f_lo + ATT_HEADS:gate_lo]], axis=1).astype(BF16)
        bl = b_in[l]
        b_main = jnp.concatenate([bl[gate_lo:], bl[:f_lo], bl[f_lo + ATT_HEADS:gate_lo]])[None]
        w_f = _pad_lanes(wl[:, f_lo:f_lo + ATT_HEADS])
        w_f_hi = w_f.astype(BF16)
        w_f = jnp.concatenate([w_f_hi, (w_f - w_f_hi.astype(F32)).astype(BF16)], axis=1)
        b_f = _pad_lanes(bl[None, f_lo:f_lo + ATT_HEADS])

        z = _inproj(hn, w_main, b_main, tm, tn)
        z3 = z.reshape(bsz, lp, Z_W)
        fcol = _forget(hn.reshape(bsz, lp, d), w_f, b_f, ts_f)
        frow = jnp.transpose(fcol[:, :, :ATT_HEADS], (0, 2, 1))
        qg = jnp.tile(q_norm[l], 2)[None] * att_scale
        kg = jnp.tile(k_norm[l], 2)[None]
        att = _attention(z3, fcol, frow, qg, kg, tq, tk)

        cw = jnp.pad(conv_w[l], ((0, HALO - CONV_K), (0, 0)))
        pw = jax.scipy.linalg.block_diag(*[pool_w[l, g] for g in range(pool_w.shape[1])]).astype(BF16)
        u, pm = _convpool(z3, cw, conv_b[l][None], conv_ln_g[l][None], conv_ln_b[l][None], pw,
                          pool_scale[l][None], ts)

        h, hn = _merge(att.reshape(n, ATT_W), u.reshape(n, CONV_W), pm.reshape(n, POOL_W), z, h,
                       w_attn_o[l].astype(BF16), w_conv_o[l].astype(BF16), w_pool_o[l].astype(BF16),
                       w_out[l].astype(BF16), norm_ffn[l][None], tm_mid)

        ng_next = norm_mix[min(l + 1, depth - 1)][None]
        i = l // 2
        if l % 2 == 0:
            h, hn = _ffn(hn, h, w_ff_gate[i].astype(BF16), w_ff_up[i].astype(BF16), w_ff_down[i].astype(BF16),
                         ng_next, tm_ffn, tf_dense)
        else:
            comb, pos, cnt = _router(h, norm_ffn[l][None], _pad_lanes(w_router[i]), _pad_lanes(b_router[i][None]),
                                     tm_moe, sb_router)
            post = jnp.transpose(pos[:, :POST_ROWS].reshape(n // tm_moe, tm_moe, POST_ROWS), (0, 2, 1))
            cnt_flat = cnt[:, 0, :N_EXPERTS].reshape(-1)
            delta = _moe(cnt_flat, hn, post, pos, comb, w_e_gate[i].astype(BF16), w_e_up[i].astype(BF16),
                         w_e_down[i].astype(BF16), tm_moe, tf_moe, moe_sizes, sub_moe)
            h, hn = _add_norm(h, delta, ng_next, tm)

    return h.reshape(bsz, lp, d)[:, N_META:length]
```

```python
import functools

import jax
import jax.numpy as jnp
from jax import lax
from jax.experimental import pallas as pl
from jax.experimental.pallas import tpu as pltpu

F32 = jnp.float32
BF16 = jnp.bfloat16
HIGHEST = lax.Precision.HIGHEST

D_MODEL = 1024
N_META = 16
HEAD_DIM = 64
ATT_W = 512
ATT_HEADS = 8
CONV_W = 256
CONV_K = 31
POOL_W = 256
POOL_WINDOWS = (2, 4, 8, 16)
N_EXPERTS = 8
LANES = 128
HALO = 32
NEG_INF = -1e30
EXP_UNDERFLOW = 110.0
SEQ_ALIGN = 256
VMEM_LIMIT = 56 * 1024 * 1024
MOE_VMEM_LIMIT = 60 * 1024 * 1024
POST_ROWS = 32

Z_GATE = 0
Z_Q = 3072
Z_K = 3584
Z_V = 4096
Z_A = 4608
Z_G = 4864
Z_P = 5120
Z_W = 5376


def _pick(n, candidates):
    for c in candidates:
        if n % c == 0:
            return c
    raise ValueError(f"no tile for {n} in {candidates}")


def _cparams(sem):
    return pltpu.CompilerParams(dimension_semantics=sem, vmem_limit_bytes=VMEM_LIMIT)


def _rms(x, g, eps=1e-6):
    return x * lax.rsqrt(jnp.mean(x * x, axis=-1, keepdims=True) + eps) * g


def _sigmoid(x):
    return 1.0 / (1.0 + jnp.exp(-x))


def _silu(x):
    return x * _sigmoid(x)


def _norm_kernel(h_ref, g_ref, o_ref):
    o_ref[...] = _rms(h_ref[...], g_ref[...]).astype(o_ref.dtype)


def _norm(h, g, tm):
    n, d = h.shape
    return pl.pallas_call(
        _norm_kernel,
        grid=(n // tm,),
        in_specs=[pl.BlockSpec((tm, d), lambda i: (i, 0)), pl.BlockSpec((1, d), lambda i: (0, 0))],
        out_specs=pl.BlockSpec((tm, d), lambda i: (i, 0)),
        out_shape=jax.ShapeDtypeStruct((n, d), BF16),
        compiler_params=_cparams(("parallel",)),
        name="rmsnorm",
    )(h, g)


def _inproj_kernel(x_ref, w_ref, b_ref, o_ref):
    acc = jnp.dot(x_ref[...], w_ref[...], preferred_element_type=F32)
    o_ref[...] = (acc + b_ref[...]).astype(o_ref.dtype)


def _inproj(hn, w, b, tm, tn):
    n, d = hn.shape
    zw = w.shape[1]
    return pl.pallas_call(
        _inproj_kernel,
        grid=(n // tm, zw // tn),
        in_specs=[pl.BlockSpec((tm, d), lambda i, j: (i, 0)),
                  pl.BlockSpec((d, tn), lambda i, j: (0, j)),
                  pl.BlockSpec((1, tn), lambda i, j: (0, j))],
        out_specs=pl.BlockSpec((tm, tn), lambda i, j: (i, j)),
        out_shape=jax.ShapeDtypeStruct((n, zw), BF16),
        compiler_params=_cparams(("parallel", "arbitrary")),
        name="inproj",
    )(hn, w, b)


def _forget_kernel(x_ref, w_ref, b_ref, o_ref, carry_ref):
    @pl.when(pl.program_id(1) == 0)
    def _():
        carry_ref[...] = jnp.zeros_like(carry_ref)

    fw = jnp.dot(x_ref[0], w_ref[...], preferred_element_type=F32)
    f = fw[:, :LANES] + fw[:, LANES:] + b_ref[...]
    ls = jnp.minimum(f, 0.0) - jnp.log(1.0 + jnp.exp(-jnp.abs(f)))
    hi = ls.astype(BF16)
    r1 = ls - hi.astype(F32)
    mid = r1.astype(BF16)
    lo = (r1 - mid.astype(F32)).astype(BF16)
    t = ls.shape[0]
    r = lax.broadcasted_iota(jnp.int32, (t, t), 0)
    c = lax.broadcasted_iota(jnp.int32, (t, t), 1)
    tril = jnp.where(c <= r, 1.0, 0.0).astype(BF16)
    parts = jnp.dot(tril, jnp.concatenate([hi, mid, lo], axis=-1), preferred_element_type=F32)
    cs = (parts[:, :LANES] + parts[:, LANES:2 * LANES]) + parts[:, 2 * LANES:] + carry_ref[...]
    o_ref[0] = cs
    carry_ref[...] = cs[t - 1:t, :]


def _forget(hn3, w_f, b_f, ts):
    b, lp, d = hn3.shape
    return pl.pallas_call(
        _forget_kernel,
        grid=(b, lp // ts),
        in_specs=[pl.BlockSpec((1, ts, d), lambda bi, t: (bi, t, 0)),
                  pl.BlockSpec((d, 2 * LANES), lambda bi, t: (0, 0)),
                  pl.BlockSpec((1, LANES), lambda bi, t: (0, 0))],
        out_specs=pl.BlockSpec((1, ts, LANES), lambda bi, t: (bi, t, 0)),
        out_shape=jax.ShapeDtypeStruct((b, lp, LANES), F32),
        scratch_shapes=[pltpu.VMEM((1, LANES), F32)],
        compiler_params=_cparams(("parallel", "arbitrary")),
        name="forget_cumsum",
    )(hn3, w_f, b_f)


def _pair_rms(x, first_head, gain, eps=1e-6):
    sq = x * x
    s0 = jnp.sum(jnp.where(first_head, sq, 0.0), axis=-1, keepdims=True)
    s1 = jnp.sum(jnp.where(first_head, 0.0, sq), axis=-1, keepdims=True)
    ms = jnp.where(first_head, s0, s1) * (1.0 / HEAD_DIM)
    return x * lax.rsqrt(ms + eps) * gain


def _attn_kernel(lo_ref, q_ref, k_ref, v_ref, fc_ref, fr_ref, qg_ref, kg_ref, o_ref, kn_ref, *, tq, tk):
    bi = pl.program_id(0)
    p = pl.program_id(1)
    i = pl.program_id(2)
    nq = pl.num_programs(2)
    lp = k_ref.shape[1]
    lane = lax.broadcasted_iota(jnp.int32, (1, LANES), 1)
    first_head = lane < HEAD_DIM

    @pl.when(i == 0)
    def _():
        def body(c, carry):
            off = pl.multiple_of(c * tk, tk)
            kk = k_ref[0, pl.ds(off, tk), :].astype(F32)
            kn_ref[pl.ds(off, tk), :] = _pair_rms(kk, first_head, kg_ref[...]).astype(BF16)
            return carry
        lax.fori_loop(0, lp // tk, body, 0)

    qn = _pair_rms(q_ref[0].astype(F32), first_head, qg_ref[...])
    fc = fc_ref[0]
    row_pos = i * tq + lax.broadcasted_iota(jnp.int32, (tq, 1), 0)

    heads = [2 * p, 2 * p + 1]
    qhs = [jnp.where(first_head, qn, 0.0).astype(BF16), jnp.where(first_head, 0.0, qn).astype(BF16)]
    fts = [jnp.sum(jnp.where(lane == hd, fc, 0.0), axis=-1, keepdims=True) for hd in heads]

    def update(off, width, carry, hh, masked):
        m, l, acc = carry
        ks = kn_ref[pl.ds(off, width), :]
        s = lax.dot_general(qhs[hh], ks, (((1,), (1,)), ((), ())), preferred_element_type=F32)
        fs = fr_ref[0, pl.ds(heads[hh], 1), pl.ds(off, width)]
        s = s + (fts[hh] - fs)
        if masked:
            col_pos = off + lax.broadcasted_iota(jnp.int32, (1, width), 1)
            s = jnp.where(col_pos <= row_pos, s, NEG_INF)
        m_new = jnp.maximum(m, jnp.max(s, axis=-1, keepdims=True))
        alpha = jnp.exp(m - m_new)
        pm = jnp.exp(s - m_new)
        l = alpha * l + jnp.sum(pm, axis=-1, keepdims=True)
        vs = v_ref[0, pl.ds(off, width), :]
        acc = alpha * acc + jnp.dot(pm.astype(BF16), vs, preferred_element_type=F32)
        return m_new, l, acc

    win = jnp.maximum(i * (tq // tk) - 1, 0)
    firsts = [lo_ref[(bi * ATT_HEADS + hd) * nq + i] for hd in heads]
    init = (jnp.full((tq, 1), NEG_INF, F32), jnp.zeros((tq, 1), F32), jnp.zeros((tq, LANES), F32))

    def far(j, carry):
        off = pl.multiple_of(j * tk, tk)
        return tuple(update(off, tk, carry[hh], hh, masked=False) for hh in range(2))

    carry = lax.fori_loop(jnp.minimum(firsts[0], firsts[1]), win, far, (init, init))
    off = pl.multiple_of(win * tk, tk)
    outs = []
    for hh in range(2):
        m, l, acc = update(off, tq + tk, carry[hh], hh, masked=True)
        outs.append(acc / l)
    o_ref[0] = jnp.where(first_head, outs[0], outs[1]).astype(o_ref.dtype)


def _first_live_chunk(frow, qg, kg, tq, tk):
    b, nh, lp = frow.shape
    qk_bound = 1.02 * HEAD_DIM * jnp.max(jnp.abs(qg)) * jnp.max(jnp.abs(kg))
    f_first = frow[:, :, 0::tq]
    f_last = frow[:, :, tk - 1::tk]
    dead = (f_first[:, :, :, None] - f_last[:, :, None, :] + 2.0 * qk_bound) < -EXP_UNDERFLOW
    n_dead = jnp.sum(dead.astype(jnp.int32), axis=-1)
    n_full = (jnp.arange(lp // tq, dtype=jnp.int32) * tq) // tk
    return jnp.minimum(n_dead, n_full[None, None, :]).reshape(-1)


def _attention(z3, fcol, frow, qg, kg, tq, tk):
    b, lp, _ = z3.shape
    npairs = ATT_HEADS // 2
    qb, kb, vb = Z_Q // LANES, Z_K // LANES, Z_V // LANES
    assert tq % tk == 0 and lp >= tq + tk, (tq, tk, lp)
    first = _first_live_chunk(frow, qg, kg, tq, tk)
    grid_spec = pltpu.PrefetchScalarGridSpec(
        num_scalar_prefetch=1,
        grid=(b, npairs, lp // tq),
        in_specs=[pl.BlockSpec((1, tq, LANES), lambda bi, p, i, lo: (bi, i, qb + p)),
                  pl.BlockSpec((1, lp, LANES), lambda bi, p, i, lo: (bi, 0, kb + p)),
                  pl.BlockSpec((1, lp, LANES), lambda bi, p, i, lo: (bi, 0, vb + p)),
                  pl.BlockSpec((1, tq, LANES), lambda bi, p, i, lo: (bi, i, 0)),
                  pl.BlockSpec((1, ATT_HEADS, lp), lambda bi, p, i, lo: (bi, 0, 0)),
                  pl.BlockSpec((1, LANES), lambda bi, p, i, lo: (0, 0)),
                  pl.BlockSpec((1, LANES), lambda bi, p, i, lo: (0, 0))],
        out_specs=pl.BlockSpec((1, tq, LANES), lambda bi, p, i, lo: (bi, i, p)),
        scratch_shapes=[pltpu.VMEM((lp, LANES), BF16)],
    )
    return pl.pallas_call(
        functools.partial(_attn_kernel, tq=tq, tk=tk),
        grid_spec=grid_spec,
        out_shape=jax.ShapeDtypeStruct((b, lp, ATT_W), BF16),
        compiler_params=_cparams(("parallel", "parallel", "arbitrary")),
        name="fox_attention",
    )(first, z3, z3, z3, fcol, frow, qg, kg)


def _convpool_kernel(a_ref, g_ref, p_ref, ah_ref, gh_ref, ph_ref, cw_ref, cb_ref, lg_ref, lb_ref,
                     pw_ref, ps_ref, u_ref, pm_ref, ext_ref, pext_ref, *, ts):
    i = pl.program_id(1)
    has_prev = i > 0

    u = a_ref[0].astype(F32) * _sigmoid(g_ref[0].astype(F32))
    uh = ah_ref[0].astype(F32) * _sigmoid(gh_ref[0].astype(F32))
    ext_ref[0:HALO, :] = jnp.where(has_prev, uh, 0.0)
    ext_ref[HALO:HALO + ts, :] = u
    acc = jnp.zeros((ts, CONV_W), F32) + cb_ref[...]
    for j in range(CONV_K):
        start = HALO - (CONV_K - 1) + j
        acc = acc + cw_ref[j:j + 1, :] * ext_ref[start:start + ts, :]
    mu = jnp.mean(acc, axis=-1, keepdims=True)
    cen = acc - mu
    var = jnp.mean(cen * cen, axis=-1, keepdims=True)
    y = cen * lax.rsqrt(var + 1e-5) * lg_ref[...] + lb_ref[...]
    u_ref[0] = _silu(y).astype(u_ref.dtype)

    x = p_ref[0].astype(F32)
    pext_ref[0:HALO, :] = jnp.where(has_prev, ph_ref[0].astype(F32), 0.0)
    pext_ref[HALO:HALO + ts, :] = x
    pos1 = (i * ts + 1 + lax.broadcasted_iota(jnp.int32, (ts, 1), 0)).astype(F32)
    lane = lax.broadcasted_iota(jnp.int32, (1, POOL_W), 1)
    group_w = POOL_W // len(POOL_WINDOWS)
    run = x
    pooled = jnp.zeros((ts, POOL_W), F32)
    for k in range(1, max(POOL_WINDOWS)):
        run = run + pext_ref[HALO - k:HALO - k + ts, :]
        if (k + 1) in POOL_WINDOWS:
            gi = POOL_WINDOWS.index(k + 1)
            mean = run / jnp.minimum(pos1, float(k + 1))
            in_group = (lane >= gi * group_w) & (lane < (gi + 1) * group_w)
            pooled = jnp.where(in_group, mean, pooled)
    pm = (pooled - x).astype(BF16)
    lin = jnp.dot(pm, pw_ref[...], preferred_element_type=F32) * ps_ref[...]
    pm_ref[0] = lin.astype(pm_ref.dtype)


def _convpool(z3, cw, cb, lg, lb, pw, ps, ts):
    b, lp, _ = z3.shape
    ab, gb, pb = Z_A // CONV_W, Z_G // CONV_W, Z_P // POOL_W
    hpt = ts // HALO
    main = lambda blk: pl.BlockSpec((1, ts, CONV_W), lambda bi, i: (bi, i, blk))
    halo = lambda blk: pl.BlockSpec((1, HALO, CONV_W), lambda bi, i: (bi, jnp.maximum(i * hpt - 1, 0), blk))
    const = lambda shape: pl.BlockSpec(shape, lambda bi, i: (0, 0))
    out = pl.BlockSpec((1, ts, CONV_W), lambda bi, i: (bi, i, 0))
    return pl.pallas_call(
        functools.partial(_convpool_kernel, ts=ts),
        grid=(b, lp // ts),
        in_specs=[main(ab), main(gb), main(pb), halo(ab), halo(gb), halo(pb),
                  const((HALO, CONV_W)), const((1, CONV_W)), const((1, CONV_W)), const((1, CONV_W)),
                  const((POOL_W, POOL_W)), const((1, POOL_W))],
        out_specs=[out, out],
        out_shape=[jax.ShapeDtypeStruct((b, lp, CONV_W), BF16), jax.ShapeDtypeStruct((b, lp, POOL_W), BF16)],
        scratch_shapes=[pltpu.VMEM((HALO + ts, CONV_W), F32), pltpu.VMEM((HALO + ts, POOL_W), F32)],
        compiler_params=_cparams(("parallel", "arbitrary")),
        name="conv_pool",
    )(z3, z3, z3, z3, z3, z3, cw, cb, lg, lb, pw, ps)


def _merge_kernel(att_ref, u_ref, pm_ref, g0_ref, g1_ref, g2_ref, h_ref, wa_ref, wc_ref, wp_ref, wo_ref,
                  ng_ref, h_out_ref, hn_out_ref):
    ya = jnp.dot(att_ref[...], wa_ref[...], preferred_element_type=F32)
    yc = jnp.dot(u_ref[...], wc_ref[...], preferred_element_type=F32)
    yp = jnp.dot(pm_ref[...], wp_ref[...], preferred_element_type=F32)
    m = (_sigmoid(g0_ref[...].astype(F32)) * ya + _sigmoid(g1_ref[...].astype(F32)) * yc
         + _sigmoid(g2_ref[...].astype(F32)) * yp)
    h = h_ref[...] + jnp.dot(m.astype(BF16), wo_ref[...], preferred_element_type=F32)
    h_out_ref[...] = h
    hn_out_ref[...] = _rms(h, ng_ref[...]).astype(hn_out_ref.dtype)


def _merge(att, u, pm, z, h, wa, wc, wp, wo, ng, tm):
    n, d = h.shape
    row = lambda w, blk=0: pl.BlockSpec((tm, w), lambda i: (i, blk))
    const = lambda shape: pl.BlockSpec(shape, lambda i: (0, 0))
    return pl.pallas_call(
        _merge_kernel,
        grid=(n // tm,),
        in_specs=[row(ATT_W), row(CONV_W), row(POOL_W), row(d, 0), row(d, 1), row(d, 2), row(d),
                  const(wa.shape), const(wc.shape), const(wp.shape), const(wo.shape), const((1, d))],
        out_specs=[row(d), row(d)],
        out_shape=[jax.ShapeDtypeStruct((n, d), F32), jax.ShapeDtypeStruct((n, d), BF16)],
        compiler_params=_cparams(("parallel",)),
        name="merge_outproj",
    )(att, u, pm, z, z, z, h, wa, wc, wp, wo, ng)


def _ffn_kernel(x_ref, h_ref, wg_ref, wu_ref, wd_ref, ng_ref, h_out_ref, hn_out_ref, *, tf):
    x = x_ref[...]
    h = h_ref[...]
    nc = wg_ref.shape[1] // tf

    def gate_up(c):
        cols = slice(c * tf, (c + 1) * tf)
        return (jnp.dot(x, wg_ref[:, cols], preferred_element_type=F32),
                jnp.dot(x, wu_ref[:, cols], preferred_element_type=F32))

    gt, up = gate_up(0)
    for c in range(nc):
        act = (_silu(gt) * up).astype(BF16)
        if c + 1 < nc:
            gt, up = gate_up(c + 1)
        h = h + jnp.dot(act, wd_ref[c * tf:(c + 1) * tf, :], preferred_element_type=F32)
    h_out_ref[...] = h
    hn_out_ref[...] = _rms(h, ng_ref[...]).astype(hn_out_ref.dtype)


def _ffn(hn, h, wg, wu, wd, ng, tm, tf):
    n, d = h.shape
    row = pl.BlockSpec((tm, d), lambda i: (i, 0))
    const = lambda shape: pl.BlockSpec(shape, lambda i: (0, 0), pipeline_mode=pl.Buffered(1))
    return pl.pallas_call(
        functools.partial(_ffn_kernel, tf=tf),
        grid=(n // tm,),
        in_specs=[row, row, const(wg.shape), const(wu.shape), const(wd.shape), const((1, d))],
        out_specs=[row, row],
        out_shape=[jax.ShapeDtypeStruct((n, d), F32), jax.ShapeDtypeStruct((n, d), BF16)],
        compiler_params=_cparams(("parallel",)),
        name="dense_swiglu",
    )(hn, h, wg, wu, wd, ng)


def _router_kernel(h_ref, ng_ref, wr_ref, br_ref, comb_ref, pos_ref, cnt_ref, *, sb):
    tm = h_ref.shape[0]
    hn = _rms(h_ref[...], ng_ref[...])
    logits = jnp.dot(hn, wr_ref[...], precision=HIGHEST, preferred_element_type=F32) + br_ref[...]
    lane = lax.broadcasted_iota(jnp.int32, (1, LANES), 1).astype(F32)
    lg = jnp.where(lane < N_EXPERTS, logits, -jnp.inf)
    m1 = jnp.max(lg, axis=-1, keepdims=True)
    i1 = jnp.min(jnp.where(lg == m1, lane, float(LANES)), axis=-1, keepdims=True)
    sel1 = lane == i1
    lg2 = jnp.where(sel1, -jnp.inf, lg)
    m2 = jnp.max(lg2, axis=-1, keepdims=True)
    i2 = jnp.min(jnp.where(lg2 == m2, lane, float(LANES)), axis=-1, keepdims=True)
    sel2 = lane == i2
    e = jnp.exp(m2 - m1)
    g1 = 1.0 / (1.0 + e)
    comb_ref[...] = jnp.where(sel1, g1, 0.0) + jnp.where(sel2, e * g1, 0.0)
    sel = jnp.where(sel1 | sel2, 1.0, 0.0)
    r = lax.broadcasted_iota(jnp.int32, (sb, sb), 0)
    c = lax.broadcasted_iota(jnp.int32, (sb, sb), 1)
    tril = jnp.where(c <= r, 1.0, 0.0).astype(BF16)
    carry = jnp.zeros((1, LANES), F32)
    for s in range(tm // sb):
        blk = slice(s * sb, (s + 1) * sb)
        incl = jnp.dot(tril, sel[blk].astype(BF16), preferred_element_type=F32) + carry
        pos_ref[blk, :] = jnp.where(sel[blk] > 0.0, incl - 1.0, -1.0)
        carry = incl[sb - 1:sb, :]
    cnt_ref[0] = jnp.broadcast_to(carry, (8, LANES)).astype(jnp.int32)


def _router(h, ng, wr, br, tm, sb):
    n, d = h.shape
    nt = n // tm
    return pl.pallas_call(
        functools.partial(_router_kernel, sb=sb),
        grid=(nt,),
        in_specs=[pl.BlockSpec((tm, d), lambda i: (i, 0)),
                  pl.BlockSpec((1, d), lambda i: (0, 0)),
                  pl.BlockSpec((d, LANES), lambda i: (0, 0)),
                  pl.BlockSpec((1, LANES), lambda i: (0, 0))],
        out_specs=[pl.BlockSpec((tm, LANES), lambda i: (i, 0)),
                   pl.BlockSpec((tm, LANES), lambda i: (i, 0)),
                   pl.BlockSpec((1, 8, LANES), lambda i: (i, 0, 0))],
        out_shape=[jax.ShapeDtypeStruct((n, LANES), F32), jax.ShapeDtypeStruct((n, LANES), F32),
                   jax.ShapeDtypeStruct((nt, 8, LANES), jnp.int32)],
        compiler_params=_cparams(("parallel",)),
        name="router_top2",
    )(h, ng, wr, br)


def _moe_kernel(cnt_ref, x_ref, post_ref, pos_ref, comb_ref, wg_ref, wu_ref, wd_ref,
                out_ref, xs_ref, ys_ref, *, sizes, sub):
    i = pl.program_id(0)
    e = pl.program_id(1)
    f = pl.program_id(2)
    nf = pl.num_programs(2)
    cnt = cnt_ref[i * N_EXPERTS + e]
    big = sizes[-1]

    def for_each_chunk(body):
        lo = 0
        for r in sizes:
            @pl.when((cnt > lo) & (cnt <= r))
            def _(r=r):
                body(0, r)
            lo = r

        @pl.when(cnt > big)
        def _():
            def step(c, carry):
                body(pl.multiple_of(c * big, big), big)
                return carry
            lax.fori_loop(0, (cnt + big - 1) // big, step, 0)

    @pl.when((e == 0) & (f == 0))
    def _():
        out_ref[...] = jnp.zeros_like(out_ref)

    @pl.when(f == 0)
    def _():
        prow = post_ref[0, pl.ds(e, 1), :]

        def gather(off, r):
            tgt = (off + lax.broadcasted_iota(jnp.int32, (r, 1), 0)).astype(F32)
            onehot = jnp.where(prow == tgt, 1.0, 0.0).astype(BF16)
            xs_ref[pl.ds(off, r), :] = jnp.dot(onehot, x_ref[...], preferred_element_type=F32).astype(BF16)
            ys_ref[pl.ds(off, r), :] = jnp.zeros((r, D_MODEL), F32)
        for_each_chunk(gather)

    def expert(off, r):
        xc = xs_ref[pl.ds(off, r), :]
        y = ys_ref[pl.ds(off, r), :]
        nsub = wg_ref.shape[2] // sub

        def gate_up(s):
            cols = slice(s * sub, (s + 1) * sub)
            return (jnp.dot(xc, wg_ref[0, :, cols], preferred_element_type=F32),
                    jnp.dot(xc, wu_ref[0, :, cols], preferred_element_type=F32))

        gt, up = gate_up(0)
        for s in range(nsub):
            act = (_silu(gt) * up).astype(BF16)
            if s + 1 < nsub:
                gt, up = gate_up(s + 1)
            y = y + jnp.dot(act, wd_ref[0, s * sub:(s + 1) * sub, :], preferred_element_type=F32)
        ys_ref[pl.ds(off, r), :] = y
    for_each_chunk(expert)

    @pl.when(f == nf - 1)
    def _():
        lane = lax.broadcasted_iota(jnp.int32, (1, LANES), 1)
        pcol = jnp.sum(jnp.where(lane == e, pos_ref[...], 0.0), axis=-1, keepdims=True)
        gcol = jnp.sum(jnp.where(lane == e, comb_ref[...], 0.0), axis=-1, keepdims=True)

        def scatter(off, r):
            tgt = (off + lax.broadcasted_iota(jnp.int32, (1, r), 1)).astype(F32)
            onehot_t = jnp.where(pcol == tgt, 1.0, 0.0).astype(BF16)
            y = ys_ref[pl.ds(off, r), :].astype(BF16)
            out_ref[...] += gcol * jnp.dot(onehot_t, y, preferred_element_type=F32)
        for_each_chunk(scatter)


def _moe(cnt, hn, post, pos, comb, wg, wu, wd, tm, tf, sizes, sub):
    n, d = hn.shape
    nf = wg.shape[2] // tf
    cap = -(-tm // sizes[-1]) * sizes[-1]
    once = pl.Buffered(1)
    row = lambda w: pl.BlockSpec((tm, w), lambda i, e, f, cnt: (i, 0), pipeline_mode=once)
    grid_spec = pltpu.PrefetchScalarGridSpec(
        num_scalar_prefetch=1,
        grid=(n // tm, N_EXPERTS, nf),
        in_specs=[row(d),
                  pl.BlockSpec((1, POST_ROWS, tm), lambda i, e, f, cnt: (i, 0, 0), pipeline_mode=once),
                  row(LANES), row(LANES),
                  pl.BlockSpec((1, d, tf), lambda i, e, f, cnt: (e, 0, f)),
                  pl.BlockSpec((1, d, tf), lambda i, e, f, cnt: (e, 0, f)),
                  pl.BlockSpec((1, tf, d), lambda i, e, f, cnt: (e, f, 0))],
        out_specs=pl.BlockSpec((tm, d), lambda i, e, f, cnt: (i, 0)),
        scratch_shapes=[pltpu.VMEM((cap, d), BF16), pltpu.VMEM((cap, d), F32)],
    )
    return pl.pallas_call(
        functools.partial(_moe_kernel, sizes=sizes, sub=sub),
        grid_spec=grid_spec,
        out_shape=jax.ShapeDtypeStruct((n, d), F32),
        compiler_params=pltpu.CompilerParams(dimension_semantics=("parallel", "arbitrary", "arbitrary"),
                                             vmem_limit_bytes=MOE_VMEM_LIMIT),
        name="expert_swiglu",
    )(cnt, hn, post, pos, comb, wg, wu, wd)


def _add_norm_kernel(h_ref, d_ref, g_ref, h_out_ref, hn_out_ref):
    h = h_ref[...] + d_ref[...]
    h_out_ref[...] = h
    hn_out_ref[...] = _rms(h, g_ref[...]).astype(hn_out_ref.dtype)


def _add_norm(h, delta, g, tm):
    n, d = h.shape
    row = pl.BlockSpec((tm, d), lambda i: (i, 0))
    return pl.pallas_call(
        _add_norm_kernel,
        grid=(n // tm,),
        in_specs=[row, row, pl.BlockSpec((1, d), lambda i: (0, 0))],
        out_specs=[row, row],
        out_shape=[jax.ShapeDtypeStruct((n, d), F32), jax.ShapeDtypeStruct((n, d), BF16)],
        compiler_params=_cparams(("parallel",)),
        name="residual_norm",
    )(h, delta, g)


def _pad_lanes(a, width=LANES):
    return jnp.pad(a, ((0, 0), (0, width - a.shape[-1])))


def kernel(x, meta, norm_mix, w_in, b_in, q_norm, k_norm, w_attn_o, conv_w, conv_b, conv_ln_g, conv_ln_b,
           w_conv_o, pool_w, pool_scale, w_pool_o, w_out, norm_ffn, w_ff_gate, w_ff_up, w_ff_down, w_router,
           b_router, w_e_gate, w_e_up, w_e_down):
    bsz, seq, d = x.shape
    depth = w_in.shape[0]
    length = seq + N_META
    lp = -(-length // SEQ_ALIGN) * SEQ_ALIGN
    n = bsz * lp

    tm = _pick(n, (1536, 768, 512, 256))
    tm_mid = _pick(n, (768, 512, 256))
    tm_ffn = _pick(n, (768, 512, 256))
    tm_moe = _pick(n, (1408, 768, 512, 256))
    sb_router = _pick(tm_moe, (704, 768, 512, 256))
    tn = _pick(Z_W, (1792, 768, 256))
    ts = _pick(lp, (768, 512, 256))
    ts_f = ts
    tk = 256
    tq = _pick(lp, (768, 256)) if lp >= 1024 else 256
    tf_dense = _pick(w_ff_gate.shape[2], (256,))
    tf_moe = _pick(w_e_gate.shape[3], (1792, 512, 256))
    sub_moe = _pick(tf_moe, (256,))
    quarter = tm_moe // 4
    moe_sizes = tuple(range(max(quarter - 96, 64) // 64 * 64, quarter + 161, 64))

    h = jnp.concatenate([jnp.broadcast_to(meta[None].astype(x.dtype), (bsz, N_META, d)), x], axis=1)
    h = jnp.pad(h, ((0, 0), (0, lp - length), (0, 0))).reshape(n, d)
    hn = _norm(h, norm_mix[0][None], tm)

    att_scale = HEAD_DIM ** -0.5
    gate_lo = 3 * ATT_W + ATT_HEADS + 2 * CONV_W + POOL_W
    f_lo = 3 * ATT_W
    for l in range(depth):
        wl = w_in[l]
        w_main = jnp.concatenate([wl[:, gate_lo:], wl[:, :f_lo], wl[:, f_lo + ATT_HEADS:gate_lo]], axis=1).astype(BF16)
        bl = b_in[l]
        b_main = jnp.concatenate([bl[gate_lo:], bl[:f_lo], bl[f_lo + ATT_HEADS:gate_lo]])[None]
        w_f = _pad_lanes(wl[:, f_lo:f_lo + ATT_HEADS])
        w_f_hi = w_f.astype(BF16)
        w_f = jnp.concatenate([w_f_hi, (w_f - w_f_hi.astype(F32)).astype(BF16)], axis=1)
        b_f = _pad_lanes(bl[None, f_lo:f_lo + ATT_HEADS])

        z = _inproj(hn, w_main, b_main, tm, tn)
        z3 = z.reshape(bsz, lp, Z_W)
        fcol = _forget(hn.reshape(bsz, lp, d), w_f, b_f, ts_f)
        frow = jnp.transpose(fcol[:, :, :ATT_HEADS], (0, 2, 1))
        qg = jnp.tile(q_norm[l], 2)[None] * att_scale
        kg = jnp.tile(k_norm[l], 2)[None]
        att = _attention(z3, fcol, frow, qg, kg, tq, tk)

        cw = jnp.pad(conv_w[l], ((0, HALO - CONV_K), (0, 0)))
        pw = jax.scipy.linalg.block_diag(*[pool_w[l, g] for g in range(pool_w.shape[1])]).astype(BF16)
        u, pm = _convpool(z3, cw, conv_b[l][None], conv_ln_g[l][None], conv_ln_b[l][None], pw,
                          pool_scale[l][None], ts)

        h, hn = _merge(att.reshape(n, ATT_W), u.reshape(n, CONV_W), pm.reshape(n, POOL_W), z, h,
                       w_attn_o[l].astype(BF16), w_conv_o[l].astype(BF16), w_pool_o[l].astype(BF16),
                       w_out[l].astype(BF16), norm_ffn[l][None], tm_mid)

        ng_next = norm_mix[min(l + 1, depth - 1)][None]
        i = l // 2
        if l % 2 == 0:
            h, hn = _ffn(hn, h, w_ff_gate[i].astype(BF16), w_ff_up[i].astype(BF16), w_ff_down[i].astype(BF16),
                         ng_next, tm_ffn, tf_dense)
        else:
            comb, pos, cnt = _router(h, norm_ffn[l][None], _pad_lanes(w_router[i]), _pad_lanes(b_router[i][None]),
                                     tm_moe, sb_router)
            post = jnp.transpose(pos[:, :POST_ROWS].reshape(n // tm_moe, tm_moe, POST_ROWS), (0, 2, 1))
            cnt_flat = cnt[:, 0, :N_EXPERTS].reshape(-1)
            delta = _moe(cnt_flat, hn, post, pos, comb, w_e_gate[i].astype(BF16), w_e_up[i].astype(BF16),
                         w_e_down[i].astype(BF16), tm_moe, tf_moe, moe_sizes, sub_moe)
            h, hn = _add_norm(h, delta, ng_next, tm)

    return h.reshape(bsz, lp, d)[:, N_META:length]
```

```python
import functools

import jax
import jax.numpy as jnp
from jax import lax
from jax.experimental import pallas as pl
from jax.experimental.pallas import tpu as pltpu

F32 = jnp.float32
BF16 = jnp.bfloat16
HIGHEST = lax.Precision.HIGHEST

D_MODEL = 1024
N_META = 16
HEAD_DIM = 64
ATT_W = 512
ATT_HEADS = 8
CONV_W = 256
CONV_K = 31
POOL_W = 256
POOL_WINDOWS = (2, 4, 8, 16)
N_EXPERTS = 8
LANES = 128
SUBLANES = 8
HALO = 32
NEG_INF = -1e30
LOG2E = 1.4426950408889634
EXP_UNDERFLOW = 110.0
SEQ_ALIGN = 256
VMEM_LIMIT = 56 * 1024 * 1024
MOE_VMEM_LIMIT = 60 * 1024 * 1024
POST_ROWS = 32

Z_GATE = 0
Z_Q = 3072
Z_K = 3584
Z_V = 4096
Z_A = 4608
Z_G = 4864
Z_P = 5120
Z_W = 5376


def _pick(n, candidates):
    for c in candidates:
        if n % c == 0:
            return c
    raise ValueError(f"no tile for {n} in {candidates}")


def _cparams(sem):
    return pltpu.CompilerParams(dimension_semantics=sem, vmem_limit_bytes=VMEM_LIMIT)


def _rms(x, g, eps=1e-6):
    return x * lax.rsqrt(jnp.mean(x * x, axis=-1, keepdims=True) + eps) * g


def _sigmoid(x):
    return 0.5 * jnp.tanh(0.5 * x) + 0.5


def _silu(x):
    return x * _sigmoid(x)


def _norm_kernel(h_ref, g_ref, o_ref):
    o_ref[...] = _rms(h_ref[...], g_ref[...]).astype(o_ref.dtype)


def _norm(h, g, tm):
    n, d = h.shape
    return pl.pallas_call(
        _norm_kernel,
        grid=(n // tm,),
        in_specs=[pl.BlockSpec((tm, d), lambda i: (i, 0)), pl.BlockSpec((1, d), lambda i: (0, 0))],
        out_specs=pl.BlockSpec((tm, d), lambda i: (i, 0)),
        out_shape=jax.ShapeDtypeStruct((n, d), BF16),
        compiler_params=_cparams(("parallel",)),
        name="rmsnorm",
    )(h, g)


def _inproj_kernel(x_ref, w_ref, b_ref, o_ref):
    acc = jnp.dot(x_ref[...], w_ref[...], preferred_element_type=F32)
    o_ref[...] = (acc + b_ref[...]).astype(o_ref.dtype)


def _inproj(hn, w, b, tm, tn):
    n, d = hn.shape
    zw = w.shape[1]
    return pl.pallas_call(
        _inproj_kernel,
        grid=(n // tm, zw // tn),
        in_specs=[pl.BlockSpec((tm, d), lambda i, j: (i, 0)),
                  pl.BlockSpec((d, tn), lambda i, j: (0, j)),
                  pl.BlockSpec((1, tn), lambda i, j: (0, j))],
        out_specs=pl.BlockSpec((tm, tn), lambda i, j: (i, j)),
        out_shape=jax.ShapeDtypeStruct((n, zw), BF16),
        compiler_params=_cparams(("parallel", "arbitrary")),
        name="inproj",
    )(hn, w, b)


def _forget_kernel(x_ref, w_ref, b_ref, o_ref, carry_ref):
    @pl.when(pl.program_id(1) == 0)
    def _():
        carry_ref[...] = jnp.zeros_like(carry_ref)

    fw = jnp.dot(x_ref[0], w_ref[...], preferred_element_type=F32)
    f = fw[:, :LANES] + fw[:, LANES:] + b_ref[...]
    ls = jnp.minimum(f, 0.0) - jnp.log(1.0 + jnp.exp(-jnp.abs(f)))
    hi = ls.astype(BF16)
    r1 = ls - hi.astype(F32)
    mid = r1.astype(BF16)
    lo = (r1 - mid.astype(F32)).astype(BF16)
    t = ls.shape[0]
    r = lax.broadcasted_iota(jnp.int32, (t, t), 0)
    c = lax.broadcasted_iota(jnp.int32, (t, t), 1)
    tril = jnp.where(c <= r, 1.0, 0.0).astype(BF16)
    parts = jnp.dot(tril, jnp.concatenate([hi, mid, lo], axis=-1), preferred_element_type=F32)
    cs = (parts[:, :LANES] + parts[:, LANES:2 * LANES]) + parts[:, 2 * LANES:] + carry_ref[...]
    o_ref[0] = cs
    carry_ref[...] = cs[t - 1:t, :]


def _forget(hn3, w_f, b_f, ts):
    b, lp, d = hn3.shape
    return pl.pallas_call(
        _forget_kernel,
        grid=(b, lp // ts),
        in_specs=[pl.BlockSpec((1, ts, d), lambda bi, t: (bi, t, 0)),
                  pl.BlockSpec((d, 2 * LANES), lambda bi, t: (0, 0)),
                  pl.BlockSpec((1, LANES), lambda bi, t: (0, 0))],
        out_specs=pl.BlockSpec((1, ts, LANES), lambda bi, t: (bi, t, 0)),
        out_shape=jax.ShapeDtypeStruct((b, lp, LANES), F32),
        scratch_shapes=[pltpu.VMEM((1, LANES), F32)],
        compiler_params=_cparams(("parallel", "arbitrary")),
        name="forget_cumsum",
    )(hn3, w_f, b_f)


def _pair_rms(x, first_head, gain, eps=1e-6):
    sq = x * x
    s0 = jnp.sum(jnp.where(first_head, sq, 0.0), axis=-1, keepdims=True)
    s1 = jnp.sum(jnp.where(first_head, 0.0, sq), axis=-1, keepdims=True)
    ms = jnp.where(first_head, s0, s1) * (1.0 / HEAD_DIM)
    return x * lax.rsqrt(ms + eps) * gain


def _attn_kernel(lo_ref, q_ref, k_ref, v_ref, fc_ref, fr_ref, qg_ref, kg_ref, o_ref, kn_ref, *, tq, tk):
    bi = pl.program_id(0)
    p = pl.program_id(1)
    i = pl.program_id(2)
    nq = pl.num_programs(2)
    lp = k_ref.shape[1]
    lane = lax.broadcasted_iota(jnp.int32, (1, LANES), 1)
    first_head = lane < HEAD_DIM

    @pl.when(i == 0)
    def _():
        def body(c, carry):
            off = pl.multiple_of(c * tk, tk)
            kk = k_ref[0, pl.ds(off, tk), :].astype(F32)
            kn_ref[pl.ds(off, tk), :] = _pair_rms(kk, first_head, kg_ref[...]).astype(BF16)
            return carry
        lax.fori_loop(0, lp // tk, body, 0)

    qn = _pair_rms(q_ref[0].astype(F32), first_head, qg_ref[...])
    fc = fc_ref[0]
    row_pos = i * tq + lax.broadcasted_iota(jnp.int32, (tq, 1), 0)

    heads = [2 * p, 2 * p + 1]
    qhs = [jnp.where(first_head, qn, 0.0).astype(BF16), jnp.where(first_head, 0.0, qn).astype(BF16)]
    fts = [jnp.sum(jnp.where(lane == hd, fc, 0.0), axis=-1, keepdims=True) * LOG2E for hd in heads]

    def update(off, width, carry, hh, masked):
        m, l, acc = carry
        ks = kn_ref[pl.ds(off, width), :]
        s = lax.dot_general(qhs[hh], ks, (((1,), (1,)), ((), ())), preferred_element_type=F32)
        fs = fr_ref[0, pl.ds(heads[hh], 1), pl.ds(off, width)] * LOG2E
        s = s + (fts[hh] - fs)
        if masked:
            col_pos = off + lax.broadcasted_iota(jnp.int32, (1, width), 1)
            s = jnp.where(col_pos <= row_pos, s, NEG_INF)
        m_new = jnp.maximum(m, jnp.max(s, axis=-1, keepdims=True))
        alpha = jnp.exp2(m - m_new)
        pm = jnp.exp2(s - m_new)
        l = alpha * l + jnp.sum(pm, axis=-1, keepdims=True)
        vs = v_ref[0, pl.ds(off, width), :]
        acc = alpha * acc + jnp.dot(pm.astype(BF16), vs, preferred_element_type=F32)
        return m_new, l, acc

    win = jnp.maximum(i * (tq // tk) - 1, 0)
    firsts = [lo_ref[(bi * ATT_HEADS + hd) * nq + i] for hd in heads]
    init = (jnp.full((tq, 1), NEG_INF, F32), jnp.zeros((tq, 1), F32), jnp.zeros((tq, LANES), F32))

    def far(j, carry):
        off = pl.multiple_of(j * tk, tk)
        return tuple(update(off, tk, carry[hh], hh, masked=False) for hh in range(2))

    carry = lax.fori_loop(jnp.minimum(firsts[0], firsts[1]), win, far, (init, init))
    off = pl.multiple_of(win * tk, tk)
    outs = []
    for hh in range(2):
        m, l, acc = update(off, tq + tk, carry[hh], hh, masked=True)
        outs.append(acc / l)
    o_ref[0] = jnp.where(first_head, outs[0], outs[1]).astype(o_ref.dtype)


def _first_live_chunk(frow, qg, kg, tq, tk):
    b, nh, lp = frow.shape
    qk_bound = 1.02 * HEAD_DIM * jnp.max(jnp.abs(qg)) * jnp.max(jnp.abs(kg))
    f_first = frow[:, :, 0::tq]
    f_last = frow[:, :, tk - 1::tk]
    dead = (f_first[:, :, :, None] - f_last[:, :, None, :] + 2.0 * qk_bound) < -EXP_UNDERFLOW
    n_dead = jnp.sum(dead.astype(jnp.int32), axis=-1)
    n_full = (jnp.arange(lp // tq, dtype=jnp.int32) * tq) // tk
    return jnp.minimum(n_dead, n_full[None, None, :]).reshape(-1)


def _attention(z3, fcol, frow, qg, kg, tq, tk):
    b, lp, _ = z3.shape
    npairs = ATT_HEADS // 2
    qb, kb, vb = Z_Q // LANES, Z_K // LANES, Z_V // LANES
    assert tq % tk == 0 and lp >= tq + tk, (tq, tk, lp)
    first = _first_live_chunk(frow, qg, kg, tq, tk)
    grid_spec = pltpu.PrefetchScalarGridSpec(
        num_scalar_prefetch=1,
        grid=(b, npairs, lp // tq),
        in_specs=[pl.BlockSpec((1, tq, LANES), lambda bi, p, i, lo: (bi, i, qb + p)),
                  pl.BlockSpec((1, lp, LANES), lambda bi, p, i, lo: (bi, 0, kb + p)),
                  pl.BlockSpec((1, lp, LANES), lambda bi, p, i, lo: (bi, 0, vb + p)),
                  pl.BlockSpec((1, tq, LANES), lambda bi, p, i, lo: (bi, i, 0)),
                  pl.BlockSpec((1, ATT_HEADS, lp), lambda bi, p, i, lo: (bi, 0, 0)),
                  pl.BlockSpec((1, LANES), lambda bi, p, i, lo: (0, 0)),
                  pl.BlockSpec((1, LANES), lambda bi, p, i, lo: (0, 0))],
        out_specs=pl.BlockSpec((1, tq, LANES), lambda bi, p, i, lo: (bi, i, p)),
        scratch_shapes=[pltpu.VMEM((lp, LANES), BF16)],
    )
    return pl.pallas_call(
        functools.partial(_attn_kernel, tq=tq, tk=tk),
        grid_spec=grid_spec,
        out_shape=jax.ShapeDtypeStruct((b, lp, ATT_W), BF16),
        compiler_params=_cparams(("parallel", "parallel", "arbitrary")),
        name="fox_attention",
    )(first, z3, z3, z3, fcol, frow, qg * LOG2E, kg)


def _convpool_kernel(a_ref, g_ref, p_ref, ah_ref, gh_ref, ph_ref, cw_ref, cb_ref, lg_ref, lb_ref,
                     pw_ref, ps_ref, u_ref, pm_ref, ext_ref, pext_ref, *, ts):
    i = pl.program_id(1)
    has_prev = i > 0

    def stage(ref, halo_rows, rows):
        ref[0, 0:HALO, :] = halo_rows
        ref[0, HALO:HALO + ts, :] = rows
        base = ref[0]
        for r in range(1, SUBLANES):
            ref[r, r:HALO + ts, :] = base[0:HALO + ts - r, :]

    def behind(ref, back):
        start = HALO - back // SUBLANES * SUBLANES
        return ref[back % SUBLANES, start:start + ts, :]

    u = a_ref[0].astype(F32) * _sigmoid(g_ref[0].astype(F32))
    uh = ah_ref[0].astype(F32) * _sigmoid(gh_ref[0].astype(F32))
    stage(ext_ref, jnp.where(has_prev, uh, 0.0), u)
    acc = jnp.zeros((ts, CONV_W), F32) + cb_ref[...]
    for j in range(CONV_K):
        acc = acc + cw_ref[j:j + 1, :] * behind(ext_ref, CONV_K - 1 - j)
    mu = jnp.mean(acc, axis=-1, keepdims=True)
    cen = acc - mu
    var = jnp.mean(cen * cen, axis=-1, keepdims=True)
    y = cen * lax.rsqrt(var + 1e-5) * lg_ref[...] + lb_ref[...]
    u_ref[0] = _silu(y).astype(u_ref.dtype)

    x = p_ref[0].astype(F32)
    stage(pext_ref, jnp.where(has_prev, ph_ref[0].astype(F32), 0.0), x)
    pos1 = (i * ts + 1 + lax.broadcasted_iota(jnp.int32, (ts, 1), 0)).astype(F32)
    lane = lax.broadcasted_iota(jnp.int32, (1, POOL_W), 1)
    group_w = POOL_W // len(POOL_WINDOWS)
    run = x
    pooled = jnp.zeros((ts, POOL_W), F32)
    for k in range(1, max(POOL_WINDOWS)):
        run = run + behind(pext_ref, k)
        if (k + 1) in POOL_WINDOWS:
            gi = POOL_WINDOWS.index(k + 1)
            mean = run / jnp.minimum(pos1, float(k + 1))
            in_group = (lane >= gi * group_w) & (lane < (gi + 1) * group_w)
            pooled = jnp.where(in_group, mean, pooled)
    pm = (pooled - x).astype(BF16)
    lin = jnp.dot(pm, pw_ref[...], preferred_element_type=F32) * ps_ref[...]
    pm_ref[0] = lin.astype(pm_ref.dtype)


def _convpool(z3, cw, cb, lg, lb, pw, ps, ts):
    b, lp, _ = z3.shape
    ab, gb, pb = Z_A // CONV_W, Z_G // CONV_W, Z_P // POOL_W
    hpt = ts // HALO
    main = lambda blk: pl.BlockSpec((1, ts, CONV_W), lambda bi, i: (bi, i, blk))
    halo = lambda blk: pl.BlockSpec((1, HALO, CONV_W), lambda bi, i: (bi, jnp.maximum(i * hpt - 1, 0), blk))
    const = lambda shape: pl.BlockSpec(shape, lambda bi, i: (0, 0))
    out = pl.BlockSpec((1, ts, CONV_W), lambda bi, i: (bi, i, 0))
    return pl.pallas_call(
        functools.partial(_convpool_kernel, ts=ts),
        grid=(b, lp // ts),
        in_specs=[main(ab), main(gb), main(pb), halo(ab), halo(gb), halo(pb),
                  const((HALO, CONV_W)), const((1, CONV_W)), const((1, CONV_W)), const((1, CONV_W)),
                  const((POOL_W, POOL_W)), const((1, POOL_W))],
        out_specs=[out, out],
        out_shape=[jax.ShapeDtypeStruct((b, lp, CONV_W), BF16), jax.ShapeDtypeStruct((b, lp, POOL_W), BF16)],
        scratch_shapes=[pltpu.VMEM((SUBLANES, HALO + ts, CONV_W), F32),
                        pltpu.VMEM((SUBLANES, HALO + ts, POOL_W), F32)],
        compiler_params=_cparams(("parallel", "arbitrary")),
        name="conv_pool",
    )(z3, z3, z3, z3, z3, z3, cw, cb, lg, lb, pw, ps)


def _merge_kernel(att_ref, u_ref, pm_ref, g0_ref, g1_ref, g2_ref, h_ref, wa_ref, wc_ref, wp_ref, wo_ref,
                  ng_ref, h_out_ref, hn_out_ref):
    ya = jnp.dot(att_ref[...], wa_ref[...], preferred_element_type=F32)
    yc = jnp.dot(u_ref[...], wc_ref[...], preferred_element_type=F32)
    yp = jnp.dot(pm_ref[...], wp_ref[...], preferred_element_type=F32)
    m = (_sigmoid(g0_ref[...].astype(F32)) * ya + _sigmoid(g1_ref[...].astype(F32)) * yc
         + _sigmoid(g2_ref[...].astype(F32)) * yp)
    h = h_ref[...] + jnp.dot(m.astype(BF16), wo_ref[...], preferred_element_type=F32)
    h_out_ref[...] = h
    hn_out_ref[...] = _rms(h, ng_ref[...]).astype(hn_out_ref.dtype)


def _merge(att, u, pm, z, h, wa, wc, wp, wo, ng, tm):
    n, d = h.shape
    row = lambda w, blk=0: pl.BlockSpec((tm, w), lambda i: (i, blk))
    const = lambda shape: pl.BlockSpec(shape, lambda i: (0, 0))
    return pl.pallas_call(
        _merge_kernel,
        grid=(n // tm,),
        in_specs=[row(ATT_W), row(CONV_W), row(POOL_W), row(d, 0), row(d, 1), row(d, 2), row(d),
                  const(wa.shape), const(wc.shape), const(wp.shape), const(wo.shape), const((1, d))],
        out_specs=[row(d), row(d)],
        out_shape=[jax.ShapeDtypeStruct((n, d), F32), jax.ShapeDtypeStruct((n, d), BF16)],
        compiler_params=_cparams(("parallel",)),
        name="merge_outproj",
    )(att, u, pm, z, z, z, h, wa, wc, wp, wo, ng)


def _ffn_kernel(x_ref, h_ref, wg_ref, wu_ref, wd_ref, ng_ref, h_out_ref, hn_out_ref, *, tf):
    x = x_ref[...]
    h = h_ref[...]
    nc = wg_ref.shape[1] // tf

    def gate_up(c):
        cols = slice(c * tf, (c + 1) * tf)
        return (jnp.dot(x, wg_ref[:, cols], preferred_element_type=F32),
                jnp.dot(x, wu_ref[:, cols], preferred_element_type=F32))

    gt, up = gate_up(0)
    for c in range(nc):
        act = (_silu(gt) * up).astype(BF16)
        if c + 1 < nc:
            gt, up = gate_up(c + 1)
        h = h + jnp.dot(act, wd_ref[c * tf:(c + 1) * tf, :], preferred_element_type=F32)
    h_out_ref[...] = h
    hn_out_ref[...] = _rms(h, ng_ref[...]).astype(hn_out_ref.dtype)


def _ffn(hn, h, wg, wu, wd, ng, tm, tf):
    n, d = h.shape
    row = pl.BlockSpec((tm, d), lambda i: (i, 0))
    const = lambda shape: pl.BlockSpec(shape, lambda i: (0, 0), pipeline_mode=pl.Buffered(1))
    return pl.pallas_call(
        functools.partial(_ffn_kernel, tf=tf),
        grid=(n // tm,),
        in_specs=[row, row, const(wg.shape), const(wu.shape), const(wd.shape), const((1, d))],
        out_specs=[row, row],
        out_shape=[jax.ShapeDtypeStruct((n, d), F32), jax.ShapeDtypeStruct((n, d), BF16)],
        compiler_params=_cparams(("parallel",)),
        name="dense_swiglu",
    )(hn, h, wg, wu, wd, ng)


def _router_kernel(h_ref, ng_ref, wr_ref, br_ref, comb_ref, pos_ref, cnt_ref, *, sb):
    tm = h_ref.shape[0]
    hn = _rms(h_ref[...], ng_ref[...])
    logits = jnp.dot(hn, wr_ref[...], precision=HIGHEST, preferred_element_type=F32) + br_ref[...]
    lane = lax.broadcasted_iota(jnp.int32, (1, LANES), 1).astype(F32)
    lg = jnp.where(lane < N_EXPERTS, logits, -jnp.inf)
    m1 = jnp.max(lg, axis=-1, keepdims=True)
    i1 = jnp.min(jnp.where(lg == m1, lane, float(LANES)), axis=-1, keepdims=True)
    sel1 = lane == i1
    lg2 = jnp.where(sel1, -jnp.inf, lg)
    m2 = jnp.max(lg2, axis=-1, keepdims=True)
    i2 = jnp.min(jnp.where(lg2 == m2, lane, float(LANES)), axis=-1, keepdims=True)
    sel2 = lane == i2
    e = jnp.exp(m2 - m1)
    g1 = 1.0 / (1.0 + e)
    comb_ref[...] = jnp.where(sel1, g1, 0.0) + jnp.where(sel2, e * g1, 0.0)
    sel = jnp.where(sel1 | sel2, 1.0, 0.0)
    r = lax.broadcasted_iota(jnp.int32, (sb, sb), 0)
    c = lax.broadcasted_iota(jnp.int32, (sb, sb), 1)
    tril = jnp.where(c <= r, 1.0, 0.0).astype(BF16)
    carry = jnp.zeros((1, LANES), F32)
    for s in range(tm // sb):
        blk = slice(s * sb, (s + 1) * sb)
        incl = jnp.dot(tril, sel[blk].astype(BF16), preferred_element_type=F32) + carry
        pos_ref[blk, :] = jnp.where(sel[blk] > 0.0, incl - 1.0, -1.0)
        carry = incl[sb - 1:sb, :]
    cnt_ref[0] = jnp.broadcast_to(carry, (8, LANES)).astype(jnp.int32)


def _router(h, ng, wr, br, tm, sb):
    n, d = h.shape
    nt = n // tm
    return pl.pallas_call(
        functools.partial(_router_kernel, sb=sb),
        grid=(nt,),
        in_specs=[pl.BlockSpec((tm, d), lambda i: (i, 0)),
                  pl.BlockSpec((1, d), lambda i: (0, 0)),
                  pl.BlockSpec((d, LANES), lambda i: (0, 0)),
                  pl.BlockSpec((1, LANES), lambda i: (0, 0))],
        out_specs=[pl.BlockSpec((tm, LANES), lambda i: (i, 0)),
                   pl.BlockSpec((tm, LANES), lambda i: (i, 0)),
                   pl.BlockSpec((1, 8, LANES), lambda i: (i, 0, 0))],
        out_shape=[jax.ShapeDtypeStruct((n, LANES), F32), jax.ShapeDtypeStruct((n, LANES), F32),
                   jax.ShapeDtypeStruct((nt, 8, LANES), jnp.int32)],
        compiler_params=_cparams(("parallel",)),
        name="router_top2",
    )(h, ng, wr, br)


def _moe_kernel(cnt_ref, x_ref, post_ref, pos_ref, comb_ref, wg_ref, wu_ref, wd_ref,
                out_ref, xs_ref, ys_ref, *, sizes, sub):
    i = pl.program_id(0)
    e = pl.program_id(1)
    f = pl.program_id(2)
    nf = pl.num_programs(2)
    cnt = cnt_ref[i * N_EXPERTS + e]
    big = sizes[-1]

    def for_each_chunk(body):
        lo = 0
        for r in sizes:
            @pl.when((cnt > lo) & (cnt <= r))
            def _(r=r):
                body(0, r)
            lo = r

        @pl.when(cnt > big)
        def _():
            def step(c, carry):
                body(pl.multiple_of(c * big, big), big)
                return carry
            lax.fori_loop(0, (cnt + big - 1) // big, step, 0)

    @pl.when((e == 0) & (f == 0))
    def _():
        out_ref[...] = jnp.zeros_like(out_ref)

    @pl.when(f == 0)
    def _():
        prow = post_ref[0, pl.ds(e, 1), :]

        def gather(off, r):
            tgt = (off + lax.broadcasted_iota(jnp.int32, (r, 1), 0)).astype(F32)
            onehot = jnp.where(prow == tgt, 1.0, 0.0).astype(BF16)
            xs_ref[pl.ds(off, r), :] = jnp.dot(onehot, x_ref[...], preferred_element_type=F32).astype(BF16)
            ys_ref[pl.ds(off, r), :] = jnp.zeros((r, D_MODEL), F32)
        for_each_chunk(gather)

    def expert(off, r):
        xc = xs_ref[pl.ds(off, r), :]
        y = ys_ref[pl.ds(off, r), :]
        nsub = wg_ref.shape[2] // sub

        def gate_up(s):
            cols = slice(s * sub, (s + 1) * sub)
            return (jnp.dot(xc, wg_ref[0, :, cols], preferred_element_type=F32),
                    jnp.dot(xc, wu_ref[0, :, cols], preferred_element_type=F32))

        gt, up = gate_up(0)
        for s in range(nsub):
            act = (_silu(gt) * up).astype(BF16)
            if s + 1 < nsub:
                gt, up = gate_up(s + 1)
            y = y + jnp.dot(act, wd_ref[0, s * sub:(s + 1) * sub, :], preferred_element_type=F32)
        ys_ref[pl.ds(off, r), :] = y
    for_each_chunk(expert)

    @pl.when(f == nf - 1)
    def _():
        lane = lax.broadcasted_iota(jnp.int32, (1, LANES), 1)
        pcol = jnp.sum(jnp.where(lane == e, pos_ref[...], 0.0), axis=-1, keepdims=True)
        gcol = jnp.sum(jnp.where(lane == e, comb_ref[...], 0.0), axis=-1, keepdims=True)

        def scatter(off, r):
            tgt = (off + lax.broadcasted_iota(jnp.int32, (1, r), 1)).astype(F32)
            onehot_t = jnp.where(pcol == tgt, 1.0, 0.0).astype(BF16)
            y = ys_ref[pl.ds(off, r), :].astype(BF16)
            out_ref[...] += gcol * jnp.dot(onehot_t, y, preferred_element_type=F32)
        for_each_chunk(scatter)


def _moe(cnt, hn, post, pos, comb, wg, wu, wd, tm, tf, sizes, sub):
    n, d = hn.shape
    nf = wg.shape[2] // tf
    cap = -(-tm // sizes[-1]) * sizes[-1]
    once = pl.Buffered(1)
    row = lambda w: pl.BlockSpec((tm, w), lambda i, e, f, cnt: (i, 0), pipeline_mode=once)
    grid_spec = pltpu.PrefetchScalarGridSpec(
        num_scalar_prefetch=1,
        grid=(n // tm, N_EXPERTS, nf),
        in_specs=[row(d),
                  pl.BlockSpec((1, POST_ROWS, tm), lambda i, e, f, cnt: (i, 0, 0), pipeline_mode=once),
                  row(LANES), row(LANES),
                  pl.BlockSpec((1, d, tf), lambda i, e, f, cnt: (e, 0, f)),
                  pl.BlockSpec((1, d, tf), lambda i, e, f, cnt: (e, 0, f)),
                  pl.BlockSpec((1, tf, d), lambda i, e, f, cnt: (e, f, 0))],
        out_specs=pl.BlockSpec((tm, d), lambda i, e, f, cnt: (i, 0)),
        scratch_shapes=[pltpu.VMEM((cap, d), BF16), pltpu.VMEM((cap, d), F32)],
    )
    return pl.pallas_call(
        functools.partial(_moe_kernel, sizes=sizes, sub=sub),
        grid_spec=grid_spec,
        out_shape=jax.ShapeDtypeStruct((n, d), F32),
        compiler_params=pltpu.CompilerParams(dimension_semantics=("parallel", "arbitrary", "arbitrary"),
                                             vmem_limit_bytes=MOE_VMEM_LIMIT),
        name="expert_swiglu",
    )(cnt, hn, post, pos, comb, wg, wu, wd)


def _add_norm_kernel(h_ref, d_ref, g_ref, h_out_ref, hn_out_ref):
    h = h_ref[...] + d_ref[...]
    h_out_ref[...] = h
    hn_out_ref[...] = _rms(h, g_ref[...]).astype(hn_out_ref.dtype)


def _add_norm(h, delta, g, tm):
    n, d = h.shape
    row = pl.BlockSpec((tm, d), lambda i: (i, 0))
    return pl.pallas_call(
        _add_norm_kernel,
        grid=(n // tm,),
        in_specs=[row, row, pl.BlockSpec((1, d), lambda i: (0, 0))],
        out_specs=[row, row],
        out_shape=[jax.ShapeDtypeStruct((n, d), F32), jax.ShapeDtypeStruct((n, d), BF16)],
        compiler_params=_cparams(("parallel",)),
        name="residual_norm",
    )(h, delta, g)


def _pad_lanes(a, width=LANES):
    return jnp.pad(a, ((0, 0), (0, width - a.shape[-1])))


def kernel(x, meta, norm_mix, w_in, b_in, q_norm, k_norm, w_attn_o, conv_w, conv_b, conv_ln_g, conv_ln_b,
           w_conv_o, pool_w, pool_scale, w_pool_o, w_out, norm_ffn, w_ff_gate, w_ff_up, w_ff_down, w_router,
           b_router, w_e_gate, w_e_up, w_e_down):
    bsz, seq, d = x.shape
    depth = w_in.shape[0]
    length = seq + N_META
    lp = -(-length // SEQ_ALIGN) * SEQ_ALIGN
    n = bsz * lp

    tm = _pick(n, (1536, 768, 512, 256))
    tm_mid = _pick(n, (768, 512, 256))
    tm_ffn = _pick(n, (768, 512, 256))
    tm_moe = _pick(n, (1408, 768, 512, 256))
    sb_router = _pick(tm_moe, (704, 768, 512, 256))
    tn = _pick(Z_W, (1792, 768, 256))
    ts = _pick(lp, (768, 512, 256))
    ts_f = ts
    tk = 256
    tq = _pick(lp, (768, 256)) if lp >= 1024 else 256
    tf_dense = _pick(w_ff_gate.shape[2], (256,))
    tf_moe = _pick(w_e_gate.shape[3], (1792, 512, 256))
    sub_moe = _pick(tf_moe, (256,))
    quarter = tm_moe // 4
    moe_sizes = tuple(range(max(quarter - 96, 64) // 64 * 64, quarter + 161, 64))

    h = jnp.concatenate([jnp.broadcast_to(meta[None].astype(x.dtype), (bsz, N_META, d)), x], axis=1)
    h = jnp.pad(h, ((0, 0), (0, lp - length), (0, 0))).reshape(n, d)
    hn = _norm(h, norm_mix[0][None], tm)

    att_scale = HEAD_DIM ** -0.5
    gate_lo = 3 * ATT_W + ATT_HEADS + 2 * CONV_W + POOL_W
    f_lo = 3 * ATT_W
    for l in range(depth):
        wl = w_in[l]
        w_main = jnp.concatenate([wl[:, gate_lo:], wl[:, :f_lo], wl[:, f_lo + ATT_HEADS:gate_lo]], axis=1).astype(BF16)
        bl = b_in[l]
        b_main = jnp.concatenate([bl[gate_lo:], bl[:f_lo], bl[f_lo + ATT_HEADS:gate_lo]])[None]
        w_f = _pad_lanes(wl[:, f_lo:f_lo + ATT_HEADS])
        w_f_hi = w_f.astype(BF16)
        w_f = jnp.concatenate([w_f_hi, (w_f - w_f_hi.astype(F32)).astype(BF16)], axis=1)
        b_f = _pad_lanes(bl[None, f_lo:f_lo + ATT_HEADS])

        z = _inproj(hn, w_main, b_main, tm, tn)
        z3 = z.reshape(bsz, lp, Z_W)
        fcol = _forget(hn.reshape(bsz, lp, d), w_f, b_f, ts_f)
        frow = jnp.transpose(fcol[:, :, :ATT_HEADS], (0, 2, 1))
        qg = jnp.tile(q_norm[l], 2)[None] * att_scale
        kg = jnp.tile(k_norm[l], 2)[None]
        att = _attention(z3, fcol, frow, qg, kg, tq, tk)

        cw = jnp.pad(conv_w[l], ((0, HALO - CONV_K), (0, 0)))
        pw = jax.scipy.linalg.block_diag(*[pool_w[l, g] for g in range(pool_w.shape[1])]).astype(BF16)
        u, pm = _convpool(z3, cw, conv_b[l][None], conv_ln_g[l][None], conv_ln_b[l][None], pw,
                          pool_scale[l][None], ts)

        h, hn = _merge(att.reshape(n, ATT_W), u.reshape(n, CONV_W), pm.reshape(n, POOL_W), z, h,
                       w_attn_o[l].astype(BF16), w_conv_o[l].astype(BF16), w_pool_o[l].astype(BF16),
                       w_out[l].astype(BF16), norm_ffn[l][None], tm_mid)

        ng_next = norm_mix[min(l + 1, depth - 1)][None]
        i = l // 2
        if l % 2 == 0:
            h, hn = _ffn(hn, h, w_ff_gate[i].astype(BF16), w_ff_up[i].astype(BF16), w_ff_down[i].astype(BF16),
                         ng_next, tm_ffn, tf_dense)
        else:
            comb, pos, cnt = _router(h, norm_ffn[l][None], _pad_lanes(w_router[i]), _pad_lanes(b_router[i][None]),
                                     tm_moe, sb_router)
            post = jnp.transpose(pos[:, :POST_ROWS].reshape(n // tm_moe, tm_moe, POST_ROWS), (0, 2, 1))
            cnt_flat = cnt[:, 0, :N_EXPERTS].reshape(-1)
            delta = _moe(cnt_flat, hn, post, pos, comb, w_e_gate[i].astype(BF16), w_e_up[i].astype(BF16),
                         w_e_down[i].astype(BF16), tm_moe, tf_moe, moe_sizes, sub_moe)
            h, hn = _add_norm(h, delta, ng_next, tm)

    return h.reshape(bsz, lp, d)[:, N_META:length]
```

```python
import functools

import jax
import jax.numpy as jnp
from jax import lax
from jax.experimental import pallas as pl
from jax.experimental.pallas import tpu as pltpu

F32 = jnp.float32
BF16 = jnp.bfloat16
HIGHEST = lax.Precision.HIGHEST

D_MODEL = 1024
N_META = 16
HEAD_DIM = 64
ATT_W = 512
ATT_HEADS = 8
CONV_W = 256
CONV_K = 31
POOL_W = 256
POOL_WINDOWS = (2, 4, 8, 16)
N_EXPERTS = 8
LANES = 128
SUBLANES = 8
HALO = 32
NEG_INF = -1e30
LOG2E = 1.4426950408889634
EXP_UNDERFLOW = 110.0
SEQ_ALIGN = 256
VMEM_LIMIT = 56 * 1024 * 1024
MOE_VMEM_LIMIT = 60 * 1024 * 1024
POST_ROWS = 32

Z_GATE = 0
Z_Q = 3072
Z_K = 3584
Z_V = 4096
Z_A = 4608
Z_G = 4864
Z_P = 5120
Z_W = 5376


def _pick(n, candidates):
    for c in candidates:
        if n % c == 0:
            return c
    raise ValueError(f"no tile for {n} in {candidates}")


def _cparams(sem):
    return pltpu.CompilerParams(dimension_semantics=sem, vmem_limit_bytes=VMEM_LIMIT)


def _rms(x, g, eps=1e-6):
    return x * lax.rsqrt(jnp.mean(x * x, axis=-1, keepdims=True) + eps) * g


def _sigmoid(x):
    return 0.5 * jnp.tanh(0.5 * x) + 0.5


def _silu(x):
    return x * _sigmoid(x)


def _cast_kernel(w_ref, o_ref):
    o_ref[...] = w_ref[0].astype(o_ref.dtype)


def _to_bf16(w, layer):
    _, s, r, c = w.shape
    rows = _pick(r, (512, 256, 128, 8))
    return pl.pallas_call(
        _cast_kernel,
        grid=(s, r // rows),
        in_specs=[pl.BlockSpec((1, 1, rows, c), lambda i, j: (layer, i, j, 0))],
        out_specs=pl.BlockSpec((1, rows, c), lambda i, j: (i, j, 0)),
        out_shape=jax.ShapeDtypeStruct((s, r, c), BF16),
        compiler_params=_cparams(("parallel", "parallel")),
        name="weights_to_bf16",
    )(w)


def _norm_kernel(h_ref, g_ref, o_ref):
    o_ref[...] = _rms(h_ref[...], g_ref[...]).astype(o_ref.dtype)


def _norm(h, g, tm):
    n, d = h.shape
    return pl.pallas_call(
        _norm_kernel,
        grid=(n // tm,),
        in_specs=[pl.BlockSpec((tm, d), lambda i: (i, 0)), pl.BlockSpec((1, d), lambda i: (0, 0))],
        out_specs=pl.BlockSpec((tm, d), lambda i: (i, 0)),
        out_shape=jax.ShapeDtypeStruct((n, d), BF16),
        compiler_params=_cparams(("parallel",)),
        name="rmsnorm",
    )(h, g)


def _inproj_kernel(x_ref, w_ref, b_ref, o_ref):
    acc = jnp.dot(x_ref[...], w_ref[...], preferred_element_type=F32)
    o_ref[...] = (acc + b_ref[...]).astype(o_ref.dtype)


def _inproj(hn, w, b, tm, tn):
    n, d = hn.shape
    zw = w.shape[1]
    return pl.pallas_call(
        _inproj_kernel,
        grid=(n // tm, zw // tn),
        in_specs=[pl.BlockSpec((tm, d), lambda i, j: (i, 0)),
                  pl.BlockSpec((d, tn), lambda i, j: (0, j)),
                  pl.BlockSpec((1, tn), lambda i, j: (0, j))],
        out_specs=pl.BlockSpec((tm, tn), lambda i, j: (i, j)),
        out_shape=jax.ShapeDtypeStruct((n, zw), BF16),
        compiler_params=_cparams(("parallel", "arbitrary")),
        name="inproj",
    )(hn, w, b)


def _forget_kernel(x_ref, w_ref, b_ref, o_ref, carry_ref):
    @pl.when(pl.program_id(1) == 0)
    def _():
        carry_ref[...] = jnp.zeros_like(carry_ref)

    fw = jnp.dot(x_ref[0], w_ref[...], preferred_element_type=F32)
    f = fw[:, :LANES] + fw[:, LANES:] + b_ref[...]
    ls = jnp.minimum(f, 0.0) - jnp.log(1.0 + jnp.exp(-jnp.abs(f)))
    hi = ls.astype(BF16)
    r1 = ls - hi.astype(F32)
    mid = r1.astype(BF16)
    lo = (r1 - mid.astype(F32)).astype(BF16)
    t = ls.shape[0]
    r = lax.broadcasted_iota(jnp.int32, (t, t), 0)
    c = lax.broadcasted_iota(jnp.int32, (t, t), 1)
    tril = jnp.where(c <= r, 1.0, 0.0).astype(BF16)
    parts = jnp.dot(tril, jnp.concatenate([hi, mid, lo], axis=-1), preferred_element_type=F32)
    cs = (parts[:, :LANES] + parts[:, LANES:2 * LANES]) + parts[:, 2 * LANES:] + carry_ref[...]
    o_ref[0] = cs
    carry_ref[...] = cs[t - 1:t, :]


def _forget(hn3, w_f, b_f, ts):
    b, lp, d = hn3.shape
    return pl.pallas_call(
        _forget_kernel,
        grid=(b, lp // ts),
        in_specs=[pl.BlockSpec((1, ts, d), lambda bi, t: (bi, t, 0)),
                  pl.BlockSpec((d, 2 * LANES), lambda bi, t: (0, 0)),
                  pl.BlockSpec((1, LANES), lambda bi, t: (0, 0))],
        out_specs=pl.BlockSpec((1, ts, LANES), lambda bi, t: (bi, t, 0)),
        out_shape=jax.ShapeDtypeStruct((b, lp, LANES), F32),
        scratch_shapes=[pltpu.VMEM((1, LANES), F32)],
        compiler_params=_cparams(("parallel", "arbitrary")),
        name="forget_cumsum",
    )(hn3, w_f, b_f)


def _pair_rms(x, first_head, gain, eps=1e-6):
    sq = x * x
    s0 = jnp.sum(jnp.where(first_head, sq, 0.0), axis=-1, keepdims=True)
    s1 = jnp.sum(jnp.where(first_head, 0.0, sq), axis=-1, keepdims=True)
    ms = jnp.where(first_head, s0, s1) * (1.0 / HEAD_DIM)
    return x * lax.rsqrt(ms + eps) * gain


def _attn_kernel(lo_ref, q_ref, k_ref, v_ref, fc_ref, fr_ref, qg_ref, kg_ref, o_ref, kn_ref, *, tq, tk):
    bi = pl.program_id(0)
    p = pl.program_id(1)
    i = pl.program_id(2)
    nq = pl.num_programs(2)
    lp = k_ref.shape[1]
    lane = lax.broadcasted_iota(jnp.int32, (1, LANES), 1)
    first_head = lane < HEAD_DIM

    @pl.when(i == 0)
    def _():
        def body(c, carry):
            off = pl.multiple_of(c * tk, tk)
            kk = k_ref[0, pl.ds(off, tk), :].astype(F32)
            kn_ref[pl.ds(off, tk), :] = _pair_rms(kk, first_head, kg_ref[...]).astype(BF16)
            return carry
        lax.fori_loop(0, lp // tk, body, 0)

    qn = _pair_rms(q_ref[0].astype(F32), first_head, qg_ref[...])
    fc = fc_ref[0]
    row_pos = i * tq + lax.broadcasted_iota(jnp.int32, (tq, 1), 0)

    heads = [2 * p, 2 * p + 1]
    qhs = [jnp.where(first_head, qn, 0.0).astype(BF16), jnp.where(first_head, 0.0, qn).astype(BF16)]
    fts = [jnp.sum(jnp.where(lane == hd, fc, 0.0), axis=-1, keepdims=True) * LOG2E for hd in heads]

    def update(off, width, carry, hh, masked):
        m, l, acc = carry
        ks = kn_ref[pl.ds(off, width), :]
        s = lax.dot_general(qhs[hh], ks, (((1,), (1,)), ((), ())), preferred_element_type=F32)
        fs = fr_ref[0, pl.ds(heads[hh], 1), pl.ds(off, width)] * LOG2E
        s = s + (fts[hh] - fs)
        if masked:
            col_pos = off + lax.broadcasted_iota(jnp.int32, (1, width), 1)
            s = jnp.where(col_pos <= row_pos, s, NEG_INF)
        m_new = jnp.maximum(m, jnp.max(s, axis=-1, keepdims=True))
        alpha = jnp.exp2(m - m_new)
        pm = jnp.exp2(s - m_new)
        l = alpha * l + jnp.sum(pm, axis=-1, keepdims=True)
        vs = v_ref[0, pl.ds(off, width), :]
        acc = alpha * acc + jnp.dot(pm.astype(BF16), vs, preferred_element_type=F32)
        return m_new, l, acc

    win = jnp.maximum(i * (tq // tk) - 1, 0)
    firsts = [lo_ref[(bi * ATT_HEADS + hd) * nq + i] for hd in heads]
    init = (jnp.full((tq, 1), NEG_INF, F32), jnp.zeros((tq, 1), F32), jnp.zeros((tq, LANES), F32))

    def far(j, carry):
        off = pl.multiple_of(j * tk, tk)
        return tuple(update(off, tk, carry[hh], hh, masked=False) for hh in range(2))

    carry = lax.fori_loop(jnp.minimum(firsts[0], firsts[1]), win, far, (init, init))
    off = pl.multiple_of(win * tk, tk)
    outs = []
    for hh in range(2):
        m, l, acc = update(off, tq + tk, carry[hh], hh, masked=True)
        outs.append(acc / l)
    o_ref[0] = jnp.where(first_head, outs[0], outs[1]).astype(o_ref.dtype)


def _first_live_chunk(frow, qg, kg, tq, tk):
    b, nh, lp = frow.shape
    qk_bound = 1.02 * HEAD_DIM * jnp.max(jnp.abs(qg)) * jnp.max(jnp.abs(kg))
    f_first = frow[:, :, 0::tq]
    f_last = frow[:, :, tk - 1::tk]
    dead = (f_first[:, :, :, None] - f_last[:, :, None, :] + 2.0 * qk_bound) < -EXP_UNDERFLOW
    n_dead = jnp.sum(dead.astype(jnp.int32), axis=-1)
    n_full = (jnp.arange(lp // tq, dtype=jnp.int32) * tq) // tk
    return jnp.minimum(n_dead, n_full[None, None, :]).reshape(-1)


def _attention(z3, fcol, frow, qg, kg, tq, tk):
    b, lp, _ = z3.shape
    npairs = ATT_HEADS // 2
    qb, kb, vb = Z_Q // LANES, Z_K // LANES, Z_V // LANES
    assert tq % tk == 0 and lp >= tq + tk, (tq, tk, lp)
    first = _first_live_chunk(frow, qg, kg, tq, tk)
    grid_spec = pltpu.PrefetchScalarGridSpec(
        num_scalar_prefetch=1,
        grid=(b, npairs, lp // tq),
        in_specs=[pl.BlockSpec((1, tq, LANES), lambda bi, p, i, lo: (bi, i, qb + p)),
                  pl.BlockSpec((1, lp, LANES), lambda bi, p, i, lo: (bi, 0, kb + p)),
                  pl.BlockSpec((1, lp, LANES), lambda bi, p, i, lo: (bi, 0, vb + p)),
                  pl.BlockSpec((1, tq, LANES), lambda bi, p, i, lo: (bi, i, 0)),
                  pl.BlockSpec((1, ATT_HEADS, lp), lambda bi, p, i, lo: (bi, 0, 0)),
                  pl.BlockSpec((1, LANES), lambda bi, p, i, lo: (0, 0)),
                  pl.BlockSpec((1, LANES), lambda bi, p, i, lo: (0, 0))],
        out_specs=pl.BlockSpec((1, tq, LANES), lambda bi, p, i, lo: (bi, i, p)),
        scratch_shapes=[pltpu.VMEM((lp, LANES), BF16)],
    )
    return pl.pallas_call(
        functools.partial(_attn_kernel, tq=tq, tk=tk),
        grid_spec=grid_spec,
        out_shape=jax.ShapeDtypeStruct((b, lp, ATT_W), BF16),
        compiler_params=_cparams(("parallel", "parallel", "arbitrary")),
        name="fox_attention",
    )(first, z3, z3, z3, fcol, frow, qg * LOG2E, kg)


def _convpool_kernel(a_ref, g_ref, p_ref, ah_ref, gh_ref, ph_ref, cw_ref, cb_ref, lg_ref, lb_ref,
                     pw_ref, ps_ref, u_ref, pm_ref, ext_ref, pext_ref, *, ts):
    i = pl.program_id(1)
    has_prev = i > 0

    def stage(ref, halo_rows, rows):
        ref[0, 0:HALO, :] = halo_rows
        ref[0, HALO:HALO + ts, :] = rows
        base = ref[0]
        for r in range(1, SUBLANES):
            ref[r, r:HALO + ts, :] = base[0:HALO + ts - r, :]

    def behind(ref, back):
        start = HALO - back // SUBLANES * SUBLANES
        return ref[back % SUBLANES, start:start + ts, :]

    u = a_ref[0].astype(F32) * _sigmoid(g_ref[0].astype(F32))
    uh = ah_ref[0].astype(F32) * _sigmoid(gh_ref[0].astype(F32))
    stage(ext_ref, jnp.where(has_prev, uh, 0.0), u)
    acc = jnp.zeros((ts, CONV_W), F32) + cb_ref[...]
    for j in range(CONV_K):
        acc = acc + cw_ref[j:j + 1, :] * behind(ext_ref, CONV_K - 1 - j)
    mu = jnp.mean(acc, axis=-1, keepdims=True)
    cen = acc - mu
    var = jnp.mean(cen * cen, axis=-1, keepdims=True)
    y = cen * lax.rsqrt(var + 1e-5) * lg_ref[...] + lb_ref[...]
    u_ref[0] = _silu(y).astype(u_ref.dtype)

    x = p_ref[0].astype(F32)
    stage(pext_ref, jnp.where(has_prev, ph_ref[0].astype(F32), 0.0), x)
    pos1 = (i * ts + 1 + lax.broadcasted_iota(jnp.int32, (ts, 1), 0)).astype(F32)
    lane = lax.broadcasted_iota(jnp.int32, (1, POOL_W), 1)
    group_w = POOL_W // len(POOL_WINDOWS)
    run = x
    pooled = jnp.zeros((ts, POOL_W), F32)
    for k in range(1, max(POOL_WINDOWS)):
        run = run + behind(pext_ref, k)
        if (k + 1) in POOL_WINDOWS:
            gi = POOL_WINDOWS.index(k + 1)
            mean = run / jnp.minimum(pos1, float(k + 1))
            in_group = (lane >= gi * group_w) & (lane < (gi + 1) * group_w)
            pooled = jnp.where(in_group, mean, pooled)
    pm = (pooled - x).astype(BF16)
    lin = jnp.dot(pm, pw_ref[...], preferred_element_type=F32) * ps_ref[...]
    pm_ref[0] = lin.astype(pm_ref.dtype)


def _convpool(z3, cw, cb, lg, lb, pw, ps, ts):
    b, lp, _ = z3.shape
    ab, gb, pb = Z_A // CONV_W, Z_G // CONV_W, Z_P // POOL_W
    hpt = ts // HALO
    main = lambda blk: pl.BlockSpec((1, ts, CONV_W), lambda bi, i: (bi, i, blk))
    halo = lambda blk: pl.BlockSpec((1, HALO, CONV_W), lambda bi, i: (bi, jnp.maximum(i * hpt - 1, 0), blk))
    const = lambda shape: pl.BlockSpec(shape, lambda bi, i: (0, 0))
    out = pl.BlockSpec((1, ts, CONV_W), lambda bi, i: (bi, i, 0))
    return pl.pallas_call(
        functools.partial(_convpool_kernel, ts=ts),
        grid=(b, lp // ts),
        in_specs=[main(ab), main(gb), main(pb), halo(ab), halo(gb), halo(pb),
                  const((HALO, CONV_W)), const((1, CONV_W)), const((1, CONV_W)), const((1, CONV_W)),
                  const((POOL_W, POOL_W)), const((1, POOL_W))],
        out_specs=[out, out],
        out_shape=[jax.ShapeDtypeStruct((b, lp, CONV_W), BF16), jax.ShapeDtypeStruct((b, lp, POOL_W), BF16)],
        scratch_shapes=[pltpu.VMEM((SUBLANES, HALO + ts, CONV_W), F32),
                        pltpu.VMEM((SUBLANES, HALO + ts, POOL_W), F32)],
        compiler_params=_cparams(("parallel", "arbitrary")),
        name="conv_pool",
    )(z3, z3, z3, z3, z3, z3, cw, cb, lg, lb, pw, ps)


def _merge_kernel(att_ref, u_ref, pm_ref, g0_ref, g1_ref, g2_ref, h_ref, wa_ref, wc_ref, wp_ref, wo_ref,
                  ng_ref, h_out_ref, hn_out_ref):
    ya = jnp.dot(att_ref[...], wa_ref[...], preferred_element_type=F32)
    yc = jnp.dot(u_ref[...], wc_ref[...], preferred_element_type=F32)
    yp = jnp.dot(pm_ref[...], wp_ref[...], preferred_element_type=F32)
    m = (_sigmoid(g0_ref[...].astype(F32)) * ya + _sigmoid(g1_ref[...].astype(F32)) * yc
         + _sigmoid(g2_ref[...].astype(F32)) * yp)
    h = h_ref[...] + jnp.dot(m.astype(BF16), wo_ref[...], preferred_element_type=F32)
    h_out_ref[...] = h
    hn_out_ref[...] = _rms(h, ng_ref[...]).astype(hn_out_ref.dtype)


def _merge(att, u, pm, z, h, wa, wc, wp, wo, ng, tm):
    n, d = h.shape
    row = lambda w, blk=0: pl.BlockSpec((tm, w), lambda i: (i, blk))
    const = lambda shape: pl.BlockSpec(shape, lambda i: (0, 0))
    return pl.pallas_call(
        _merge_kernel,
        grid=(n // tm,),
        in_specs=[row(ATT_W), row(CONV_W), row(POOL_W), row(d, 0), row(d, 1), row(d, 2), row(d),
                  const(wa.shape), const(wc.shape), const(wp.shape), const(wo.shape), const((1, d))],
        out_specs=[row(d), row(d)],
        out_shape=[jax.ShapeDtypeStruct((n, d), F32), jax.ShapeDtypeStruct((n, d), BF16)],
        compiler_params=_cparams(("parallel",)),
        name="merge_outproj",
    )(att, u, pm, z, z, z, h, wa, wc, wp, wo, ng)


def _ffn_kernel(x_ref, h_ref, wg_ref, wu_ref, wd_ref, ng_ref, h_out_ref, hn_out_ref, *, tf):
    x = x_ref[...]
    h = h_ref[...]
    nc = wg_ref.shape[1] // tf

    def gate_up(c):
        cols = slice(c * tf, (c + 1) * tf)
        return (jnp.dot(x, wg_ref[:, cols], preferred_element_type=F32),
                jnp.dot(x, wu_ref[:, cols], preferred_element_type=F32))

    gt, up = gate_up(0)
    for c in range(nc):
        act = (_silu(gt) * up).astype(BF16)
        if c + 1 < nc:
            gt, up = gate_up(c + 1)
        h = h + jnp.dot(act, wd_ref[c * tf:(c + 1) * tf, :], preferred_element_type=F32)
    h_out_ref[...] = h
    hn_out_ref[...] = _rms(h, ng_ref[...]).astype(hn_out_ref.dtype)


def _ffn(hn, h, wg, wu, wd, ng, tm, tf):
    n, d = h.shape
    row = pl.BlockSpec((tm, d), lambda i: (i, 0))
    const = lambda shape: pl.BlockSpec(shape, lambda i: (0, 0), pipeline_mode=pl.Buffered(1))
    return pl.pallas_call(
        functools.partial(_ffn_kernel, tf=tf),
        grid=(n // tm,),
        in_specs=[row, row, const(wg.shape), const(wu.shape), const(wd.shape), const((1, d))],
        out_specs=[row, row],
        out_shape=[jax.ShapeDtypeStruct((n, d), F32), jax.ShapeDtypeStruct((n, d), BF16)],
        compiler_params=_cparams(("parallel",)),
        name="dense_swiglu",
    )(hn, h, wg, wu, wd, ng)


def _router_kernel(h_ref, ng_ref, wr_ref, br_ref, comb_ref, pos_ref, cnt_ref, *, sb):
    tm = h_ref.shape[0]
    hn = _rms(h_ref[...], ng_ref[...])
    logits = jnp.dot(hn, wr_ref[...], precision=HIGHEST, preferred_element_type=F32) + br_ref[...]
    lane = lax.broadcasted_iota(jnp.int32, (1, LANES), 1).astype(F32)
    lg = jnp.where(lane < N_EXPERTS, logits, -jnp.inf)
    m1 = jnp.max(lg, axis=-1, keepdims=True)
    i1 = jnp.min(jnp.where(lg == m1, lane, float(LANES)), axis=-1, keepdims=True)
    sel1 = lane == i1
    lg2 = jnp.where(sel1, -jnp.inf, lg)
    m2 = jnp.max(lg2, axis=-1, keepdims=True)
    i2 = jnp.min(jnp.where(lg2 == m2, lane, float(LANES)), axis=-1, keepdims=True)
    sel2 = lane == i2
    e = jnp.exp(m2 - m1)
    g1 = 1.0 / (1.0 + e)
    comb_ref[...] = jnp.where(sel1, g1, 0.0) + jnp.where(sel2, e * g1, 0.0)
    sel = jnp.where(sel1 | sel2, 1.0, 0.0)
    r = lax.broadcasted_iota(jnp.int32, (sb, sb), 0)
    c = lax.broadcasted_iota(jnp.int32, (sb, sb), 1)
    tril = jnp.where(c <= r, 1.0, 0.0).astype(BF16)
    carry = jnp.zeros((1, LANES), F32)
    for s in range(tm // sb):
        blk = slice(s * sb, (s + 1) * sb)
        incl = jnp.dot(tril, sel[blk].astype(BF16), preferred_element_type=F32) + carry
        pos_ref[blk, :] = jnp.where(sel[blk] > 0.0, incl - 1.0, -1.0)
        carry = incl[sb - 1:sb, :]
    cnt_ref[0] = jnp.broadcast_to(carry, (8, LANES)).astype(jnp.int32)


def _router(h, ng, wr, br, tm, sb):
    n, d = h.shape
    nt = n // tm
    return pl.pallas_call(
        functools.partial(_router_kernel, sb=sb),
        grid=(nt,),
        in_specs=[pl.BlockSpec((tm, d), lambda i: (i, 0)),
                  pl.BlockSpec((1, d), lambda i: (0, 0)),
                  pl.BlockSpec((d, LANES), lambda i: (0, 0)),
                  pl.BlockSpec((1, LANES), lambda i: (0, 0))],
        out_specs=[pl.BlockSpec((tm, LANES), lambda i: (i, 0)),
                   pl.BlockSpec((tm, LANES), lambda i: (i, 0)),
                   pl.BlockSpec((1, 8, LANES), lambda i: (i, 0, 0))],
        out_shape=[jax.ShapeDtypeStruct((n, LANES), F32), jax.ShapeDtypeStruct((n, LANES), F32),
                   jax.ShapeDtypeStruct((nt, 8, LANES), jnp.int32)],
        compiler_params=_cparams(("parallel",)),
        name="router_top2",
    )(h, ng, wr, br)


def _moe_kernel(cnt_ref, x_ref, post_ref, pos_ref, comb_ref, wg_ref, wu_ref, wd_ref,
                out_ref, xs_ref, ys_ref, *, sizes, sub):
    i = pl.program_id(0)
    e = pl.program_id(1)
    f = pl.program_id(2)
    nf = pl.num_programs(2)
    cnt = cnt_ref[i * N_EXPERTS + e]
    big = sizes[-1]

    def for_each_chunk(body):
        lo = 0
        for r in sizes:
            @pl.when((cnt > lo) & (cnt <= r))
            def _(r=r):
                body(0, r)
            lo = r

        @pl.when(cnt > big)
        def _():
            def step(c, carry):
                body(pl.multiple_of(c * big, big), big)
                return carry
            lax.fori_loop(0, (cnt + big - 1) // big, step, 0)

    @pl.when((e == 0) & (f == 0))
    def _():
        out_ref[...] = jnp.zeros_like(out_ref)

    @pl.when(f == 0)
    def _():
        prow = post_ref[0, pl.ds(e, 1), :]

        def gather(off, r):
            tgt = (off + lax.broadcasted_iota(jnp.int32, (r, 1), 0)).astype(F32)
            onehot = jnp.where(prow == tgt, 1.0, 0.0).astype(BF16)
            xs_ref[pl.ds(off, r), :] = jnp.dot(onehot, x_ref[...], preferred_element_type=F32).astype(BF16)
            ys_ref[pl.ds(off, r), :] = jnp.zeros((r, D_MODEL), F32)
        for_each_chunk(gather)

    def expert(off, r):
        xc = xs_ref[pl.ds(off, r), :]
        y = ys_ref[pl.ds(off, r), :]
        nsub = wg_ref.shape[2] // sub

        def gate_up(s):
            cols = slice(s * sub, (s + 1) * sub)
            return (jnp.dot(xc, wg_ref[0, :, cols], preferred_element_type=F32),
                    jnp.dot(xc, wu_ref[0, :, cols], preferred_element_type=F32))

        gt, up = gate_up(0)
        for s in range(nsub):
            act = (_silu(gt) * up).astype(BF16)
            if s + 1 < nsub:
                gt, up = gate_up(s + 1)
            y = y + jnp.dot(act, wd_ref[0, s * sub:(s + 1) * sub, :], preferred_element_type=F32)
        ys_ref[pl.ds(off, r), :] = y
    for_each_chunk(expert)

    @pl.when(f == nf - 1)
    def _():
        lane = lax.broadcasted_iota(jnp.int32, (1, LANES), 1)
        pcol = jnp.sum(jnp.where(lane == e, pos_ref[...], 0.0), axis=-1, keepdims=True)
        gcol = jnp.sum(jnp.where(lane == e, comb_ref[...], 0.0), axis=-1, keepdims=True)

        def scatter(off, r):
            tgt = (off + lax.broadcasted_iota(jnp.int32, (1, r), 1)).astype(F32)
            onehot_t = jnp.where(pcol == tgt, 1.0, 0.0).astype(BF16)
            y = ys_ref[pl.ds(off, r), :].astype(BF16)
            out_ref[...] += gcol * jnp.dot(onehot_t, y, preferred_element_type=F32)
        for_each_chunk(scatter)


def _moe(cnt, hn, post, pos, comb, wg, wu, wd, tm, tf, sizes, sub):
    n, d = hn.shape
    nf = wg.shape[2] // tf
    cap = -(-tm // sizes[-1]) * sizes[-1]
    once = pl.Buffered(1)
    row = lambda w: pl.BlockSpec((tm, w), lambda i, e, f, cnt: (i, 0), pipeline_mode=once)
    grid_spec = pltpu.PrefetchScalarGridSpec(
        num_scalar_prefetch=1,
        grid=(n // tm, N_EXPERTS, nf),
        in_specs=[row(d),
                  pl.BlockSpec((1, POST_ROWS, tm), lambda i, e, f, cnt: (i, 0, 0), pipeline_mode=once),
                  row(LANES), row(LANES),
                  pl.BlockSpec((1, d, tf), lambda i, e, f, cnt: (e, 0, f)),
                  pl.BlockSpec((1, d, tf), lambda i, e, f, cnt: (e, 0, f)),
                  pl.BlockSpec((1, tf, d), lambda i, e, f, cnt: (e, f, 0))],
        out_specs=pl.BlockSpec((tm, d), lambda i, e, f, cnt: (i, 0)),
        scratch_shapes=[pltpu.VMEM((cap, d), BF16), pltpu.VMEM((cap, d), F32)],
    )
    return pl.pallas_call(
        functools.partial(_moe_kernel, sizes=sizes, sub=sub),
        grid_spec=grid_spec,
        out_shape=jax.ShapeDtypeStruct((n, d), F32),
        compiler_params=pltpu.CompilerParams(dimension_semantics=("parallel", "arbitrary", "arbitrary"),
                                             vmem_limit_bytes=MOE_VMEM_LIMIT),
        name="expert_swiglu",
    )(cnt, hn, post, pos, comb, wg, wu, wd)


def _add_norm_kernel(h_ref, d_ref, g_ref, h_out_ref, hn_out_ref):
    h = h_ref[...] + d_ref[...]
    h_out_ref[...] = h
    hn_out_ref[...] = _rms(h, g_ref[...]).astype(hn_out_ref.dtype)


def _add_norm(h, delta, g, tm):
    n, d = h.shape
    row = pl.BlockSpec((tm, d), lambda i: (i, 0))
    return pl.pallas_call(
        _add_norm_kernel,
        grid=(n // tm,),
        in_specs=[row, row, pl.BlockSpec((1, d), lambda i: (0, 0))],
        out_specs=[row, row],
        out_shape=[jax.ShapeDtypeStruct((n, d), F32), jax.ShapeDtypeStruct((n, d), BF16)],
        compiler_params=_cparams(("parallel",)),
        name="residual_norm",
    )(h, delta, g)


def _pad_lanes(a, width=LANES):
    return jnp.pad(a, ((0, 0), (0, width - a.shape[-1])))


def kernel(x, meta, norm_mix, w_in, b_in, q_norm, k_norm, w_attn_o, conv_w, conv_b, conv_ln_g, conv_ln_b,
           w_conv_o, pool_w, pool_scale, w_pool_o, w_out, norm_ffn, w_ff_gate, w_ff_up, w_ff_down, w_router,
           b_router, w_e_gate, w_e_up, w_e_down):
    bsz, seq, d = x.shape
    depth = w_in.shape[0]
    length = seq + N_META
    lp = -(-length // SEQ_ALIGN) * SEQ_ALIGN
    n = bsz * lp

    tm = _pick(n, (1536, 768, 512, 256))
    tm_mid = _pick(n, (768, 512, 256))
    tm_ffn = _pick(n, (768, 512, 256))
    tm_moe = _pick(n, (1408, 768, 512, 256))
    sb_router = _pick(tm_moe, (704, 768, 512, 256))
    tn = _pick(Z_W, (1792, 768, 256))
    ts = _pick(lp, (768, 512, 256))
    ts_f = ts
    tk = 256
    tq = _pick(lp, (768, 256)) if lp >= 1024 else 256
    tf_dense = _pick(w_ff_gate.shape[2], (256,))
    tf_moe = _pick(w_e_gate.shape[3], (1792, 512, 256))
    sub_moe = _pick(tf_moe, (256,))
    quarter = tm_moe // 4
    moe_sizes = tuple(range(max(quarter - 96, 64) // 64 * 64, quarter + 161, 64))

    h = jnp.concatenate([jnp.broadcast_to(meta[None].astype(x.dtype), (bsz, N_META, d)), x], axis=1)
    h = jnp.pad(h, ((0, 0), (0, lp - length), (0, 0))).reshape(n, d)
    hn = _norm(h, norm_mix[0][None], tm)

    att_scale = HEAD_DIM ** -0.5
    gate_lo = 3 * ATT_W + ATT_HEADS + 2 * CONV_W + POOL_W
    f_lo = 3 * ATT_W
    for l in range(depth):
        wl = w_in[l]
        w_main = jnp.concatenate([wl[:, gate_lo:], wl[:, :f_lo], wl[:, f_lo + ATT_HEADS:gate_lo]], axis=1).astype(BF16)
        bl = b_in[l]
        b_main = jnp.concatenate([bl[gate_lo:], bl[:f_lo], bl[f_lo + ATT_HEADS:gate_lo]])[None]
        w_f = _pad_lanes(wl[:, f_lo:f_lo + ATT_HEADS])
        w_f_hi = w_f.astype(BF16)
        w_f = jnp.concatenate([w_f_hi, (w_f - w_f_hi.astype(F32)).astype(BF16)], axis=1)
        b_f = _pad_lanes(bl[None, f_lo:f_lo + ATT_HEADS])

        z = _inproj(hn, w_main, b_main, tm, tn)
        z3 = z.reshape(bsz, lp, Z_W)
        fcol = _forget(hn.reshape(bsz, lp, d), w_f, b_f, ts_f)
        frow = jnp.transpose(fcol[:, :, :ATT_HEADS], (0, 2, 1))
        qg = jnp.tile(q_norm[l], 2)[None] * att_scale
        kg = jnp.tile(k_norm[l], 2)[None]
        att = _attention(z3, fcol, frow, qg, kg, tq, tk)

        cw = jnp.pad(conv_w[l], ((0, HALO - CONV_K), (0, 0)))
        pw = jax.scipy.linalg.block_diag(*[pool_w[l, g] for g in range(pool_w.shape[1])]).astype(BF16)
        u, pm = _convpool(z3, cw, conv_b[l][None], conv_ln_g[l][None], conv_ln_b[l][None], pw,
                          pool_scale[l][None], ts)

        h, hn = _merge(att.reshape(n, ATT_W), u.reshape(n, CONV_W), pm.reshape(n, POOL_W), z, h,
                       w_attn_o[l].astype(BF16), w_conv_o[l].astype(BF16), w_pool_o[l].astype(BF16),
                       w_out[l].astype(BF16), norm_ffn[l][None], tm_mid)

        ng_next = norm_mix[min(l + 1, depth - 1)][None]
        i = l // 2
        if l % 2 == 0:
            h, hn = _ffn(hn, h, w_ff_gate[i].astype(BF16), w_ff_up[i].astype(BF16), w_ff_down[i].astype(BF16),
                         ng_next, tm_ffn, tf_dense)
        else:
            comb, pos, cnt = _router(h, norm_ffn[l][None], _pad_lanes(w_router[i]), _pad_lanes(b_router[i][None]),
                                     tm_moe, sb_router)
            post = jnp.transpose(pos[:, :POST_ROWS].reshape(n // tm_moe, tm_moe, POST_ROWS), (0, 2, 1))
            cnt_flat = cnt[:, 0, :N_EXPERTS].reshape(-1)
            delta = _moe(cnt_flat, hn, post, pos, comb, _to_bf16(w_e_gate, i), _to_bf16(w_e_up, i),
                         _to_bf16(w_e_down, i), tm_moe, tf_moe, moe_sizes, sub_moe)
            h, hn = _add_norm(h, delta, ng_next, tm)

    return h.reshape(bsz, lp, d)[:, N_META:length]
```

```python
import functools

import jax
import jax.numpy as jnp
from jax import lax
from jax.experimental import pallas as pl
from jax.experimental.pallas import tpu as pltpu

F32 = jnp.float32
BF16 = jnp.bfloat16
HIGHEST = lax.Precision.HIGHEST

D_MODEL = 1024
N_META = 16
HEAD_DIM = 64
ATT_W = 512
ATT_HEADS = 8
CONV_W = 256
CONV_K = 31
POOL_W = 256
POOL_WINDOWS = (2, 4, 8, 16)
N_EXPERTS = 8
LANES = 128
SUBLANES = 8
HALO = 32
NEG_INF = -1e30
LOG2E = 1.4426950408889634
KNORM_ROWS = 256
EXP_UNDERFLOW = 110.0
SEQ_ALIGN = 256
VMEM_LIMIT = 56 * 1024 * 1024
MOE_VMEM_LIMIT = 60 * 1024 * 1024
POST_ROWS = 32

Z_GATE = 0
Z_Q = 3072
Z_K = 3584
Z_V = 4096
Z_A = 4608
Z_G = 4864
Z_P = 5120
Z_W = 5376


def _pick(n, candidates):
    for c in candidates:
        if n % c == 0:
            return c
    raise ValueError(f"no tile for {n} in {candidates}")


def _cparams(sem):
    return pltpu.CompilerParams(dimension_semantics=sem, vmem_limit_bytes=VMEM_LIMIT)


def _rms(x, g, eps=1e-6):
    return x * lax.rsqrt(jnp.mean(x * x, axis=-1, keepdims=True) + eps) * g


def _sigmoid(x):
    return 0.5 * jnp.tanh(0.5 * x) + 0.5


def _silu(x):
    return x * _sigmoid(x)


def _cast_kernel(w_ref, o_ref):
    o_ref[...] = w_ref[0].astype(o_ref.dtype)


def _to_bf16(w, layer):
    _, s, r, c = w.shape
    rows = _pick(r, (512, 256, 128, 8))
    return pl.pallas_call(
        _cast_kernel,
        grid=(s, r // rows),
        in_specs=[pl.BlockSpec((1, 1, rows, c), lambda i, j: (layer, i, j, 0))],
        out_specs=pl.BlockSpec((1, rows, c), lambda i, j: (i, j, 0)),
        out_shape=jax.ShapeDtypeStruct((s, r, c), BF16),
        compiler_params=_cparams(("parallel", "parallel")),
        name="weights_to_bf16",
    )(w)


def _norm_kernel(h_ref, g_ref, o_ref):
    o_ref[...] = _rms(h_ref[...], g_ref[...]).astype(o_ref.dtype)


def _norm(h, g, tm):
    n, d = h.shape
    return pl.pallas_call(
        _norm_kernel,
        grid=(n // tm,),
        in_specs=[pl.BlockSpec((tm, d), lambda i: (i, 0)), pl.BlockSpec((1, d), lambda i: (0, 0))],
        out_specs=pl.BlockSpec((tm, d), lambda i: (i, 0)),
        out_shape=jax.ShapeDtypeStruct((n, d), BF16),
        compiler_params=_cparams(("parallel",)),
        name="rmsnorm",
    )(h, g)


def _inproj_kernel(x_ref, w_ref, b_ref, o_ref):
    acc = jnp.dot(x_ref[...], w_ref[...], preferred_element_type=F32)
    o_ref[...] = (acc + b_ref[...]).astype(o_ref.dtype)


def _inproj(hn, w, b, tm, tn):
    n, d = hn.shape
    zw = w.shape[1]
    return pl.pallas_call(
        _inproj_kernel,
        grid=(n // tm, zw // tn),
        in_specs=[pl.BlockSpec((tm, d), lambda i, j: (i, 0)),
                  pl.BlockSpec((d, tn), lambda i, j: (0, j)),
                  pl.BlockSpec((1, tn), lambda i, j: (0, j))],
        out_specs=pl.BlockSpec((tm, tn), lambda i, j: (i, j)),
        out_shape=jax.ShapeDtypeStruct((n, zw), BF16),
        compiler_params=_cparams(("parallel", "arbitrary")),
        name="inproj",
    )(hn, w, b)


def _forget_kernel(x_ref, w_ref, b_ref, o_ref, carry_ref):
    @pl.when(pl.program_id(1) == 0)
    def _():
        carry_ref[...] = jnp.zeros_like(carry_ref)

    fw = jnp.dot(x_ref[0], w_ref[...], preferred_element_type=F32)
    f = fw[:, :LANES] + fw[:, LANES:] + b_ref[...]
    ls = jnp.minimum(f, 0.0) - jnp.log(1.0 + jnp.exp(-jnp.abs(f)))
    hi = ls.astype(BF16)
    r1 = ls - hi.astype(F32)
    mid = r1.astype(BF16)
    lo = (r1 - mid.astype(F32)).astype(BF16)
    t = ls.shape[0]
    r = lax.broadcasted_iota(jnp.int32, (t, t), 0)
    c = lax.broadcasted_iota(jnp.int32, (t, t), 1)
    tril = jnp.where(c <= r, 1.0, 0.0).astype(BF16)
    parts = jnp.dot(tril, jnp.concatenate([hi, mid, lo], axis=-1), preferred_element_type=F32)
    cs = (parts[:, :LANES] + parts[:, LANES:2 * LANES]) + parts[:, 2 * LANES:] + carry_ref[...]
    o_ref[0] = cs
    carry_ref[...] = cs[t - 1:t, :]


def _forget(hn3, w_f, b_f, ts):
    b, lp, d = hn3.shape
    return pl.pallas_call(
        _forget_kernel,
        grid=(b, lp // ts),
        in_specs=[pl.BlockSpec((1, ts, d), lambda bi, t: (bi, t, 0)),
                  pl.BlockSpec((d, 2 * LANES), lambda bi, t: (0, 0)),
                  pl.BlockSpec((1, LANES), lambda bi, t: (0, 0))],
        out_specs=pl.BlockSpec((1, ts, LANES), lambda bi, t: (bi, t, 0)),
        out_shape=jax.ShapeDtypeStruct((b, lp, LANES), F32),
        scratch_shapes=[pltpu.VMEM((1, LANES), F32)],
        compiler_params=_cparams(("parallel", "arbitrary")),
        name="forget_cumsum",
    )(hn3, w_f, b_f)


def _pair_rms(x, first_head, gain, eps=1e-6):
    sq = x * x
    s0 = jnp.sum(jnp.where(first_head, sq, 0.0), axis=-1, keepdims=True)
    s1 = jnp.sum(jnp.where(first_head, 0.0, sq), axis=-1, keepdims=True)
    ms = jnp.where(first_head, s0, s1) * (1.0 / HEAD_DIM)
    return x * lax.rsqrt(ms + eps) * gain


def _attn_kernel(lo_ref, q_ref, k_ref, v_ref, fc_ref, fr_ref, qg_ref, kg_ref, o_ref, kn_ref, *,
                 tq, ts, tk, lead):
    bi = pl.program_id(0)
    p = pl.program_id(1)
    i = pl.program_id(2)
    nq = pl.num_programs(2)
    lp = k_ref.shape[1]
    lane = lax.broadcasted_iota(jnp.int32, (1, LANES), 1)
    first_head = lane < HEAD_DIM

    @pl.when(i == 0)
    def _():
        def body(c, carry):
            off = pl.multiple_of(c * KNORM_ROWS, KNORM_ROWS)
            kk = k_ref[0, pl.ds(off, KNORM_ROWS), :].astype(F32)
            kn_ref[pl.ds(off, KNORM_ROWS), :] = _pair_rms(kk, first_head, kg_ref[...]).astype(BF16)
            return carry
        lax.fori_loop(0, lp // KNORM_ROWS, body, 0)

    qn = _pair_rms(q_ref[0].astype(F32), first_head, qg_ref[...])
    fc = fc_ref[0]

    heads = [2 * p, 2 * p + 1]
    qhs = [jnp.where(first_head, qn, 0.0).astype(BF16), jnp.where(first_head, 0.0, qn).astype(BF16)]
    fts = [jnp.sum(jnp.where(lane == hd, fc, 0.0), axis=-1, keepdims=True) * LOG2E for hd in heads]

    def update(rows, off, width, carry, hh, row_pos=None):
        m, l, acc = carry
        ks = kn_ref[pl.ds(off, width), :]
        s = lax.dot_general(qhs[hh][rows], ks, (((1,), (1,)), ((), ())), preferred_element_type=F32)
        f_all = fr_ref[0, :, pl.ds(off, width)]
        head_row = lax.broadcasted_iota(jnp.int32, (ATT_HEADS, 1), 0) == heads[hh]
        fs = jnp.sum(jnp.where(head_row, f_all, 0.0), axis=0, keepdims=True) * LOG2E
        s = s + (fts[hh][rows] - fs)
        if row_pos is not None:
            col_pos = off + lax.broadcasted_iota(jnp.int32, (1, width), 1)
            s = jnp.where(col_pos <= row_pos, s, NEG_INF)
        m_new = jnp.maximum(m, jnp.max(s, axis=-1, keepdims=True))
        alpha = jnp.exp2(m - m_new)
        pm = jnp.exp2(s - m_new)
        l = alpha * l + jnp.sum(pm, axis=-1, keepdims=True)
        vs = v_ref[0, pl.ds(off, width), :]
        acc = alpha * acc + jnp.dot(pm.astype(BF16), vs, preferred_element_type=F32)
        return m_new, l, acc

    nsub = tq // ts
    init = (jnp.full((ts, 1), NEG_INF, F32), jnp.zeros((ts, 1), F32), jnp.zeros((ts, LANES), F32))
    carries, wins = [], []
    for u in range(nsub):
        rows = slice(u * ts, (u + 1) * ts)
        g = i * nsub + u
        win = jnp.maximum(g * (ts // tk) - lead // tk, 0)
        firsts = [lo_ref[(bi * ATT_HEADS + hd) * (nq * nsub) + g] for hd in heads]

        def far(j, carry, rows=rows):
            off = pl.multiple_of(j * tk, tk)
            return tuple(update(rows, off, tk, carry[hh], hh) for hh in range(2))

        carries.append(lax.fori_loop(jnp.minimum(firsts[0], firsts[1]), win, far, (init, init)))
        wins.append(win)
    for u in range(nsub):
        rows = slice(u * ts, (u + 1) * ts)
        row_pos = i * tq + u * ts + lax.broadcasted_iota(jnp.int32, (ts, 1), 0)
        off = pl.multiple_of(wins[u] * tk, tk)
        outs = []
        for hh in range(2):
            m, l, acc = update(rows, off, lead + ts, carries[u][hh], hh, row_pos)
            outs.append(acc / l)
        o_ref[0, rows, :] = jnp.where(first_head, outs[0], outs[1]).astype(o_ref.dtype)


def _first_live_chunk(frow, qg, kg, tq, tk):
    b, nh, lp = frow.shape
    qk_bound = 1.02 * HEAD_DIM * jnp.max(jnp.abs(qg)) * jnp.max(jnp.abs(kg))
    f_first = frow[:, :, 0::tq]
    f_last = frow[:, :, tk - 1::tk]
    dead = (f_first[:, :, :, None] - f_last[:, :, None, :] + 2.0 * qk_bound) < -EXP_UNDERFLOW
    n_dead = jnp.sum(dead.astype(jnp.int32), axis=-1)
    n_full = (jnp.arange(lp // tq, dtype=jnp.int32) * tq) // tk
    return jnp.minimum(n_dead, n_full[None, None, :]).reshape(-1)


def _attention(z3, fcol, frow, qg, kg, tq, ts, tk, lead):
    b, lp, _ = z3.shape
    npairs = ATT_HEADS // 2
    qb, kb, vb = Z_Q // LANES, Z_K // LANES, Z_V // LANES
    assert tq % ts == 0 and ts % tk == 0 and lead % tk == 0 and lp >= lead + ts and lp % KNORM_ROWS == 0
    first = _first_live_chunk(frow, qg, kg, ts, tk)
    grid_spec = pltpu.PrefetchScalarGridSpec(
        num_scalar_prefetch=1,
        grid=(b, npairs, lp // tq),
        in_specs=[pl.BlockSpec((1, tq, LANES), lambda bi, p, i, lo: (bi, i, qb + p)),
                  pl.BlockSpec((1, lp, LANES), lambda bi, p, i, lo: (bi, 0, kb + p)),
                  pl.BlockSpec((1, lp, LANES), lambda bi, p, i, lo: (bi, 0, vb + p)),
                  pl.BlockSpec((1, tq, LANES), lambda bi, p, i, lo: (bi, i, 0)),
                  pl.BlockSpec((1, ATT_HEADS, lp), lambda bi, p, i, lo: (bi, 0, 0)),
                  pl.BlockSpec((1, LANES), lambda bi, p, i, lo: (0, 0)),
                  pl.BlockSpec((1, LANES), lambda bi, p, i, lo: (0, 0))],
        out_specs=pl.BlockSpec((1, tq, LANES), lambda bi, p, i, lo: (bi, i, p)),
        scratch_shapes=[pltpu.VMEM((lp, LANES), BF16)],
    )
    return pl.pallas_call(
        functools.partial(_attn_kernel, tq=tq, ts=ts, tk=tk, lead=lead),
        grid_spec=grid_spec,
        out_shape=jax.ShapeDtypeStruct((b, lp, ATT_W), BF16),
        compiler_params=_cparams(("parallel", "parallel", "arbitrary")),
        name="fox_attention",
    )(first, z3, z3, z3, fcol, frow, qg * LOG2E, kg)


def _convpool_kernel(a_ref, g_ref, p_ref, ah_ref, gh_ref, ph_ref, cw_ref, cb_ref, lg_ref, lb_ref,
                     pw_ref, ps_ref, u_ref, pm_ref, ext_ref, pext_ref, *, ts):
    i = pl.program_id(1)
    has_prev = i > 0

    def stage(ref, halo_rows, rows):
        ref[0, 0:HALO, :] = halo_rows
        ref[0, HALO:HALO + ts, :] = rows
        base = ref[0]
        for r in range(1, SUBLANES):
            ref[r, r:HALO + ts, :] = base[0:HALO + ts - r, :]

    def behind(ref, back):
        start = HALO - back // SUBLANES * SUBLANES
        return ref[back % SUBLANES, start:start + ts, :]

    u = a_ref[0].astype(F32) * _sigmoid(g_ref[0].astype(F32))
    uh = ah_ref[0].astype(F32) * _sigmoid(gh_ref[0].astype(F32))
    stage(ext_ref, jnp.where(has_prev, uh, 0.0), u)
    acc = jnp.zeros((ts, CONV_W), F32) + cb_ref[...]
    for j in range(CONV_K):
        acc = acc + cw_ref[j:j + 1, :] * behind(ext_ref, CONV_K - 1 - j)
    mu = jnp.mean(acc, axis=-1, keepdims=True)
    cen = acc - mu
    var = jnp.mean(cen * cen, axis=-1, keepdims=True)
    y = cen * lax.rsqrt(var + 1e-5) * lg_ref[...] + lb_ref[...]
    u_ref[0] = _silu(y).astype(u_ref.dtype)

    x = p_ref[0].astype(F32)
    stage(pext_ref, jnp.where(has_prev, ph_ref[0].astype(F32), 0.0), x)
    pos1 = (i * ts + 1 + lax.broadcasted_iota(jnp.int32, (ts, 1), 0)).astype(F32)
    lane = lax.broadcasted_iota(jnp.int32, (1, POOL_W), 1)
    group_w = POOL_W // len(POOL_WINDOWS)
    run = x
    pooled = jnp.zeros((ts, POOL_W), F32)
    for k in range(1, max(POOL_WINDOWS)):
        run = run + behind(pext_ref, k)
        if (k + 1) in POOL_WINDOWS:
            gi = POOL_WINDOWS.index(k + 1)
            mean = run / jnp.minimum(pos1, float(k + 1))
            in_group = (lane >= gi * group_w) & (lane < (gi + 1) * group_w)
            pooled = jnp.where(in_group, mean, pooled)
    pm = (pooled - x).astype(BF16)
    lin = jnp.dot(pm, pw_ref[...], preferred_element_type=F32) * ps_ref[...]
    pm_ref[0] = lin.astype(pm_ref.dtype)


def _convpool(z3, cw, cb, lg, lb, pw, ps, ts):
    b, lp, _ = z3.shape
    ab, gb, pb = Z_A // CONV_W, Z_G // CONV_W, Z_P // POOL_W
    hpt = ts // HALO
    main = lambda blk: pl.BlockSpec((1, ts, CONV_W), lambda bi, i: (bi, i, blk))
    halo = lambda blk: pl.BlockSpec((1, HALO, CONV_W), lambda bi, i: (bi, jnp.maximum(i * hpt - 1, 0), blk))
    const = lambda shape: pl.BlockSpec(shape, lambda bi, i: (0, 0))
    out = pl.BlockSpec((1, ts, CONV_W), lambda bi, i: (bi, i, 0))
    return pl.pallas_call(
        functools.partial(_convpool_kernel, ts=ts),
        grid=(b, lp // ts),
        in_specs=[main(ab), main(gb), main(pb), halo(ab), halo(gb), halo(pb),
                  const((HALO, CONV_W)), const((1, CONV_W)), const((1, CONV_W)), const((1, CONV_W)),
                  const((POOL_W, POOL_W)), const((1, POOL_W))],
        out_specs=[out, out],
        out_shape=[jax.ShapeDtypeStruct((b, lp, CONV_W), BF16), jax.ShapeDtypeStruct((b, lp, POOL_W), BF16)],
        scratch_shapes=[pltpu.VMEM((SUBLANES, HALO + ts, CONV_W), F32),
                        pltpu.VMEM((SUBLANES, HALO + ts, POOL_W), F32)],
        compiler_params=_cparams(("parallel", "arbitrary")),
        name="conv_pool",
    )(z3, z3, z3, z3, z3, z3, cw, cb, lg, lb, pw, ps)


def _merge_kernel(att_ref, u_ref, pm_ref, g0_ref, g1_ref, g2_ref, h_ref, wa_ref, wc_ref, wp_ref, wo_ref,
                  ng_ref, h_out_ref, hn_out_ref):
    ya = jnp.dot(att_ref[...], wa_ref[...], preferred_element_type=F32)
    yc = jnp.dot(u_ref[...], wc_ref[...], preferred_element_type=F32)
    yp = jnp.dot(pm_ref[...], wp_ref[...], preferred_element_type=F32)
    m = (_sigmoid(g0_ref[...].astype(F32)) * ya + _sigmoid(g1_ref[...].astype(F32)) * yc
         + _sigmoid(g2_ref[...].astype(F32)) * yp)
    h = h_ref[...] + jnp.dot(m.astype(BF16), wo_ref[...], preferred_element_type=F32)
    h_out_ref[...] = h
    hn_out_ref[...] = _rms(h, ng_ref[...]).astype(hn_out_ref.dtype)


def _merge(att, u, pm, z, h, wa, wc, wp, wo, ng, tm):
    n, d = h.shape
    row = lambda w, blk=0: pl.BlockSpec((tm, w), lambda i: (i, blk))
    const = lambda shape: pl.BlockSpec(shape, lambda i: (0, 0))
    return pl.pallas_call(
        _merge_kernel,
        grid=(n // tm,),
        in_specs=[row(ATT_W), row(CONV_W), row(POOL_W), row(d, 0), row(d, 1), row(d, 2), row(d),
                  const(wa.shape), const(wc.shape), const(wp.shape), const(wo.shape), const((1, d))],
        out_specs=[row(d), row(d)],
        out_shape=[jax.ShapeDtypeStruct((n, d), F32), jax.ShapeDtypeStruct((n, d), BF16)],
        compiler_params=_cparams(("parallel",)),
        name="merge_outproj",
    )(att, u, pm, z, z, z, h, wa, wc, wp, wo, ng)


def _ffn_kernel(x_ref, h_ref, wg_ref, wu_ref, wd_ref, ng_ref, h_out_ref, hn_out_ref, *, tf):
    x = x_ref[...]
    h = h_ref[...]
    nc = wg_ref.shape[1] // tf

    def gate_up(c):
        cols = slice(c * tf, (c + 1) * tf)
        return (jnp.dot(x, wg_ref[:, cols], preferred_element_type=F32),
                jnp.dot(x, wu_ref[:, cols], preferred_element_type=F32))

    gt, up = gate_up(0)
    for c in range(nc):
        act = (_silu(gt) * up).astype(BF16)
        if c + 1 < nc:
            gt, up = gate_up(c + 1)
        h = h + jnp.dot(act, wd_ref[c * tf:(c + 1) * tf, :], preferred_element_type=F32)
    h_out_ref[...] = h
    hn_out_ref[...] = _rms(h, ng_ref[...]).astype(hn_out_ref.dtype)


def _ffn(hn, h, wg, wu, wd, ng, tm, tf):
    n, d = h.shape
    row = pl.BlockSpec((tm, d), lambda i: (i, 0))
    const = lambda shape: pl.BlockSpec(shape, lambda i: (0, 0), pipeline_mode=pl.Buffered(1))
    return pl.pallas_call(
        functools.partial(_ffn_kernel, tf=tf),
        grid=(n // tm,),
        in_specs=[row, row, const(wg.shape), const(wu.shape), const(wd.shape), const((1, d))],
        out_specs=[row, row],
        out_shape=[jax.ShapeDtypeStruct((n, d), F32), jax.ShapeDtypeStruct((n, d), BF16)],
        compiler_params=_cparams(("parallel",)),
        name="dense_swiglu",
    )(hn, h, wg, wu, wd, ng)


def _router_kernel(h_ref, ng_ref, wr_ref, br_ref, comb_ref, pos_ref, cnt_ref, *, sb):
    tm = h_ref.shape[0]
    hn = _rms(h_ref[...], ng_ref[...])
    logits = jnp.dot(hn, wr_ref[...], precision=HIGHEST, preferred_element_type=F32) + br_ref[...]
    lane = lax.broadcasted_iota(jnp.int32, (1, LANES), 1).astype(F32)
    lg = jnp.where(lane < N_EXPERTS, logits, -jnp.inf)
    m1 = jnp.max(lg, axis=-1, keepdims=True)
    i1 = jnp.min(jnp.where(lg == m1, lane, float(LANES)), axis=-1, keepdims=True)
    sel1 = lane == i1
    lg2 = jnp.where(sel1, -jnp.inf, lg)
    m2 = jnp.max(lg2, axis=-1, keepdims=True)
    i2 = jnp.min(jnp.where(lg2 == m2, lane, float(LANES)), axis=-1, keepdims=True)
    sel2 = lane == i2
    e = jnp.exp(m2 - m1)
    g1 = 1.0 / (1.0 + e)
    comb_ref[...] = jnp.where(sel1, g1, 0.0) + jnp.where(sel2, e * g1, 0.0)
    sel = jnp.where(sel1 | sel2, 1.0, 0.0)
    r = lax.broadcasted_iota(jnp.int32, (sb, sb), 0)
    c = lax.broadcasted_iota(jnp.int32, (sb, sb), 1)
    tril = jnp.where(c <= r, 1.0, 0.0).astype(BF16)
    carry = jnp.zeros((1, LANES), F32)
    for s in range(tm // sb):
        blk = slice(s * sb, (s + 1) * sb)
        incl = jnp.dot(tril, sel[blk].astype(BF16), preferred_element_type=F32) + carry
        pos_ref[blk, :] = jnp.where(sel[blk] > 0.0, incl - 1.0, -1.0)
        carry = incl[sb - 1:sb, :]
    cnt_ref[0] = jnp.broadcast_to(carry, (8, LANES)).astype(jnp.int32)


def _router(h, ng, wr, br, tm, sb):
    n, d = h.shape
    nt = n // tm
    return pl.pallas_call(
        functools.partial(_router_kernel, sb=sb),
        grid=(nt,),
        in_specs=[pl.BlockSpec((tm, d), lambda i: (i, 0)),
                  pl.BlockSpec((1, d), lambda i: (0, 0)),
                  pl.BlockSpec((d, LANES), lambda i: (0, 0)),
                  pl.BlockSpec((1, LANES), lambda i: (0, 0))],
        out_specs=[pl.BlockSpec((tm, LANES), lambda i: (i, 0)),
                   pl.BlockSpec((tm, LANES), lambda i: (i, 0)),
                   pl.BlockSpec((1, 8, LANES), lambda i: (i, 0, 0))],
        out_shape=[jax.ShapeDtypeStruct((n, LANES), F32), jax.ShapeDtypeStruct((n, LANES), F32),
                   jax.ShapeDtypeStruct((nt, 8, LANES), jnp.int32)],
        compiler_params=_cparams(("parallel",)),
        name="router_top2",
    )(h, ng, wr, br)


def _moe_kernel(cnt_ref, x_ref, post_ref, pos_ref, comb_ref, wg_ref, wu_ref, wd_ref,
                out_ref, xs_ref, ys_ref, *, sizes, sub):
    i = pl.program_id(0)
    e = pl.program_id(1)
    f = pl.program_id(2)
    nf = pl.num_programs(2)
    cnt = cnt_ref[i * N_EXPERTS + e]
    big = sizes[-1]

    def for_each_chunk(body):
        lo = 0
        for r in sizes:
            @pl.when((cnt > lo) & (cnt <= r))
            def _(r=r):
                body(0, r)
            lo = r

        @pl.when(cnt > big)
        def _():
            def step(c, carry):
                body(pl.multiple_of(c * big, big), big)
                return carry
            lax.fori_loop(0, (cnt + big - 1) // big, step, 0)

    @pl.when((e == 0) & (f == 0))
    def _():
        out_ref[...] = jnp.zeros_like(out_ref)

    @pl.when(f == 0)
    def _():
        prow = post_ref[0, pl.ds(e, 1), :]

        def gather(off, r):
            tgt = (off + lax.broadcasted_iota(jnp.int32, (r, 1), 0)).astype(F32)
            onehot = jnp.where(prow == tgt, 1.0, 0.0).astype(BF16)
            xs_ref[pl.ds(off, r), :] = jnp.dot(onehot, x_ref[...], preferred_element_type=F32).astype(BF16)
            ys_ref[pl.ds(off, r), :] = jnp.zeros((r, D_MODEL), F32)
        for_each_chunk(gather)

    def expert(off, r):
        xc = xs_ref[pl.ds(off, r), :]
        y = ys_ref[pl.ds(off, r), :]
        nsub = wg_ref.shape[2] // sub

        def gate_up(s):
            cols = slice(s * sub, (s + 1) * sub)
            return (jnp.dot(xc, wg_ref[0, :, cols], preferred_element_type=F32),
                    jnp.dot(xc, wu_ref[0, :, cols], preferred_element_type=F32))

        gt, up = gate_up(0)
        for s in range(nsub):
            act = (_silu(gt) * up).astype(BF16)
            if s + 1 < nsub:
                gt, up = gate_up(s + 1)
            y = y + jnp.dot(act, wd_ref[0, s * sub:(s + 1) * sub, :], preferred_element_type=F32)
        ys_ref[pl.ds(off, r), :] = y
    for_each_chunk(expert)

    @pl.when(f == nf - 1)
    def _():
        lane = lax.broadcasted_iota(jnp.int32, (1, LANES), 1)
        pcol = jnp.sum(jnp.where(lane == e, pos_ref[...], 0.0), axis=-1, keepdims=True)
        gcol = jnp.sum(jnp.where(lane == e, comb_ref[...], 0.0), axis=-1, keepdims=True)

        def scatter(off, r):
            tgt = (off + lax.broadcasted_iota(jnp.int32, (1, r), 1)).astype(F32)
            onehot_t = jnp.where(pcol == tgt, 1.0, 0.0).astype(BF16)
            y = ys_ref[pl.ds(off, r), :].astype(BF16)
            out_ref[...] += gcol * jnp.dot(onehot_t, y, preferred_element_type=F32)
        for_each_chunk(scatter)


def _moe(cnt, hn, post, pos, comb, wg, wu, wd, tm, tf, sizes, sub):
    n, d = hn.shape
    nf = wg.shape[2] // tf
    cap = -(-tm // sizes[-1]) * sizes[-1]
    once = pl.Buffered(1)
    row = lambda w: pl.BlockSpec((tm, w), lambda i, e, f, cnt: (i, 0), pipeline_mode=once)
    grid_spec = pltpu.PrefetchScalarGridSpec(
        num_scalar_prefetch=1,
        grid=(n // tm, N_EXPERTS, nf),
        in_specs=[row(d),
                  pl.BlockSpec((1, POST_ROWS, tm), lambda i, e, f, cnt: (i, 0, 0), pipeline_mode=once),
                  row(LANES), row(LANES),
                  pl.BlockSpec((1, d, tf), lambda i, e, f, cnt: (e, 0, f)),
                  pl.BlockSpec((1, d, tf), lambda i, e, f, cnt: (e, 0, f)),
                  pl.BlockSpec((1, tf, d), lambda i, e, f, cnt: (e, f, 0))],
        out_specs=pl.BlockSpec((tm, d), lambda i, e, f, cnt: (i, 0)),
        scratch_shapes=[pltpu.VMEM((cap, d), BF16), pltpu.VMEM((cap, d), F32)],
    )
    return pl.pallas_call(
        functools.partial(_moe_kernel, sizes=sizes, sub=sub),
        grid_spec=grid_spec,
        out_shape=jax.ShapeDtypeStruct((n, d), F32),
        compiler_params=pltpu.CompilerParams(dimension_semantics=("parallel", "arbitrary", "arbitrary"),
                                             vmem_limit_bytes=MOE_VMEM_LIMIT),
        name="expert_swiglu",
    )(cnt, hn, post, pos, comb, wg, wu, wd)


def _add_norm_kernel(h_ref, d_ref, g_ref, h_out_ref, hn_out_ref):
    h = h_ref[...] + d_ref[...]
    h_out_ref[...] = h
    hn_out_ref[...] = _rms(h, g_ref[...]).astype(hn_out_ref.dtype)


def _add_norm(h, delta, g, tm):
    n, d = h.shape
    row = pl.BlockSpec((tm, d), lambda i: (i, 0))
    return pl.pallas_call(
        _add_norm_kernel,
        grid=(n // tm,),
        in_specs=[row, row, pl.BlockSpec((1, d), lambda i: (0, 0))],
        out_specs=[row, row],
        out_shape=[jax.ShapeDtypeStruct((n, d), F32), jax.ShapeDtypeStruct((n, d), BF16)],
        compiler_params=_cparams(("parallel",)),
        name="residual_norm",
    )(h, delta, g)


def _pad_lanes(a, width=LANES):
    return jnp.pad(a, ((0, 0), (0, width - a.shape[-1])))


def kernel(x, meta, norm_mix, w_in, b_in, q_norm, k_norm, w_attn_o, conv_w, conv_b, conv_ln_g, conv_ln_b,
           w_conv_o, pool_w, pool_scale, w_pool_o, w_out, norm_ffn, w_ff_gate, w_ff_up, w_ff_down, w_router,
           b_router, w_e_gate, w_e_up, w_e_down):
    bsz, seq, d = x.shape
    depth = w_in.shape[0]
    length = seq + N_META
    lp = -(-length // SEQ_ALIGN) * SEQ_ALIGN
    n = bsz * lp

    tm = _pick(n, (1536, 768, 512, 256))
    tm_mid = _pick(n, (768, 512, 256))
    tm_ffn = _pick(n, (768, 512, 256))
    tm_moe = _pick(n, (1408, 768, 512, 256))
    sb_router = _pick(tm_moe, (704, 768, 512, 256))
    tn = _pick(Z_W, (1792, 768, 256))
    ts = _pick(lp, (768, 512, 256))
    ts_f = ts
    tq = _pick(lp, (2816, 768, 256))
    ts_att, tk, att_lead = (256, 256, 256)
    tf_dense = _pick(w_ff_gate.shape[2], (256,))
    tf_moe = _pick(w_e_gate.shape[3], (1792, 512, 256))
    sub_moe = _pick(tf_moe, (256,))
    quarter = tm_moe // 4
    moe_sizes = tuple(range(max(quarter - 96, 64) // 64 * 64, quarter + 161, 64))

    h = jnp.concatenate([jnp.broadcast_to(meta[None].astype(x.dtype), (bsz, N_META, d)), x], axis=1)
    h = jnp.pad(h, ((0, 0), (0, lp - length), (0, 0))).reshape(n, d)
    hn = _norm(h, norm_mix[0][None], tm)

    att_scale = HEAD_DIM ** -0.5
    gate_lo = 3 * ATT_W + ATT_HEADS + 2 * CONV_W + POOL_W
    f_lo = 3 * ATT_W
    for l in range(depth):
        wl = w_in[l]
        w_main = jnp.concatenate([wl[:, gate_lo:], wl[:, :f_lo], wl[:, f_lo + ATT_HEADS:gate_lo]], axis=1).astype(BF16)
        bl = b_in[l]
        b_main = jnp.concatenate([bl[gate_lo:], bl[:f_lo], bl[f_lo + ATT_HEADS:gate_lo]])[None]
        w_f = _pad_lanes(wl[:, f_lo:f_lo + ATT_HEADS])
        w_f_hi = w_f.astype(BF16)
        w_f = jnp.concatenate([w_f_hi, (w_f - w_f_hi.astype(F32)).astype(BF16)], axis=1)
        b_f = _pad_lanes(bl[None, f_lo:f_lo + ATT_HEADS])

        z = _inproj(hn, w_main, b_main, tm, tn)
        z3 = z.reshape(bsz, lp, Z_W)
        fcol = _forget(hn.reshape(bsz, lp, d), w_f, b_f, ts_f)
        frow = jnp.transpose(fcol[:, :, :ATT_HEADS], (0, 2, 1))
        qg = jnp.tile(q_norm[l], 2)[None] * att_scale
        kg = jnp.tile(k_norm[l], 2)[None]
        att = _attention(z3, fcol, frow, qg, kg, tq, ts_att, tk, att_lead)

        cw = jnp.pad(conv_w[l], ((0, HALO - CONV_K), (0, 0)))
        pw = jax.scipy.linalg.block_diag(*[pool_w[l, g] for g in range(pool_w.shape[1])]).astype(BF16)
        u, pm = _convpool(z3, cw, conv_b[l][None], conv_ln_g[l][None], conv_ln_b[l][None], pw,
                          pool_scale[l][None], ts)

        h, hn = _merge(att.reshape(n, ATT_W), u.reshape(n, CONV_W), pm.reshape(n, POOL_W), z, h,
                       w_attn_o[l].astype(BF16), w_conv_o[l].astype(BF16), w_pool_o[l].astype(BF16),
                       w_out[l].astype(BF16), norm_ffn[l][None], tm_mid)

        ng_next = norm_mix[min(l + 1, depth - 1)][None]
        i = l // 2
        if l % 2 == 0:
            h, hn = _ffn(hn, h, w_ff_gate[i].astype(BF16), w_ff_up[i].astype(BF16), w_ff_down[i].astype(BF16),
                         ng_next, tm_ffn, tf_dense)
        else:
            comb, pos, cnt = _router(h, norm_ffn[l][None], _pad_lanes(w_router[i]), _pad_lanes(b_router[i][None]),
                                     tm_moe, sb_router)
            post = jnp.transpose(pos[:, :POST_ROWS].reshape(n // tm_moe, tm_moe, POST_ROWS), (0, 2, 1))
            cnt_flat = cnt[:, 0, :N_EXPERTS].reshape(-1)
            delta = _moe(cnt_flat, hn, post, pos, comb, _to_bf16(w_e_gate, i), _to_bf16(w_e_up, i),
                         _to_bf16(w_e_down, i), tm_moe, tf_moe, moe_sizes, sub_moe)
            h, hn = _add_norm(h, delta, ng_next, tm)

    return h.reshape(bsz, lp, d)[:, N_META:length]
```

```python
import functools

import jax
import jax.numpy as jnp
from jax import lax
from jax.experimental import pallas as pl
from jax.experimental.pallas import tpu as pltpu

F32 = jnp.float32
BF16 = jnp.bfloat16
HIGHEST = lax.Precision.HIGHEST

D_MODEL = 1024
N_META = 16
HEAD_DIM = 64
ATT_W = 512
ATT_HEADS = 8
CONV_W = 256
CONV_K = 31
POOL_W = 256
POOL_WINDOWS = (2, 4, 8, 16)
N_EXPERTS = 8
LANES = 128
SUBLANES = 8
HALO = 32
NEG_INF = -1e30
LOG2E = 1.4426950408889634
KNORM_ROWS = 256
EXP_UNDERFLOW = 110.0
SEQ_ALIGN = 256
VMEM_LIMIT = 56 * 1024 * 1024
MOE_VMEM_LIMIT = 60 * 1024 * 1024
POST_ROWS = 32

Z_GATE = 0
Z_Q = 3072
Z_K = 3584
Z_V = 4096
Z_A = 4608
Z_G = 4864
Z_P = 5120
Z_W = 5376


def _pick(n, candidates):
    for c in candidates:
        if n % c == 0:
            return c
    raise ValueError(f"no tile for {n} in {candidates}")


def _cparams(sem):
    return pltpu.CompilerParams(dimension_semantics=sem, vmem_limit_bytes=VMEM_LIMIT)


def _rms(x, g, eps=1e-6):
    return x * lax.rsqrt(jnp.mean(x * x, axis=-1, keepdims=True) + eps) * g


def _sigmoid(x):
    return 0.5 * jnp.tanh(0.5 * x) + 0.5


def _silu(x):
    return x * _sigmoid(x)


def _cast_kernel(w_ref, o_ref):
    o_ref[...] = w_ref[0].astype(o_ref.dtype)


def _to_bf16(w, layer):
    _, s, r, c = w.shape
    rows = _pick(r, (512, 256, 128, 8))
    return pl.pallas_call(
        _cast_kernel,
        grid=(s, r // rows),
        in_specs=[pl.BlockSpec((1, 1, rows, c), lambda i, j: (layer, i, j, 0))],
        out_specs=pl.BlockSpec((1, rows, c), lambda i, j: (i, j, 0)),
        out_shape=jax.ShapeDtypeStruct((s, r, c), BF16),
        compiler_params=_cparams(("parallel", "parallel")),
        name="weights_to_bf16",
    )(w)


def _norm_kernel(h_ref, g_ref, o_ref):
    o_ref[...] = _rms(h_ref[...], g_ref[...]).astype(o_ref.dtype)


def _norm(h, g, tm):
    n, d = h.shape
    return pl.pallas_call(
        _norm_kernel,
        grid=(n // tm,),
        in_specs=[pl.BlockSpec((tm, d), lambda i: (i, 0)), pl.BlockSpec((1, d), lambda i: (0, 0))],
        out_specs=pl.BlockSpec((tm, d), lambda i: (i, 0)),
        out_shape=jax.ShapeDtypeStruct((n, d), BF16),
        compiler_params=_cparams(("parallel",)),
        name="rmsnorm",
    )(h, g)


def _inproj_kernel(x_ref, w_ref, b_ref, o_ref):
    acc = jnp.dot(x_ref[...], w_ref[...], preferred_element_type=F32)
    o_ref[...] = (acc + b_ref[...]).astype(o_ref.dtype)


def _inproj(hn, w, b, tm, tn):
    n, d = hn.shape
    zw = w.shape[1]
    return pl.pallas_call(
        _inproj_kernel,
        grid=(n // tm, zw // tn),
        in_specs=[pl.BlockSpec((tm, d), lambda i, j: (i, 0)),
                  pl.BlockSpec((d, tn), lambda i, j: (0, j)),
                  pl.BlockSpec((1, tn), lambda i, j: (0, j))],
        out_specs=pl.BlockSpec((tm, tn), lambda i, j: (i, j)),
        out_shape=jax.ShapeDtypeStruct((n, zw), BF16),
        compiler_params=_cparams(("parallel", "arbitrary")),
        name="inproj",
    )(hn, w, b)


def _forget_kernel(x_ref, w_ref, b_ref, o_ref, carry_ref):
    @pl.when(pl.program_id(1) == 0)
    def _():
        carry_ref[...] = jnp.zeros_like(carry_ref)

    fw = jnp.dot(x_ref[0], w_ref[...], preferred_element_type=F32)
    f = fw[:, :LANES] + fw[:, LANES:] + b_ref[...]
    ls = jnp.minimum(f, 0.0) - jnp.log(1.0 + jnp.exp(-jnp.abs(f)))
    hi = ls.astype(BF16)
    r1 = ls - hi.astype(F32)
    mid = r1.astype(BF16)
    lo = (r1 - mid.astype(F32)).astype(BF16)
    t = ls.shape[0]
    r = lax.broadcasted_iota(jnp.int32, (t, t), 0)
    c = lax.broadcasted_iota(jnp.int32, (t, t), 1)
    tril = jnp.where(c <= r, 1.0, 0.0).astype(BF16)
    parts = jnp.dot(tril, jnp.concatenate([hi, mid, lo], axis=-1), preferred_element_type=F32)
    cs = (parts[:, :LANES] + parts[:, LANES:2 * LANES]) + parts[:, 2 * LANES:] + carry_ref[...]
    o_ref[0] = cs
    carry_ref[...] = cs[t - 1:t, :]


def _forget(hn3, w_f, b_f, ts):
    b, lp, d = hn3.shape
    return pl.pallas_call(
        _forget_kernel,
        grid=(b, lp // ts),
        in_specs=[pl.BlockSpec((1, ts, d), lambda bi, t: (bi, t, 0)),
                  pl.BlockSpec((d, 2 * LANES), lambda bi, t: (0, 0)),
                  pl.BlockSpec((1, LANES), lambda bi, t: (0, 0))],
        out_specs=pl.BlockSpec((1, ts, LANES), lambda bi, t: (bi, t, 0)),
        out_shape=jax.ShapeDtypeStruct((b, lp, LANES), F32),
        scratch_shapes=[pltpu.VMEM((1, LANES), F32)],
        compiler_params=_cparams(("parallel", "arbitrary")),
        name="forget_cumsum",
    )(hn3, w_f, b_f)


def _pair_rms(x, first_head, gain, eps=1e-6):
    sq = x * x
    s0 = jnp.sum(jnp.where(first_head, sq, 0.0), axis=-1, keepdims=True)
    s1 = jnp.sum(jnp.where(first_head, 0.0, sq), axis=-1, keepdims=True)
    ms = jnp.where(first_head, s0, s1) * (1.0 / HEAD_DIM)
    return x * lax.rsqrt(ms + eps) * gain


def _attn_kernel(lo_ref, q_ref, k_ref, v_ref, fc_ref, fr_ref, qg_ref, kg_ref, o_ref, kn_ref, *,
                 tq, ts, tk, lead):
    bi = pl.program_id(0)
    p = pl.program_id(1)
    i = pl.program_id(2)
    nq = pl.num_programs(2)
    lp = k_ref.shape[1]
    lane = lax.broadcasted_iota(jnp.int32, (1, LANES), 1)
    first_head = lane < HEAD_DIM

    @pl.when(i == 0)
    def _():
        def body(c, carry):
            off = pl.multiple_of(c * KNORM_ROWS, KNORM_ROWS)
            kk = k_ref[0, pl.ds(off, KNORM_ROWS), :].astype(F32)
            kn_ref[pl.ds(off, KNORM_ROWS), :] = _pair_rms(kk, first_head, kg_ref[...]).astype(BF16)
            return carry
        lax.fori_loop(0, lp // KNORM_ROWS, body, 0)

    qn = _pair_rms(q_ref[0].astype(F32), first_head, qg_ref[...])
    fc = fc_ref[0]

    heads = [2 * p, 2 * p + 1]
    qhs = [jnp.where(first_head, qn, 0.0).astype(BF16), jnp.where(first_head, 0.0, qn).astype(BF16)]
    fts = [jnp.sum(jnp.where(lane == hd, fc, 0.0), axis=-1, keepdims=True) * LOG2E for hd in heads]

    def update(rows, off, width, carry, hh, row_pos=None, own_mask=None):
        m, l, acc = carry
        ks = kn_ref[pl.ds(off, width), :]
        s = lax.dot_general(qhs[hh][rows], ks, (((1,), (1,)), ((), ())), preferred_element_type=F32)
        f_all = fr_ref[0, :, pl.ds(off, width)]
        head_row = lax.broadcasted_iota(jnp.int32, (ATT_HEADS, 1), 0) == heads[hh]
        fs = jnp.sum(jnp.where(head_row, f_all, 0.0), axis=0, keepdims=True) * LOG2E
        s = s + (fts[hh][rows] - fs)
        if own_mask is not None:
            s = jnp.concatenate([s[:, :lead], jnp.where(own_mask, s[:, lead:], NEG_INF)], axis=1)
        elif row_pos is not None:
            col_pos = off + lax.broadcasted_iota(jnp.int32, (1, width), 1)
            s = jnp.where(col_pos <= row_pos, s, NEG_INF)
        m_new = jnp.maximum(m, jnp.max(s, axis=-1, keepdims=True))
        alpha = jnp.exp2(m - m_new)
        pm = jnp.exp2(s - m_new)
        l = alpha * l + jnp.sum(pm, axis=-1, keepdims=True)
        vs = v_ref[0, pl.ds(off, width), :]
        acc = alpha * acc + jnp.dot(pm.astype(BF16), vs, preferred_element_type=F32)
        return m_new, l, acc

    nsub = tq // ts
    init = (jnp.full((ts, 1), NEG_INF, F32), jnp.zeros((ts, 1), F32), jnp.zeros((ts, LANES), F32))
    carries, wins = [], []
    for u in range(nsub):
        rows = slice(u * ts, (u + 1) * ts)
        g = i * nsub + u
        win = jnp.maximum(g * (ts // tk) - lead // tk, 0)
        firsts = [lo_ref[(bi * ATT_HEADS + hd) * (nq * nsub) + g] for hd in heads]

        def far(j, carry, rows=rows):
            off = pl.multiple_of(j * tk, tk)
            return tuple(update(rows, off, tk, carry[hh], hh) for hh in range(2))

        carries.append(lax.fori_loop(jnp.minimum(firsts[0], firsts[1]), win, far, (init, init)))
        wins.append(win)
    lower_tri = (lax.broadcasted_iota(jnp.int32, (ts, ts), 1) <= lax.broadcasted_iota(jnp.int32, (ts, ts), 0))
    for u in range(nsub):
        rows = slice(u * ts, (u + 1) * ts)
        off = pl.multiple_of(wins[u] * tk, tk)
        if u == 0:
            mask = dict(row_pos=i * tq + lax.broadcasted_iota(jnp.int32, (ts, 1), 0))
        else:
            mask = dict(own_mask=lower_tri)
        outs = []
        for hh in range(2):
            m, l, acc = update(rows, off, lead + ts, carries[u][hh], hh, **mask)
            outs.append(acc / l)
        o_ref[0, rows, :] = jnp.where(first_head, outs[0], outs[1]).astype(o_ref.dtype)


def _first_live_chunk(frow, qg, kg, tq, tk):
    b, nh, lp = frow.shape
    qk_bound = 1.02 * HEAD_DIM * jnp.max(jnp.abs(qg)) * jnp.max(jnp.abs(kg))
    f_first = frow[:, :, 0::tq]
    f_last = frow[:, :, tk - 1::tk]
    dead = (f_first[:, :, :, None] - f_last[:, :, None, :] + 2.0 * qk_bound) < -EXP_UNDERFLOW
    n_dead = jnp.sum(dead.astype(jnp.int32), axis=-1)
    n_full = (jnp.arange(lp // tq, dtype=jnp.int32) * tq) // tk
    return jnp.minimum(n_dead, n_full[None, None, :]).reshape(-1)


def _attention(z3, fcol, frow, qg, kg, tq, ts, tk, lead):
    b, lp, _ = z3.shape
    npairs = ATT_HEADS // 2
    qb, kb, vb = Z_Q // LANES, Z_K // LANES, Z_V // LANES
    assert tq % ts == 0 and ts % tk == 0 and lead % tk == 0 and lp >= lead + ts and lp % KNORM_ROWS == 0
    first = _first_live_chunk(frow, qg, kg, ts, tk)
    grid_spec = pltpu.PrefetchScalarGridSpec(
        num_scalar_prefetch=1,
        grid=(b, npairs, lp // tq),
        in_specs=[pl.BlockSpec((1, tq, LANES), lambda bi, p, i, lo: (bi, i, qb + p)),
                  pl.BlockSpec((1, lp, LANES), lambda bi, p, i, lo: (bi, 0, kb + p)),
                  pl.BlockSpec((1, lp, LANES), lambda bi, p, i, lo: (bi, 0, vb + p)),
                  pl.BlockSpec((1, tq, LANES), lambda bi, p, i, lo: (bi, i, 0)),
                  pl.BlockSpec((1, ATT_HEADS, lp), lambda bi, p, i, lo: (bi, 0, 0)),
                  pl.BlockSpec((1, LANES), lambda bi, p, i, lo: (0, 0)),
                  pl.BlockSpec((1, LANES), lambda bi, p, i, lo: (0, 0))],
        out_specs=pl.BlockSpec((1, tq, LANES), lambda bi, p, i, lo: (bi, i, p)),
        scratch_shapes=[pltpu.VMEM((lp, LANES), BF16)],
    )
    return pl.pallas_call(
        functools.partial(_attn_kernel, tq=tq, ts=ts, tk=tk, lead=lead),
        grid_spec=grid_spec,
        out_shape=jax.ShapeDtypeStruct((b, lp, ATT_W), BF16),
        compiler_params=_cparams(("parallel", "parallel", "arbitrary")),
        name="fox_attention",
    )(first, z3, z3, z3, fcol, frow, qg * LOG2E, kg)


def _convpool_kernel(a_ref, g_ref, p_ref, ah_ref, gh_ref, ph_ref, cw_ref, cb_ref, lg_ref, lb_ref,
                     pw_ref, ps_ref, u_ref, pm_ref, ext_ref, pext_ref, *, ts):
    i = pl.program_id(1)
    has_prev = i > 0

    def stage(ref, halo_rows, rows):
        ref[0, 0:HALO, :] = halo_rows
        ref[0, HALO:HALO + ts, :] = rows
        base = ref[0]
        for r in range(1, SUBLANES):
            ref[r, r:HALO + ts, :] = base[0:HALO + ts - r, :]

    def behind(ref, back):
        start = HALO - back // SUBLANES * SUBLANES
        return ref[back % SUBLANES, start:start + ts, :]

    u = a_ref[0].astype(F32) * _sigmoid(g_ref[0].astype(F32))
    uh = ah_ref[0].astype(F32) * _sigmoid(gh_ref[0].astype(F32))
    stage(ext_ref, jnp.where(has_prev, uh, 0.0), u)
    acc = jnp.zeros((ts, CONV_W), F32) + cb_ref[...]
    for j in range(CONV_K):
        acc = acc + cw_ref[j:j + 1, :] * behind(ext_ref, CONV_K - 1 - j)
    mu = jnp.mean(acc, axis=-1, keepdims=True)
    cen = acc - mu
    var = jnp.mean(cen * cen, axis=-1, keepdims=True)
    y = cen * lax.rsqrt(var + 1e-5) * lg_ref[...] + lb_ref[...]
    u_ref[0] = _silu(y).astype(u_ref.dtype)

    x = p_ref[0].astype(F32)
    stage(pext_ref, jnp.where(has_prev, ph_ref[0].astype(F32), 0.0), x)
    pos1 = (i * ts + 1 + lax.broadcasted_iota(jnp.int32, (ts, 1), 0)).astype(F32)
    lane = lax.broadcasted_iota(jnp.int32, (1, POOL_W), 1)
    group_w = POOL_W // len(POOL_WINDOWS)
    run = x
    pooled = jnp.zeros((ts, POOL_W), F32)
    for k in range(1, max(POOL_WINDOWS)):
        run = run + behind(pext_ref, k)
        if (k + 1) in POOL_WINDOWS:
            gi = POOL_WINDOWS.index(k + 1)
            mean = run / jnp.minimum(pos1, float(k + 1))
            in_group = (lane >= gi * group_w) & (lane < (gi + 1) * group_w)
            pooled = jnp.where(in_group, mean, pooled)
    pm = (pooled - x).astype(BF16)
    lin = jnp.dot(pm, pw_ref[...], preferred_element_type=F32) * ps_ref[...]
    pm_ref[0] = lin.astype(pm_ref.dtype)


def _convpool(z3, cw, cb, lg, lb, pw, ps, ts):
    b, lp, _ = z3.shape
    ab, gb, pb = Z_A // CONV_W, Z_G // CONV_W, Z_P // POOL_W
    hpt = ts // HALO
    main = lambda blk: pl.BlockSpec((1, ts, CONV_W), lambda bi, i: (bi, i, blk))
    halo = lambda blk: pl.BlockSpec((1, HALO, CONV_W), lambda bi, i: (bi, jnp.maximum(i * hpt - 1, 0), blk))
    const = lambda shape: pl.BlockSpec(shape, lambda bi, i: (0, 0))
    out = pl.BlockSpec((1, ts, CONV_W), lambda bi, i: (bi, i, 0))
    return pl.pallas_call(
        functools.partial(_convpool_kernel, ts=ts),
        grid=(b, lp // ts),
        in_specs=[main(ab), main(gb), main(pb), halo(ab), halo(gb), halo(pb),
                  const((HALO, CONV_W)), const((1, CONV_W)), const((1, CONV_W)), const((1, CONV_W)),
                  const((POOL_W, POOL_W)), const((1, POOL_W))],
        out_specs=[out, out],
        out_shape=[jax.ShapeDtypeStruct((b, lp, CONV_W), BF16), jax.ShapeDtypeStruct((b, lp, POOL_W), BF16)],
        scratch_shapes=[pltpu.VMEM((SUBLANES, HALO + ts, CONV_W), F32),
                        pltpu.VMEM((SUBLANES, HALO + ts, POOL_W), F32)],
        compiler_params=_cparams(("parallel", "arbitrary")),
        name="conv_pool",
    )(z3, z3, z3, z3, z3, z3, cw, cb, lg, lb, pw, ps)


def _merge_kernel(att_ref, u_ref, pm_ref, g0_ref, g1_ref, g2_ref, h_ref, wa_ref, wc_ref, wp_ref, wo_ref,
                  ng_ref, h_out_ref, hn_out_ref):
    ya = jnp.dot(att_ref[...], wa_ref[...], preferred_element_type=F32)
    yc = jnp.dot(u_ref[...], wc_ref[...], preferred_element_type=F32)
    yp = jnp.dot(pm_ref[...], wp_ref[...], preferred_element_type=F32)
    m = (_sigmoid(g0_ref[...].astype(F32)) * ya + _sigmoid(g1_ref[...].astype(F32)) * yc
         + _sigmoid(g2_ref[...].astype(F32)) * yp)
    h = h_ref[...] + jnp.dot(m.astype(BF16), wo_ref[...], preferred_element_type=F32)
    h_out_ref[...] = h
    hn_out_ref[...] = _rms(h, ng_ref[...]).astype(hn_out_ref.dtype)


def _merge(att, u, pm, z, h, wa, wc, wp, wo, ng, tm):
    n, d = h.shape
    row = lambda w, blk=0: pl.BlockSpec((tm, w), lambda i: (i, blk))
    const = lambda shape: pl.BlockSpec(shape, lambda i: (0, 0))
    return pl.pallas_call(
        _merge_kernel,
        grid=(n // tm,),
        in_specs=[row(ATT_W), row(CONV_W), row(POOL_W), row(d, 0), row(d, 1), row(d, 2), row(d),
                  const(wa.shape), const(wc.shape), const(wp.shape), const(wo.shape), const((1, d))],
        out_specs=[row(d), row(d)],
        out_shape=[jax.ShapeDtypeStruct((n, d), F32), jax.ShapeDtypeStruct((n, d), BF16)],
        compiler_params=_cparams(("parallel",)),
        name="merge_outproj",
    )(att, u, pm, z, z, z, h, wa, wc, wp, wo, ng)


def _ffn_kernel(x_ref, h_ref, wg_ref, wu_ref, wd_ref, ng_ref, h_out_ref, hn_out_ref, *, tf):
    x = x_ref[...]
    h = h_ref[...]
    nc = wg_ref.shape[1] // tf

    def gate_up(c):
        cols = slice(c * tf, (c + 1) * tf)
        return (jnp.dot(x, wg_ref[:, cols], preferred_element_type=F32),
                jnp.dot(x, wu_ref[:, cols], preferred_element_type=F32))

    gt, up = gate_up(0)
    for c in range(nc):
        act = (_silu(gt) * up).astype(BF16)
        if c + 1 < nc:
            gt, up = gate_up(c + 1)
        h = h + jnp.dot(act, wd_ref[c * tf:(c + 1) * tf, :], preferred_element_type=F32)
    h_out_ref[...] = h
    hn_out_ref[...] = _rms(h, ng_ref[...]).astype(hn_out_ref.dtype)


def _ffn(hn, h, wg, wu, wd, ng, tm, tf):
    n, d = h.shape
    row = pl.BlockSpec((tm, d), lambda i: (i, 0))
    const = lambda shape: pl.BlockSpec(shape, lambda i: (0, 0), pipeline_mode=pl.Buffered(1))
    return pl.pallas_call(
        functools.partial(_ffn_kernel, tf=tf),
        grid=(n // tm,),
        in_specs=[row, row, const(wg.shape), const(wu.shape), const(wd.shape), const((1, d))],
        out_specs=[row, row],
        out_shape=[jax.ShapeDtypeStruct((n, d), F32), jax.ShapeDtypeStruct((n, d), BF16)],
        compiler_params=_cparams(("parallel",)),
        name="dense_swiglu",
    )(hn, h, wg, wu, wd, ng)


def _router_kernel(h_ref, ng_ref, wr_ref, br_ref, comb_ref, pos_ref, cnt_ref, *, sb):
    tm = h_ref.shape[0]
    hn = _rms(h_ref[...], ng_ref[...])
    logits = jnp.dot(hn, wr_ref[...], precision=HIGHEST, preferred_element_type=F32) + br_ref[...]
    lane = lax.broadcasted_iota(jnp.int32, (1, LANES), 1).astype(F32)
    lg = jnp.where(lane < N_EXPERTS, logits, -jnp.inf)
    m1 = jnp.max(lg, axis=-1, keepdims=True)
    i1 = jnp.min(jnp.where(lg == m1, lane, float(LANES)), axis=-1, keepdims=True)
    sel1 = lane == i1
    lg2 = jnp.where(sel1, -jnp.inf, lg)
    m2 = jnp.max(lg2, axis=-1, keepdims=True)
    i2 = jnp.min(jnp.where(lg2 == m2, lane, float(LANES)), axis=-1, keepdims=True)
    sel2 = lane == i2
    e = jnp.exp(m2 - m1)
    g1 = 1.0 / (1.0 + e)
    comb_ref[...] = jnp.where(sel1, g1, 0.0) + jnp.where(sel2, e * g1, 0.0)
    sel = jnp.where(sel1 | sel2, 1.0, 0.0)
    r = lax.broadcasted_iota(jnp.int32, (sb, sb), 0)
    c = lax.broadcasted_iota(jnp.int32, (sb, sb), 1)
    tril = jnp.where(c <= r, 1.0, 0.0).astype(BF16)
    carry = jnp.zeros((1, LANES), F32)
    for s in range(tm // sb):
        blk = slice(s * sb, (s + 1) * sb)
        incl = jnp.dot(tril, sel[blk].astype(BF16), preferred_element_type=F32) + carry
        pos_ref[blk, :] = jnp.where(sel[blk] > 0.0, incl - 1.0, -1.0)
        carry = incl[sb - 1:sb, :]
    cnt_ref[0] = jnp.broadcast_to(carry, (8, LANES)).astype(jnp.int32)


def _router(h, ng, wr, br, tm, sb):
    n, d = h.shape
    nt = n // tm
    return pl.pallas_call(
        functools.partial(_router_kernel, sb=sb),
        grid=(nt,),
        in_specs=[pl.BlockSpec((tm, d), lambda i: (i, 0)),
                  pl.BlockSpec((1, d), lambda i: (0, 0)),
                  pl.BlockSpec((d, LANES), lambda i: (0, 0)),
                  pl.BlockSpec((1, LANES), lambda i: (0, 0))],
        out_specs=[pl.BlockSpec((tm, LANES), lambda i: (i, 0)),
                   pl.BlockSpec((tm, LANES), lambda i: (i, 0)),
                   pl.BlockSpec((1, 8, LANES), lambda i: (i, 0, 0))],
        out_shape=[jax.ShapeDtypeStruct((n, LANES), F32), jax.ShapeDtypeStruct((n, LANES), F32),
                   jax.ShapeDtypeStruct((nt, 8, LANES), jnp.int32)],
        compiler_params=_cparams(("parallel",)),
        name="router_top2",
    )(h, ng, wr, br)


def _moe_kernel(cnt_ref, x_ref, post_ref, pos_ref, comb_ref, wg_ref, wu_ref, wd_ref,
                out_ref, xs_ref, ys_ref, *, sizes, sub):
    i = pl.program_id(0)
    e = pl.program_id(1)
    f = pl.program_id(2)
    nf = pl.num_programs(2)
    cnt = cnt_ref[i * N_EXPERTS + e]
    big = sizes[-1]

    def for_each_chunk(body):
        lo = 0
        for r in sizes:
            @pl.when((cnt > lo) & (cnt <= r))
            def _(r=r):
                body(0, r)
            lo = r

        @pl.when(cnt > big)
        def _():
            def step(c, carry):
                body(pl.multiple_of(c * big, big), big)
                return carry
            lax.fori_loop(0, (cnt + big - 1) // big, step, 0)

    @pl.when((e == 0) & (f == 0))
    def _():
        out_ref[...] = jnp.zeros_like(out_ref)

    @pl.when(f == 0)
    def _():
        prow = post_ref[0, pl.ds(e, 1), :]

        def gather(off, r):
            tgt = (off + lax.broadcasted_iota(jnp.int32, (r, 1), 0)).astype(F32)
            onehot = jnp.where(prow == tgt, 1.0, 0.0).astype(BF16)
            xs_ref[pl.ds(off, r), :] = jnp.dot(onehot, x_ref[...], preferred_element_type=F32).astype(BF16)
            ys_ref[pl.ds(off, r), :] = jnp.zeros((r, D_MODEL), F32)
        for_each_chunk(gather)

    def expert(off, r):
        xc = xs_ref[pl.ds(off, r), :]
        y = ys_ref[pl.ds(off, r), :]
        nsub = wg_ref.shape[2] // sub

        def gate_up(s):
            cols = slice(s * sub, (s + 1) * sub)
            return (jnp.dot(xc, wg_ref[0, :, cols], preferred_element_type=F32),
                    jnp.dot(xc, wu_ref[0, :, cols], preferred_element_type=F32))

        gt, up = gate_up(0)
        for s in range(nsub):
            act = (_silu(gt) * up).astype(BF16)
            if s + 1 < nsub:
                gt, up = gate_up(s + 1)
            y = y + jnp.dot(act, wd_ref[0, s * sub:(s + 1) * sub, :], preferred_element_type=F32)
        ys_ref[pl.ds(off, r), :] = y
    for_each_chunk(expert)

    @pl.when(f == nf - 1)
    def _():
        lane = lax.broadcasted_iota(jnp.int32, (1, LANES), 1)
        pcol = jnp.sum(jnp.where(lane == e, pos_ref[...], 0.0), axis=-1, keepdims=True)
        gcol = jnp.sum(jnp.where(lane == e, comb_ref[...], 0.0), axis=-1, keepdims=True)

        def scatter(off, r):
            tgt = (off + lax.broadcasted_iota(jnp.int32, (1, r), 1)).astype(F32)
            onehot_t = jnp.where(pcol == tgt, 1.0, 0.0).astype(BF16)
            y = ys_ref[pl.ds(off, r), :].astype(BF16)
            out_ref[...] += gcol * jnp.dot(onehot_t, y, preferred_element_type=F32)
        for_each_chunk(scatter)


def _moe(cnt, hn, post, pos, comb, wg, wu, wd, tm, tf, sizes, sub):
    n, d = hn.shape
    nf = wg.shape[2] // tf
    cap = -(-tm // sizes[-1]) * sizes[-1]
    once = pl.Buffered(1)
    row = lambda w: pl.BlockSpec((tm, w), lambda i, e, f, cnt: (i, 0), pipeline_mode=once)
    grid_spec = pltpu.PrefetchScalarGridSpec(
        num_scalar_prefetch=1,
        grid=(n // tm, N_EXPERTS, nf),
        in_specs=[row(d),
                  pl.BlockSpec((1, POST_ROWS, tm), lambda i, e, f, cnt: (i, 0, 0), pipeline_mode=once),
                  row(LANES), row(LANES),
                  pl.BlockSpec((1, d, tf), lambda i, e, f, cnt: (e, 0, f)),
                  pl.BlockSpec((1, d, tf), lambda i, e, f, cnt: (e, 0, f)),
                  pl.BlockSpec((1, tf, d), lambda i, e, f, cnt: (e, f, 0))],
        out_specs=pl.BlockSpec((tm, d), lambda i, e, f, cnt: (i, 0)),
        scratch_shapes=[pltpu.VMEM((cap, d), BF16), pltpu.VMEM((cap, d), F32)],
    )
    return pl.pallas_call(
        functools.partial(_moe_kernel, sizes=sizes, sub=sub),
        grid_spec=grid_spec,
        out_shape=jax.ShapeDtypeStruct((n, d), F32),
        compiler_params=pltpu.CompilerParams(dimension_semantics=("parallel", "arbitrary", "arbitrary"),
                                             vmem_limit_bytes=MOE_VMEM_LIMIT),
        name="expert_swiglu",
    )(cnt, hn, post, pos, comb, wg, wu, wd)


def _add_norm_kernel(h_ref, d_ref, g_ref, h_out_ref, hn_out_ref):
    h = h_ref[...] + d_ref[...]
    h_out_ref[...] = h
    hn_out_ref[...] = _rms(h, g_ref[...]).astype(hn_out_ref.dtype)


def _add_norm(h, delta, g, tm):
    n, d = h.shape
    row = pl.BlockSpec((tm, d), lambda i: (i, 0))
    return pl.pallas_call(
        _add_norm_kernel,
        grid=(n // tm,),
        in_specs=[row, row, pl.BlockSpec((1, d), lambda i: (0, 0))],
        out_specs=[row, row],
        out_shape=[jax.ShapeDtypeStruct((n, d), F32), jax.ShapeDtypeStruct((n, d), BF16)],
        compiler_params=_cparams(("parallel",)),
        name="residual_norm",
    )(h, delta, g)


def _pad_lanes(a, width=LANES):
    return jnp.pad(a, ((0, 0), (0, width - a.shape[-1])))


def kernel(x, meta, norm_mix, w_in, b_in, q_norm, k_norm, w_attn_o, conv_w, conv_b, conv_ln_g, conv_ln_b,
           w_conv_o, pool_w, pool_scale, w_pool_o, w_out, norm_ffn, w_ff_gate, w_ff_up, w_ff_down, w_router,
           b_router, w_e_gate, w_e_up, w_e_down):
    bsz, seq, d = x.shape
    depth = w_in.shape[0]
    length = seq + N_META
    lp = -(-length // SEQ_ALIGN) * SEQ_ALIGN
    n = bsz * lp

    tm = _pick(n, (1536, 768, 512, 256))
    tm_mid = _pick(n, (768, 512, 256))
    tm_ffn = _pick(n, (768, 512, 256))
    tm_moe = _pick(n, (1408, 768, 512, 256))
    sb_router = _pick(tm_moe, (704, 768, 512, 256))
    tn = _pick(Z_W, (1792, 768, 256))
    ts = _pick(lp, (768, 512, 256))
    ts_f = ts
    tq = _pick(lp, (2816, 768, 256))
    ts_att, tk, att_lead = (256, 256, 256)
    tf_dense = _pick(w_ff_gate.shape[2], (256,))
    tf_moe = _pick(w_e_gate.shape[3], (1792, 512, 256))
    sub_moe = _pick(tf_moe, (256,))
    quarter = tm_moe // 4
    moe_sizes = tuple(range(max(quarter - 96, 64) // 64 * 64, quarter + 161, 64))

    h = jnp.concatenate([jnp.broadcast_to(meta[None].astype(x.dtype), (bsz, N_META, d)), x], axis=1)
    h = jnp.pad(h, ((0, 0), (0, lp - length), (0, 0))).reshape(n, d)
    hn = _norm(h, norm_mix[0][None], tm)

    att_scale = HEAD_DIM ** -0.5
    gate_lo = 3 * ATT_W + ATT_HEADS + 2 * CONV_W + POOL_W
    f_lo = 3 * ATT_W
    for l in range(depth):
        wl = w_in[l]
        w_main = jnp.concatenate([wl[:, gate_lo:], wl[:, :f_lo], wl[:, f_lo + ATT_HEADS:gate_lo]], axis=1).astype(BF16)
        bl = b_in[l]
        b_main = jnp.concatenate([bl[gate_lo:], bl[:f_lo], bl[f_lo + ATT_HEADS:gate_lo]])[None]
        w_f = _pad_lanes(wl[:, f_lo:f_lo + ATT_HEADS])
        w_f_hi = w_f.astype(BF16)
        w_f = jnp.concatenate([w_f_hi, (w_f - w_f_hi.astype(F32)).astype(BF16)], axis=1)
        b_f = _pad_lanes(bl[None, f_lo:f_lo + ATT_HEADS])

        z = _inproj(hn, w_main, b_main, tm, tn)
        z3 = z.reshape(bsz, lp, Z_W)
        fcol = _forget(hn.reshape(bsz, lp, d), w_f, b_f, ts_f)
        frow = jnp.transpose(fcol[:, :, :ATT_HEADS], (0, 2, 1))
        qg = jnp.tile(q_norm[l], 2)[None] * att_scale
        kg = jnp.tile(k_norm[l], 2)[None]
        att = _attention(z3, fcol, frow, qg, kg, tq, ts_att, tk, att_lead)

        cw = jnp.pad(conv_w[l], ((0, HALO - CONV_K), (0, 0)))
        pw = jax.scipy.linalg.block_diag(*[pool_w[l, g] for g in range(pool_w.shape[1])]).astype(BF16)
        u, pm = _convpool(z3, cw, conv_b[l][None], conv_ln_g[l][None], conv_ln_b[l][None], pw,
                          pool_scale[l][None], ts)

        h, hn = _merge(att.reshape(n, ATT_W), u.reshape(n, CONV_W), pm.reshape(n, POOL_W), z, h,
                       w_attn_o[l].astype(BF16), w_conv_o[l].astype(BF16), w_pool_o[l].astype(BF16),
                       w_out[l].astype(BF16), norm_ffn[l][None], tm_mid)

        ng_next = norm_mix[min(l + 1, depth - 1)][None]
        i = l // 2
        if l % 2 == 0:
            h, hn = _ffn(hn, h, w_ff_gate[i].astype(BF16), w_ff_up[i].astype(BF16), w_ff_down[i].astype(BF16),
                         ng_next, tm_ffn, tf_dense)
        else:
            comb, pos, cnt = _router(h, norm_ffn[l][None], _pad_lanes(w_router[i]), _pad_lanes(b_router[i][None]),
                                     tm_moe, sb_router)
            post = jnp.transpose(pos[:, :POST_ROWS].reshape(n // tm_moe, tm_moe, POST_ROWS), (0, 2, 1))
            cnt_flat = cnt[:, 0, :N_EXPERTS].reshape(-1)
            delta = _moe(cnt_flat, hn, post, pos, comb, _to_bf16(w_e_gate, i), _to_bf16(w_e_up, i),
                         _to_bf16(w_e_down, i), tm_moe, tf_moe, moe_sizes, sub_moe)
            h, hn = _add_norm(h, delta, ng_next, tm)

    return h.reshape(bsz, lp, d)[:, N_META:length]
```

```python
import functools

import jax
import jax.numpy as jnp
from jax import lax
from jax.experimental import pallas as pl
from jax.experimental.pallas import tpu as pltpu

F32 = jnp.float32
BF16 = jnp.bfloat16

D_MODEL = 1024
N_META = 16
HEAD_DIM = 64
ATT_W = 512
ATT_HEADS = 8
CONV_W = 256
CONV_K = 31
POOL_W = 256
POOL_WINDOWS = (2, 4, 8, 16)
N_EXPERTS = 8
LANES = 128
SUBLANES = 8
HALO = 32
NEG_INF = -1e30
LOG2E = 1.4426950408889634
OUT_ROWS = 256
KNORM_ROWS = 256
EXP_UNDERFLOW = 110.0
SEQ_ALIGN = 256
VMEM_LIMIT = 56 * 1024 * 1024
MOE_VMEM_LIMIT = 60 * 1024 * 1024
POST_ROWS = 32

Z_GATE = 0
Z_Q = 3072
Z_K = 3584
Z_V = 4096
Z_A = 4608
Z_G = 4864
Z_P = 5120
Z_W = 5376


def _pick(n, candidates):
    for c in candidates:
        if n % c == 0:
            return c
    raise ValueError(f"no tile for {n} in {candidates}")


def _cparams(sem):
    return pltpu.CompilerParams(dimension_semantics=sem, vmem_limit_bytes=VMEM_LIMIT)


def _rms(x, g, eps=1e-6):
    return x * lax.rsqrt(jnp.mean(x * x, axis=-1, keepdims=True) + eps) * g


def _sigmoid(x):
    return 0.5 * jnp.tanh(0.5 * x) + 0.5


def _silu(x):
    return x * _sigmoid(x)


def _cast_kernel(w_ref, o_ref):
    o_ref[...] = w_ref[0].astype(o_ref.dtype)


def _to_bf16(w, layer):
    _, s, r, c = w.shape
    rows = _pick(r, (512, 256, 128, 8))
    return pl.pallas_call(
        _cast_kernel,
        grid=(s, r // rows),
        in_specs=[pl.BlockSpec((1, 1, rows, c), lambda i, j: (layer, i, j, 0))],
        out_specs=pl.BlockSpec((1, rows, c), lambda i, j: (i, j, 0)),
        out_shape=jax.ShapeDtypeStruct((s, r, c), BF16),
        compiler_params=_cparams(("parallel", "parallel")),
        name="weights_to_bf16",
    )(w)


def _norm_kernel(h_ref, g_ref, o_ref):
    o_ref[...] = _rms(h_ref[...], g_ref[...]).astype(o_ref.dtype)


def _norm(h, g, tm):
    n, d = h.shape
    return pl.pallas_call(
        _norm_kernel,
        grid=(n // tm,),
        in_specs=[pl.BlockSpec((tm, d), lambda i: (i, 0)), pl.BlockSpec((1, d), lambda i: (0, 0))],
        out_specs=pl.BlockSpec((tm, d), lambda i: (i, 0)),
        out_shape=jax.ShapeDtypeStruct((n, d), BF16),
        compiler_params=_cparams(("parallel",)),
        name="rmsnorm",
    )(h, g)


def _inproj_kernel(x_ref, w_ref, b_ref, o_ref):
    acc = jnp.dot(x_ref[...], w_ref[...], preferred_element_type=F32)
    o_ref[...] = (acc + b_ref[...]).astype(o_ref.dtype)


def _inproj(hn, w, b, tm, tn):
    n, d = hn.shape
    zw = w.shape[1]
    return pl.pallas_call(
        _inproj_kernel,
        grid=(n // tm, zw // tn),
        in_specs=[pl.BlockSpec((tm, d), lambda i, j: (i, 0)),
                  pl.BlockSpec((d, tn), lambda i, j: (0, j)),
                  pl.BlockSpec((1, tn), lambda i, j: (0, j))],
        out_specs=pl.BlockSpec((tm, tn), lambda i, j: (i, j)),
        out_shape=jax.ShapeDtypeStruct((n, zw), BF16),
        compiler_params=_cparams(("parallel", "arbitrary")),
        name="inproj",
    )(hn, w, b)


def _forget_kernel(x_ref, w_ref, b_ref, o_ref, carry_ref):
    @pl.when(pl.program_id(1) == 0)
    def _():
        carry_ref[...] = jnp.zeros_like(carry_ref)

    fw = jnp.dot(x_ref[0], w_ref[...], preferred_element_type=F32)
    f = fw[:, :LANES] + fw[:, LANES:] + b_ref[...]
    ls = jnp.minimum(f, 0.0) - jnp.log(1.0 + jnp.exp(-jnp.abs(f)))
    hi = ls.astype(BF16)
    r1 = ls - hi.astype(F32)
    mid = r1.astype(BF16)
    lo = (r1 - mid.astype(F32)).astype(BF16)
    t = ls.shape[0]
    r = lax.broadcasted_iota(jnp.int32, (t, t), 0)
    c = lax.broadcasted_iota(jnp.int32, (t, t), 1)
    tril = jnp.where(c <= r, 1.0, 0.0).astype(BF16)
    parts = jnp.dot(tril, jnp.concatenate([hi, mid, lo], axis=-1), preferred_element_type=F32)
    cs = (parts[:, :LANES] + parts[:, LANES:2 * LANES]) + parts[:, 2 * LANES:] + carry_ref[...]
    o_ref[0] = cs
    carry_ref[...] = cs[t - 1:t, :]


def _forget(hn3, w_f, b_f, ts):
    b, lp, d = hn3.shape
    return pl.pallas_call(
        _forget_kernel,
        grid=(b, lp // ts),
        in_specs=[pl.BlockSpec((1, ts, d), lambda bi, t: (bi, t, 0)),
                  pl.BlockSpec((d, 2 * LANES), lambda bi, t: (0, 0)),
                  pl.BlockSpec((1, LANES), lambda bi, t: (0, 0))],
        out_specs=pl.BlockSpec((1, ts, LANES), lambda bi, t: (bi, t, 0)),
        out_shape=jax.ShapeDtypeStruct((b, lp, LANES), F32),
        scratch_shapes=[pltpu.VMEM((1, LANES), F32)],
        compiler_params=_cparams(("parallel", "arbitrary")),
        name="forget_cumsum",
    )(hn3, w_f, b_f)


def _pair_rms(x, first_head, gain, eps=1e-6):
    sq = x * x
    s0 = jnp.sum(jnp.where(first_head, sq, 0.0), axis=-1, keepdims=True)
    s1 = jnp.sum(jnp.where(first_head, 0.0, sq), axis=-1, keepdims=True)
    ms = jnp.where(first_head, s0, s1) * (1.0 / HEAD_DIM)
    return x * lax.rsqrt(ms + eps) * gain


def _attn_kernel(lo_ref, q_ref, k_ref, v_ref, fc_ref, fr_ref, qg_ref, kg_ref, o_ref, kn_ref, *,
                 tq, ts, tk, lead):
    bi = pl.program_id(0)
    p = pl.program_id(1)
    i = pl.program_id(2)
    nq = pl.num_programs(2)
    lp = k_ref.shape[1]
    lane = lax.broadcasted_iota(jnp.int32, (1, LANES), 1)
    first_head = lane < HEAD_DIM

    @pl.when(i == 0)
    def _():
        def body(c, carry):
            off = pl.multiple_of(c * KNORM_ROWS, KNORM_ROWS)
            kk = k_ref[0, pl.ds(off, KNORM_ROWS), :].astype(F32)
            kn_ref[pl.ds(off, KNORM_ROWS), :] = _pair_rms(kk, first_head, kg_ref[...]).astype(BF16)
            return carry
        lax.fori_loop(0, lp // KNORM_ROWS, body, 0)

    qn = _pair_rms(q_ref[0].astype(F32), first_head, qg_ref[...])
    fc = fc_ref[0]

    heads = [2 * p, 2 * p + 1]
    qhs = [jnp.where(first_head, qn, 0.0).astype(BF16), jnp.where(first_head, 0.0, qn).astype(BF16)]
    fts = [jnp.sum(jnp.where(lane == hd, fc, 0.0), axis=-1, keepdims=True) * LOG2E for hd in heads]

    def update(rows, off, width, carry, hh, row_pos=None, own_mask=None):
        m, l, acc = carry
        ks = kn_ref[pl.ds(off, width), :]
        s = lax.dot_general(qhs[hh][rows], ks, (((1,), (1,)), ((), ())), preferred_element_type=F32)
        f_all = fr_ref[0, :, pl.ds(off, width)]
        head_row = lax.broadcasted_iota(jnp.int32, (ATT_HEADS, 1), 0) == heads[hh]
        fs = jnp.sum(jnp.where(head_row, f_all, 0.0), axis=0, keepdims=True) * LOG2E
        s = s + (fts[hh][rows] - fs)
        if own_mask is not None:
            s = jnp.concatenate([s[:, :lead], jnp.where(own_mask, s[:, lead:], NEG_INF)], axis=1)
        elif row_pos is not None:
            col_pos = off + lax.broadcasted_iota(jnp.int32, (1, width), 1)
            s = jnp.where(col_pos <= row_pos, s, NEG_INF)
        m_new = jnp.maximum(m, jnp.max(s, axis=-1, keepdims=True))
        alpha = jnp.exp2(m - m_new)
        pm = jnp.exp2(s - m_new)
        l = alpha * l + jnp.sum(pm, axis=-1, keepdims=True)
        vs = v_ref[0, pl.ds(off, width), :]
        acc = alpha * acc + jnp.dot(pm.astype(BF16), vs, preferred_element_type=F32)
        return m_new, l, acc

    nsub = tq // ts
    init = (jnp.full((ts, 1), NEG_INF, F32), jnp.zeros((ts, 1), F32), jnp.zeros((ts, LANES), F32))
    carries, wins = [], []
    for u in range(nsub):
        rows = slice(u * ts, (u + 1) * ts)
        g = i * nsub + u
        win = jnp.maximum(g * (ts // tk) - lead // tk, 0)
        firsts = [lo_ref[(bi * ATT_HEADS + hd) * (nq * nsub) + g] for hd in heads]

        def far(j, carry, rows=rows):
            off = pl.multiple_of(j * tk, tk)
            return tuple(update(rows, off, tk, carry[hh], hh) for hh in range(2))

        carries.append(lax.fori_loop(jnp.minimum(firsts[0], firsts[1]), win, far, (init, init)))
        wins.append(win)
    lower_tri = (lax.broadcasted_iota(jnp.int32, (ts, ts), 1) <= lax.broadcasted_iota(jnp.int32, (ts, ts), 0))
    for u in range(nsub):
        rows = slice(u * ts, (u + 1) * ts)
        off = pl.multiple_of(wins[u] * tk, tk)
        if u == 0:
            mask = dict(row_pos=i * tq + lax.broadcasted_iota(jnp.int32, (ts, 1), 0))
        else:
            mask = dict(own_mask=lower_tri)
        outs = []
        for hh in range(2):
            m, l, acc = update(rows, off, lead + ts, carries[u][hh], hh, **mask)
            outs.append(acc / l)
        o_ref[0, rows, :] = jnp.where(first_head, outs[0], outs[1]).astype(o_ref.dtype)


def _first_live_chunk(frow, qg, kg, tq, tk):
    b, nh, lp = frow.shape
    qk_bound = 1.02 * HEAD_DIM * jnp.max(jnp.abs(qg)) * jnp.max(jnp.abs(kg))
    f_first = frow[:, :, 0::tq]
    f_last = frow[:, :, tk - 1::tk]
    dead = (f_first[:, :, :, None] - f_last[:, :, None, :] + 2.0 * qk_bound) < -EXP_UNDERFLOW
    n_dead = jnp.sum(dead.astype(jnp.int32), axis=-1)
    n_full = (jnp.arange(lp // tq, dtype=jnp.int32) * tq) // tk
    return jnp.minimum(n_dead, n_full[None, None, :]).reshape(-1)


def _attention(z3, fcol, frow, qg, kg, tq, ts, tk, lead):
    b, lp, _ = z3.shape
    npairs = ATT_HEADS // 2
    qb, kb, vb = Z_Q // LANES, Z_K // LANES, Z_V // LANES
    assert tq % ts == 0 and ts % tk == 0 and lead % tk == 0 and lp >= lead + ts and lp % KNORM_ROWS == 0
    first = _first_live_chunk(frow, qg, kg, ts, tk)
    grid_spec = pltpu.PrefetchScalarGridSpec(
        num_scalar_prefetch=1,
        grid=(b, npairs, lp // tq),
        in_specs=[pl.BlockSpec((1, tq, LANES), lambda bi, p, i, lo: (bi, i, qb + p)),
                  pl.BlockSpec((1, lp, LANES), lambda bi, p, i, lo: (bi, 0, kb + p)),
                  pl.BlockSpec((1, lp, LANES), lambda bi, p, i, lo: (bi, 0, vb + p)),
                  pl.BlockSpec((1, tq, LANES), lambda bi, p, i, lo: (bi, i, 0)),
                  pl.BlockSpec((1, ATT_HEADS, lp), lambda bi, p, i, lo: (bi, 0, 0)),
                  pl.BlockSpec((1, LANES), lambda bi, p, i, lo: (0, 0)),
                  pl.BlockSpec((1, LANES), lambda bi, p, i, lo: (0, 0))],
        out_specs=pl.BlockSpec((1, tq, LANES), lambda bi, p, i, lo: (bi, i, p)),
        scratch_shapes=[pltpu.VMEM((lp, LANES), BF16)],
    )
    return pl.pallas_call(
        functools.partial(_attn_kernel, tq=tq, ts=ts, tk=tk, lead=lead),
        grid_spec=grid_spec,
        out_shape=jax.ShapeDtypeStruct((b, lp, ATT_W), BF16),
        compiler_params=_cparams(("parallel", "parallel", "arbitrary")),
        name="fox_attention",
    )(first, z3, z3, z3, fcol, frow, qg * LOG2E, kg)


def _convpool_kernel(a_ref, g_ref, p_ref, ah_ref, gh_ref, ph_ref, cw_ref, cb_ref, lg_ref, lb_ref,
                     pw_ref, ps_ref, u_ref, pm_ref, ext_ref, pext_ref, *, ts):
    i = pl.program_id(1)
    has_prev = i > 0

    def stage(ref, halo_rows, rows):
        ref[0, 0:HALO, :] = halo_rows
        ref[0, HALO:HALO + ts, :] = rows
        base = ref[0]
        for r in range(1, SUBLANES):
            ref[r, r:HALO + ts, :] = base[0:HALO + ts - r, :]

    def behind(ref, back):
        start = HALO - back // SUBLANES * SUBLANES
        return ref[back % SUBLANES, start:start + ts, :]

    u = a_ref[0].astype(F32) * _sigmoid(g_ref[0].astype(F32))
    uh = ah_ref[0].astype(F32) * _sigmoid(gh_ref[0].astype(F32))
    stage(ext_ref, jnp.where(has_prev, uh, 0.0), u)
    acc = jnp.zeros((ts, CONV_W), F32) + cb_ref[...]
    for j in range(CONV_K):
        acc = acc + cw_ref[j:j + 1, :] * behind(ext_ref, CONV_K - 1 - j)
    mu = jnp.mean(acc, axis=-1, keepdims=True)
    cen = acc - mu
    var = jnp.mean(cen * cen, axis=-1, keepdims=True)
    y = cen * lax.rsqrt(var + 1e-5) * lg_ref[...] + lb_ref[...]
    u_ref[0] = _silu(y).astype(u_ref.dtype)

    x = p_ref[0].astype(F32)
    stage(pext_ref, jnp.where(has_prev, ph_ref[0].astype(F32), 0.0), x)
    pos1 = (i * ts + 1 + lax.broadcasted_iota(jnp.int32, (ts, 1), 0)).astype(F32)
    lane = lax.broadcasted_iota(jnp.int32, (1, POOL_W), 1)
    group_w = POOL_W // len(POOL_WINDOWS)
    run = x
    pooled = jnp.zeros((ts, POOL_W), F32)
    for k in range(1, max(POOL_WINDOWS)):
        run = run + behind(pext_ref, k)
        if (k + 1) in POOL_WINDOWS:
            gi = POOL_WINDOWS.index(k + 1)
            mean = run / jnp.minimum(pos1, float(k + 1))
            in_group = (lane >= gi * group_w) & (lane < (gi + 1) * group_w)
            pooled = jnp.where(in_group, mean, pooled)
    pm = (pooled - x).astype(BF16)
    lin = jnp.dot(pm, pw_ref[...], preferred_element_type=F32) * ps_ref[...]
    pm_ref[0] = lin.astype(pm_ref.dtype)


def _convpool(z3, cw, cb, lg, lb, pw, ps, ts):
    b, lp, _ = z3.shape
    ab, gb, pb = Z_A // CONV_W, Z_G // CONV_W, Z_P // POOL_W
    hpt = ts // HALO
    main = lambda blk: pl.BlockSpec((1, ts, CONV_W), lambda bi, i: (bi, i, blk))
    halo = lambda blk: pl.BlockSpec((1, HALO, CONV_W), lambda bi, i: (bi, jnp.maximum(i * hpt - 1, 0), blk))
    const = lambda shape: pl.BlockSpec(shape, lambda bi, i: (0, 0))
    out = pl.BlockSpec((1, ts, CONV_W), lambda bi, i: (bi, i, 0))
    return pl.pallas_call(
        functools.partial(_convpool_kernel, ts=ts),
        grid=(b, lp // ts),
        in_specs=[main(ab), main(gb), main(pb), halo(ab), halo(gb), halo(pb),
                  const((HALO, CONV_W)), const((1, CONV_W)), const((1, CONV_W)), const((1, CONV_W)),
                  const((POOL_W, POOL_W)), const((1, POOL_W))],
        out_specs=[out, out],
        out_shape=[jax.ShapeDtypeStruct((b, lp, CONV_W), BF16), jax.ShapeDtypeStruct((b, lp, POOL_W), BF16)],
        scratch_shapes=[pltpu.VMEM((SUBLANES, HALO + ts, CONV_W), F32),
                        pltpu.VMEM((SUBLANES, HALO + ts, POOL_W), F32)],
        compiler_params=_cparams(("parallel", "arbitrary")),
        name="conv_pool",
    )(z3, z3, z3, z3, z3, z3, cw, cb, lg, lb, pw, ps)


def _merge_kernel(att_ref, u_ref, pm_ref, g0_ref, g1_ref, g2_ref, h_ref, wa_ref, wc_ref, wp_ref, wo_ref,
                  ng_ref, h_out_ref, hn_out_ref):
    ya = jnp.dot(att_ref[...], wa_ref[...], preferred_element_type=F32)
    yc = jnp.dot(u_ref[...], wc_ref[...], preferred_element_type=F32)
    yp = jnp.dot(pm_ref[...], wp_ref[...], preferred_element_type=F32)
    m = (_sigmoid(g0_ref[...].astype(F32)) * ya + _sigmoid(g1_ref[...].astype(F32)) * yc
         + _sigmoid(g2_ref[...].astype(F32)) * yp)
    h = h_ref[...] + jnp.dot(m.astype(BF16), wo_ref[...], preferred_element_type=F32)
    h_out_ref[...] = h
    hn_out_ref[...] = _rms(h, ng_ref[...]).astype(hn_out_ref.dtype)


def _merge(att, u, pm, z, h, wa, wc, wp, wo, ng, tm):
    n, d = h.shape
    row = lambda w, blk=0: pl.BlockSpec((tm, w), lambda i: (i, blk))
    const = lambda shape: pl.BlockSpec(shape, lambda i: (0, 0))
    return pl.pallas_call(
        _merge_kernel,
        grid=(n // tm,),
        in_specs=[row(ATT_W), row(CONV_W), row(POOL_W), row(d, 0), row(d, 1), row(d, 2), row(d),
                  const(wa.shape), const(wc.shape), const(wp.shape), const(wo.shape), const((1, d))],
        out_specs=[row(d), row(d)],
        out_shape=[jax.ShapeDtypeStruct((n, d), F32), jax.ShapeDtypeStruct((n, d), BF16)],
        compiler_params=_cparams(("parallel",)),
        name="merge_outproj",
    )(att, u, pm, z, z, z, h, wa, wc, wp, wo, ng)


def _ffn_kernel(x_ref, h_ref, wg_ref, wu_ref, wd_ref, ng_ref, h_out_ref, hn_out_ref, *, tf):
    x = x_ref[...]
    h = h_ref[...]
    nc = wg_ref.shape[1] // tf

    def gate_up(c):
        cols = slice(c * tf, (c + 1) * tf)
        return (jnp.dot(x, wg_ref[:, cols], preferred_element_type=F32),
                jnp.dot(x, wu_ref[:, cols], preferred_element_type=F32))

    gt, up = gate_up(0)
    for c in range(nc):
        act = (_silu(gt) * up).astype(BF16)
        if c + 1 < nc:
            gt, up = gate_up(c + 1)
        h = h + jnp.dot(act, wd_ref[c * tf:(c + 1) * tf, :], preferred_element_type=F32)
    h_out_ref[...] = h
    hn_out_ref[...] = _rms(h, ng_ref[...]).astype(hn_out_ref.dtype)


def _ffn(hn, h, wg, wu, wd, ng, tm, tf):
    n, d = h.shape
    row = pl.BlockSpec((tm, d), lambda i: (i, 0))
    const = lambda shape: pl.BlockSpec(shape, lambda i: (0, 0), pipeline_mode=pl.Buffered(1))
    return pl.pallas_call(
        functools.partial(_ffn_kernel, tf=tf),
        grid=(n // tm,),
        in_specs=[row, row, const(wg.shape), const(wu.shape), const(wd.shape), const((1, d))],
        out_specs=[row, row],
        out_shape=[jax.ShapeDtypeStruct((n, d), F32), jax.ShapeDtypeStruct((n, d), BF16)],
        compiler_params=_cparams(("parallel",)),
        name="dense_swiglu",
    )(hn, h, wg, wu, wd, ng)


def _router_kernel(h_ref, ng_ref, wr_ref, br_ref, comb_ref, pos_ref, cnt_ref, *, sb):
    tm = h_ref.shape[0]
    hn = _rms(h_ref[...], ng_ref[...])
    h_hi = hn.astype(BF16)
    h_lo = (hn - h_hi.astype(F32)).astype(BF16)
    hw = jnp.dot(h_hi, wr_ref[...], preferred_element_type=F32)
    lw = jnp.dot(h_lo, wr_ref[:, :LANES], preferred_element_type=F32)
    logits = hw[:, :LANES] + (hw[:, LANES:] + lw) + br_ref[...]
    lane = lax.broadcasted_iota(jnp.int32, (1, LANES), 1).astype(F32)
    lg = jnp.where(lane < N_EXPERTS, logits, -jnp.inf)
    m1 = jnp.max(lg, axis=-1, keepdims=True)
    i1 = jnp.min(jnp.where(lg == m1, lane, float(LANES)), axis=-1, keepdims=True)
    sel1 = lane == i1
    lg2 = jnp.where(sel1, -jnp.inf, lg)
    m2 = jnp.max(lg2, axis=-1, keepdims=True)
    i2 = jnp.min(jnp.where(lg2 == m2, lane, float(LANES)), axis=-1, keepdims=True)
    sel2 = lane == i2
    e = jnp.exp(m2 - m1)
    g1 = 1.0 / (1.0 + e)
    comb_ref[...] = jnp.where(sel1, g1, 0.0) + jnp.where(sel2, e * g1, 0.0)
    sel = jnp.where(sel1 | sel2, 1.0, 0.0)
    r = lax.broadcasted_iota(jnp.int32, (sb, sb), 0)
    c = lax.broadcasted_iota(jnp.int32, (sb, sb), 1)
    tril = jnp.where(c <= r, 1.0, 0.0).astype(BF16)
    carry = jnp.zeros((1, LANES), F32)
    for s in range(tm // sb):
        blk = slice(s * sb, (s + 1) * sb)
        incl = jnp.dot(tril, sel[blk].astype(BF16), preferred_element_type=F32) + carry
        pos_ref[blk, :] = jnp.where(sel[blk] > 0.0, incl - 1.0, -1.0)
        carry = incl[sb - 1:sb, :]
    cnt_ref[0] = jnp.broadcast_to(carry, (8, LANES)).astype(jnp.int32)


def _router(h, ng, wr, br, tm, sb):
    n, d = h.shape
    nt = n // tm
    return pl.pallas_call(
        functools.partial(_router_kernel, sb=sb),
        grid=(nt,),
        in_specs=[pl.BlockSpec((tm, d), lambda i: (i, 0)),
                  pl.BlockSpec((1, d), lambda i: (0, 0)),
                  pl.BlockSpec((d, 2 * LANES), lambda i: (0, 0)),
                  pl.BlockSpec((1, LANES), lambda i: (0, 0))],
        out_specs=[pl.BlockSpec((tm, LANES), lambda i: (i, 0)),
                   pl.BlockSpec((tm, LANES), lambda i: (i, 0)),
                   pl.BlockSpec((1, 8, LANES), lambda i: (i, 0, 0))],
        out_shape=[jax.ShapeDtypeStruct((n, LANES), F32), jax.ShapeDtypeStruct((n, LANES), F32),
                   jax.ShapeDtypeStruct((nt, 8, LANES), jnp.int32)],
        compiler_params=_cparams(("parallel",)),
        name="router_top2",
    )(h, ng, wr, br)


def _moe_kernel(cnt_ref, x_ref, post_ref, pos_ref, comb_ref, wg_ref, wu_ref, wd_ref,
                out_ref, xs_ref, ys_ref, *, sizes, sub):
    i = pl.program_id(0)
    e = pl.program_id(1)
    f = pl.program_id(2)
    nf = pl.num_programs(2)
    cnt = cnt_ref[i * N_EXPERTS + e]
    big = sizes[-1]

    def for_each_chunk(body):
        lo = 0
        for r in sizes:
            @pl.when((cnt > lo) & (cnt <= r))
            def _(r=r):
                body(0, r)
            lo = r

        @pl.when(cnt > big)
        def _():
            def step(c, carry):
                body(pl.multiple_of(c * big, big), big)
                return carry
            lax.fori_loop(0, (cnt + big - 1) // big, step, 0)

    @pl.when((e == 0) & (f == 0))
    def _():
        out_ref[...] = jnp.zeros_like(out_ref)

    @pl.when(f == 0)
    def _():
        prow = post_ref[0, pl.ds(e, 1), :]

        def gather(off, r):
            tgt = (off + lax.broadcasted_iota(jnp.int32, (r, 1), 0)).astype(F32)
            onehot = jnp.where(prow == tgt, 1.0, 0.0).astype(BF16)
            xs_ref[pl.ds(off, r), :] = jnp.dot(onehot, x_ref[...], preferred_element_type=F32).astype(BF16)
            ys_ref[pl.ds(off, r), :] = jnp.zeros((r, D_MODEL), F32)
        for_each_chunk(gather)

    def expert(off, r):
        xc = xs_ref[pl.ds(off, r), :]
        y = ys_ref[pl.ds(off, r), :]
        tf = wg_ref.shape[2]
        edges = list(range(0, tf, sub)) + [tf]
        spans = [slice(a, b) for a, b in zip(edges[:-1], edges[1:])]

        def gate_up(cols):
            return (jnp.dot(xc, wg_ref[0, :, cols], preferred_element_type=F32),
                    jnp.dot(xc, wu_ref[0, :, cols], preferred_element_type=F32))

        gt, up = gate_up(spans[0])
        for s, cols in enumerate(spans):
            act = (_silu(gt) * up).astype(BF16)
            if s + 1 < len(spans):
                gt, up = gate_up(spans[s + 1])
            y = y + jnp.dot(act, wd_ref[0, cols, :], preferred_element_type=F32)
        ys_ref[pl.ds(off, r), :] = y
    for_each_chunk(expert)

    @pl.when(f == nf - 1)
    def _():
        lane = lax.broadcasted_iota(jnp.int32, (1, LANES), 1)
        pcol = jnp.sum(jnp.where(lane == e, pos_ref[...], 0.0), axis=-1, keepdims=True)
        gcol = jnp.sum(jnp.where(lane == e, comb_ref[...], 0.0), axis=-1, keepdims=True)

        def scatter(off, r):
            tgt = (off + lax.broadcasted_iota(jnp.int32, (1, r), 1)).astype(F32)
            onehot_t = jnp.where(pcol == tgt, 1.0, 0.0).astype(BF16)
            y = ys_ref[pl.ds(off, r), :].astype(BF16)
            out_ref[...] += gcol * jnp.dot(onehot_t, y, preferred_element_type=F32)
        for_each_chunk(scatter)


def _moe(cnt, hn, post, pos, comb, wg, wu, wd, tm, tf, sizes, sub):
    n, d = hn.shape
    nf = wg.shape[2] // tf
    cap = -(-tm // sizes[-1]) * sizes[-1]
    once = pl.Buffered(1)
    row = lambda w: pl.BlockSpec((tm, w), lambda i, e, f, cnt: (i, 0), pipeline_mode=once)
    grid_spec = pltpu.PrefetchScalarGridSpec(
        num_scalar_prefetch=1,
        grid=(n // tm, N_EXPERTS, nf),
        in_specs=[row(d),
                  pl.BlockSpec((1, POST_ROWS, tm), lambda i, e, f, cnt: (i, 0, 0), pipeline_mode=once),
                  row(LANES), row(LANES),
                  pl.BlockSpec((1, d, tf), lambda i, e, f, cnt: (e, 0, f)),
                  pl.BlockSpec((1, d, tf), lambda i, e, f, cnt: (e, 0, f)),
                  pl.BlockSpec((1, tf, d), lambda i, e, f, cnt: (e, f, 0))],
        out_specs=pl.BlockSpec((tm, d), lambda i, e, f, cnt: (i, 0)),
        scratch_shapes=[pltpu.VMEM((cap, d), BF16), pltpu.VMEM((cap, d), F32)],
    )
    return pl.pallas_call(
        functools.partial(_moe_kernel, sizes=sizes, sub=sub),
        grid_spec=grid_spec,
        out_shape=jax.ShapeDtypeStruct((n, d), F32),
        compiler_params=pltpu.CompilerParams(dimension_semantics=("parallel", "arbitrary", "arbitrary"),
                                             vmem_limit_bytes=MOE_VMEM_LIMIT),
        name="expert_swiglu",
    )(cnt, hn, post, pos, comb, wg, wu, wd)


def _add_norm_kernel(h_ref, d_ref, g_ref, h_out_ref, hn_out_ref):
    h = h_ref[...] + d_ref[...]
    h_out_ref[...] = h
    hn_out_ref[...] = _rms(h, g_ref[...]).astype(hn_out_ref.dtype)


def _add_norm(h, delta, g, tm):
    n, d = h.shape
    row = pl.BlockSpec((tm, d), lambda i: (i, 0))
    return pl.pallas_call(
        _add_norm_kernel,
        grid=(n // tm,),
        in_specs=[row, row, pl.BlockSpec((1, d), lambda i: (0, 0))],
        out_specs=[row, row],
        out_shape=[jax.ShapeDtypeStruct((n, d), F32), jax.ShapeDtypeStruct((n, d), BF16)],
        compiler_params=_cparams(("parallel",)),
        name="residual_norm",
    )(h, delta, g)


def _add_drop_meta_kernel(h_ref, d_ref, ht_ref, dt_ref, o_ref):
    body = h_ref[0] + d_ref[0]
    tail = ht_ref[0] + dt_ref[0]
    o_ref[0] = jnp.concatenate([body[N_META:], tail], axis=0)


def _add_drop_meta(h3, delta3, seq):
    b, lp, d = h3.shape
    per = OUT_ROWS // N_META
    main = pl.BlockSpec((1, OUT_ROWS, d), lambda bi, j: (bi, j, 0))
    tail = pl.BlockSpec((1, N_META, d), lambda bi, j: (bi, (j + 1) * per, 0))
    return pl.pallas_call(
        _add_drop_meta_kernel,
        grid=(b, seq // OUT_ROWS),
        in_specs=[main, main, tail, tail],
        out_specs=main,
        out_shape=jax.ShapeDtypeStruct((b, seq, d), F32),
        compiler_params=_cparams(("parallel", "parallel")),
        name="residual_drop_meta",
    )(h3, delta3, h3, delta3)


def _pad_lanes(a, width=LANES):
    return jnp.pad(a, ((0, 0), (0, width - a.shape[-1])))


def kernel(x, meta, norm_mix, w_in, b_in, q_norm, k_norm, w_attn_o, conv_w, conv_b, conv_ln_g, conv_ln_b,
           w_conv_o, pool_w, pool_scale, w_pool_o, w_out, norm_ffn, w_ff_gate, w_ff_up, w_ff_down, w_router,
           b_router, w_e_gate, w_e_up, w_e_down):
    bsz, seq, d = x.shape
    depth = w_in.shape[0]
    length = seq + N_META
    lp = -(-length // SEQ_ALIGN) * SEQ_ALIGN
    n = bsz * lp

    tm = _pick(n, (1536, 768, 512, 256))
    tm_mid = _pick(n, (768, 512, 256))
    tm_ffn = _pick(n, (768, 512, 256))
    tm_moe = _pick(n, (1408, 768, 512, 256))
    sb_router = _pick(tm_moe, (704, 768, 512, 256))
    tn = _pick(Z_W, (1792, 768, 256))
    ts = _pick(lp, (768, 512, 256))
    ts_f = ts
    tq = _pick(lp, (2816, 768, 256))
    ts_att, tk, att_lead = (256, 256, 256)
    tf_dense = _pick(w_ff_gate.shape[2], (256,))
    tf_moe = _pick(w_e_gate.shape[3], (1792, 512, 256))
    sub_moe = _pick(tf_moe, (256,))
    quarter = tm_moe // 4
    moe_sizes = tuple(range(max(quarter - 96, 64) // 64 * 64, quarter + 161, 64))

    h = jnp.concatenate([jnp.broadcast_to(meta[None].astype(x.dtype), (bsz, N_META, d)), x], axis=1)
    h = jnp.pad(h, ((0, 0), (0, lp - length), (0, 0))).reshape(n, d)
    hn = _norm(h, norm_mix[0][None], tm)

    att_scale = HEAD_DIM ** -0.5
    gate_lo = 3 * ATT_W + ATT_HEADS + 2 * CONV_W + POOL_W
    f_lo = 3 * ATT_W
    for l in range(depth):
        wl = w_in[l]
        w_main = jnp.concatenate([wl[:, gate_lo:], wl[:, :f_lo], wl[:, f_lo + ATT_HEADS:gate_lo]], axis=1).astype(BF16)
        bl = b_in[l]
        b_main = jnp.concatenate([bl[gate_lo:], bl[:f_lo], bl[f_lo + ATT_HEADS:gate_lo]])[None]
        w_f = _pad_lanes(wl[:, f_lo:f_lo + ATT_HEADS])
        w_f_hi = w_f.astype(BF16)
        w_f = jnp.concatenate([w_f_hi, (w_f - w_f_hi.astype(F32)).astype(BF16)], axis=1)
        b_f = _pad_lanes(bl[None, f_lo:f_lo + ATT_HEADS])

        z = _inproj(hn, w_main, b_main, tm, tn)
        z3 = z.reshape(bsz, lp, Z_W)
        fcol = _forget(hn.reshape(bsz, lp, d), w_f, b_f, ts_f)
        frow = jnp.transpose(fcol[:, :, :ATT_HEADS], (0, 2, 1))
        qg = jnp.tile(q_norm[l], 2)[None] * att_scale
        kg = jnp.tile(k_norm[l], 2)[None]
        att = _attention(z3, fcol, frow, qg, kg, tq, ts_att, tk, att_lead)

        cw = jnp.pad(conv_w[l], ((0, HALO - CONV_K), (0, 0)))
        pw = jax.scipy.linalg.block_diag(*[pool_w[l, g] for g in range(pool_w.shape[1])]).astype(BF16)
        u, pm = _convpool(z3, cw, conv_b[l][None], conv_ln_g[l][None], conv_ln_b[l][None], pw,
                          pool_scale[l][None], ts)

        h, hn = _merge(att.reshape(n, ATT_W), u.reshape(n, CONV_W), pm.reshape(n, POOL_W), z, h,
                       w_attn_o[l].astype(BF16), w_conv_o[l].astype(BF16), w_pool_o[l].astype(BF16),
                       w_out[l].astype(BF16), norm_ffn[l][None], tm_mid)

        ng_next = norm_mix[min(l + 1, depth - 1)][None]
        i = l // 2
        if l % 2 == 0:
            h, hn = _ffn(hn, h, w_ff_gate[i].astype(BF16), w_ff_up[i].astype(BF16), w_ff_down[i].astype(BF16),
                         ng_next, tm_ffn, tf_dense)
        else:
            w_r = _pad_lanes(w_router[i])
            w_r_hi = w_r.astype(BF16)
            w_r = jnp.concatenate([w_r_hi, (w_r - w_r_hi.astype(F32)).astype(BF16)], axis=1)
            comb, pos, cnt = _router(h, norm_ffn[l][None], w_r, _pad_lanes(b_router[i][None]), tm_moe, sb_router)
            post = jnp.transpose(pos[:, :POST_ROWS].reshape(n // tm_moe, tm_moe, POST_ROWS), (0, 2, 1))
            cnt_flat = cnt[:, 0, :N_EXPERTS].reshape(-1)
            delta = _moe(cnt_flat, hn, post, pos, comb, _to_bf16(w_e_gate, i), _to_bf16(w_e_up, i),
                         _to_bf16(w_e_down, i), tm_moe, tf_moe, moe_sizes, sub_moe)
            if l == depth - 1 and seq % OUT_ROWS == 0 and lp >= seq + OUT_ROWS:
                return _add_drop_meta(h.reshape(bsz, lp, d), delta.reshape(bsz, lp, d), seq)
            h, hn = _add_norm(h, delta, ng_next, tm)

    return h.reshape(bsz, lp, d)[:, N_META:length]
```

```python
import functools

import jax
import jax.numpy as jnp
from jax import lax
from jax.experimental import pallas as pl
from jax.experimental.pallas import tpu as pltpu

F32 = jnp.float32
BF16 = jnp.bfloat16

D_MODEL = 1024
N_META = 16
HEAD_DIM = 64
ATT_W = 512
ATT_HEADS = 8
CONV_W = 256
CONV_K = 31
POOL_W = 256
POOL_WINDOWS = (2, 4, 8, 16)
N_EXPERTS = 8
LANES = 128
SUBLANES = 8
HALO = 32
NEG_INF = -1e30
LOG2E = 1.4426950408889634
OUT_ROWS = 256
KNORM_ROWS = 256
EXP_UNDERFLOW = 110.0
SEQ_ALIGN = 256
VMEM_LIMIT = 56 * 1024 * 1024
MOE_VMEM_LIMIT = 60 * 1024 * 1024
TOP_K = 2
MOE_SIZE_STEP = 64
MOE_SIZE_BELOW = 96
MOE_SIZE_ABOVE = 160
POST_ROWS = 32

Z_Q = 3072
Z_K = 3584
Z_V = 4096
Z_A = 4608
Z_G = 4864
Z_P = 5120
Z_W = 5376


def _pick(n, candidates):
    for c in candidates:
        if n % c == 0:
            return c
    raise ValueError(f"no tile for {n} in {candidates}")


def _cparams(sem):
    return pltpu.CompilerParams(dimension_semantics=sem, vmem_limit_bytes=VMEM_LIMIT)


def _rms(x, g, eps=1e-6):
    return x * lax.rsqrt(jnp.mean(x * x, axis=-1, keepdims=True) + eps) * g


def _sigmoid(x):
    return 0.5 * jnp.tanh(0.5 * x) + 0.5


def _silu(x):
    return x * _sigmoid(x)


def _cast_kernel(w_ref, o_ref):
    o_ref[...] = w_ref[0].astype(o_ref.dtype)


def _to_bf16(w, layer):
    _, s, r, c = w.shape
    rows = _pick(r, (512, 256, 128, 8))
    return pl.pallas_call(
        _cast_kernel,
        grid=(s, r // rows),
        in_specs=[pl.BlockSpec((1, 1, rows, c), lambda i, j: (layer, i, j, 0))],
        out_specs=pl.BlockSpec((1, rows, c), lambda i, j: (i, j, 0)),
        out_shape=jax.ShapeDtypeStruct((s, r, c), BF16),
        compiler_params=_cparams(("parallel", "parallel")),
        name="weights_to_bf16",
    )(w)


def _norm_kernel(h_ref, g_ref, o_ref):
    o_ref[...] = _rms(h_ref[...], g_ref[...]).astype(o_ref.dtype)


def _norm(h, g, tm):
    n, d = h.shape
    return pl.pallas_call(
        _norm_kernel,
        grid=(n // tm,),
        in_specs=[pl.BlockSpec((tm, d), lambda i: (i, 0)), pl.BlockSpec((1, d), lambda i: (0, 0))],
        out_specs=pl.BlockSpec((tm, d), lambda i: (i, 0)),
        out_shape=jax.ShapeDtypeStruct((n, d), BF16),
        compiler_params=_cparams(("parallel",)),
        name="rmsnorm",
    )(h, g)


def _inproj_kernel(x_ref, w_ref, b_ref, o_ref):
    acc = jnp.dot(x_ref[...], w_ref[...], preferred_element_type=F32)
    o_ref[...] = (acc + b_ref[...]).astype(o_ref.dtype)


def _inproj(hn, w, b, tm, tn):
    n, d = hn.shape
    zw = w.shape[1]
    return pl.pallas_call(
        _inproj_kernel,
        grid=(n // tm, zw // tn),
        in_specs=[pl.BlockSpec((tm, d), lambda i, j: (i, 0)),
                  pl.BlockSpec((d, tn), lambda i, j: (0, j)),
                  pl.BlockSpec((1, tn), lambda i, j: (0, j))],
        out_specs=pl.BlockSpec((tm, tn), lambda i, j: (i, j)),
        out_shape=jax.ShapeDtypeStruct((n, zw), BF16),
        compiler_params=_cparams(("parallel", "arbitrary")),
        name="inproj",
    )(hn, w, b)


def _forget_kernel(x_ref, w_ref, b_ref, o_ref, carry_ref):
    @pl.when(pl.program_id(1) == 0)
    def _():
        carry_ref[...] = jnp.zeros_like(carry_ref)

    fw = jnp.dot(x_ref[0], w_ref[...], preferred_element_type=F32)
    f = fw[:, :LANES] + fw[:, LANES:] + b_ref[...]
    ls = jnp.minimum(f, 0.0) - jnp.log(1.0 + jnp.exp(-jnp.abs(f)))
    hi = ls.astype(BF16)
    r1 = ls - hi.astype(F32)
    mid = r1.astype(BF16)
    lo = (r1 - mid.astype(F32)).astype(BF16)
    t = ls.shape[0]
    r = lax.broadcasted_iota(jnp.int32, (t, t), 0)
    c = lax.broadcasted_iota(jnp.int32, (t, t), 1)
    tril = jnp.where(c <= r, 1.0, 0.0).astype(BF16)
    parts = jnp.dot(tril, jnp.concatenate([hi, mid, lo], axis=-1), preferred_element_type=F32)
    cs = (parts[:, :LANES] + parts[:, LANES:2 * LANES]) + parts[:, 2 * LANES:] + carry_ref[...]
    o_ref[0] = cs
    carry_ref[...] = cs[t - 1:t, :]


def _forget(hn3, w_f, b_f, ts):
    b, lp, d = hn3.shape
    return pl.pallas_call(
        _forget_kernel,
        grid=(b, lp // ts),
        in_specs=[pl.BlockSpec((1, ts, d), lambda bi, t: (bi, t, 0)),
                  pl.BlockSpec((d, 2 * LANES), lambda bi, t: (0, 0)),
                  pl.BlockSpec((1, LANES), lambda bi, t: (0, 0))],
        out_specs=pl.BlockSpec((1, ts, LANES), lambda bi, t: (bi, t, 0)),
        out_shape=jax.ShapeDtypeStruct((b, lp, LANES), F32),
        scratch_shapes=[pltpu.VMEM((1, LANES), F32)],
        compiler_params=_cparams(("parallel", "arbitrary")),
        name="forget_cumsum",
    )(hn3, w_f, b_f)


def _pair_rms(x, first_head, gain, eps=1e-6):
    sq = x * x
    s0 = jnp.sum(jnp.where(first_head, sq, 0.0), axis=-1, keepdims=True)
    s1 = jnp.sum(jnp.where(first_head, 0.0, sq), axis=-1, keepdims=True)
    ms = jnp.where(first_head, s0, s1) * (1.0 / HEAD_DIM)
    return x * lax.rsqrt(ms + eps) * gain


def _attn_kernel(lo_ref, q_ref, k_ref, v_ref, fc_ref, fr_ref, qg_ref, kg_ref, o_ref, kn_ref, *,
                 tq, ts, tk, lead):
    bi = pl.program_id(0)
    p = pl.program_id(1)
    i = pl.program_id(2)
    nq = pl.num_programs(2)
    lp = k_ref.shape[1]
    lane = lax.broadcasted_iota(jnp.int32, (1, LANES), 1)
    first_head = lane < HEAD_DIM

    @pl.when(i == 0)
    def _():
        def body(c, carry):
            off = pl.multiple_of(c * KNORM_ROWS, KNORM_ROWS)
            kk = k_ref[0, pl.ds(off, KNORM_ROWS), :].astype(F32)
            kn_ref[pl.ds(off, KNORM_ROWS), :] = _pair_rms(kk, first_head, kg_ref[...]).astype(BF16)
            return carry
        lax.fori_loop(0, lp // KNORM_ROWS, body, 0)

    qn = _pair_rms(q_ref[0].astype(F32), first_head, qg_ref[...])
    fc = fc_ref[0]

    heads = [2 * p, 2 * p + 1]
    qhs = [jnp.where(first_head, qn, 0.0).astype(BF16), jnp.where(first_head, 0.0, qn).astype(BF16)]
    fts = [jnp.sum(jnp.where(lane == hd, fc, 0.0), axis=-1, keepdims=True) * LOG2E for hd in heads]

    def update(rows, off, width, carry, hh, row_pos=None, own_mask=None):
        m, l, acc = carry
        ks = kn_ref[pl.ds(off, width), :]
        s = lax.dot_general(qhs[hh][rows], ks, (((1,), (1,)), ((), ())), preferred_element_type=F32)
        f_all = fr_ref[0, :, pl.ds(off, width)]
        head_row = lax.broadcasted_iota(jnp.int32, (ATT_HEADS, 1), 0) == heads[hh]
        fs = jnp.sum(jnp.where(head_row, f_all, 0.0), axis=0, keepdims=True) * LOG2E
        s = s + (fts[hh][rows] - fs)
        if own_mask is not None:
            s = jnp.concatenate([s[:, :lead], jnp.where(own_mask, s[:, lead:], NEG_INF)], axis=1)
        elif row_pos is not None:
            col_pos = off + lax.broadcasted_iota(jnp.int32, (1, width), 1)
            s = jnp.where(col_pos <= row_pos, s, NEG_INF)
        m_new = jnp.maximum(m, jnp.max(s, axis=-1, keepdims=True))
        alpha = jnp.exp2(m - m_new)
        pm = jnp.exp2(s - m_new)
        l = alpha * l + jnp.sum(pm, axis=-1, keepdims=True)
        vs = v_ref[0, pl.ds(off, width), :]
        acc = alpha * acc + jnp.dot(pm.astype(BF16), vs, preferred_element_type=F32)
        return m_new, l, acc

    nsub = tq // ts
    init = (jnp.full((ts, 1), NEG_INF, F32), jnp.zeros((ts, 1), F32), jnp.zeros((ts, LANES), F32))
    carries, wins = [], []
    for u in range(nsub):
        rows = slice(u * ts, (u + 1) * ts)
        g = i * nsub + u
        win = jnp.maximum(g * (ts // tk) - lead // tk, 0)
        firsts = [lo_ref[(bi * ATT_HEADS + hd) * (nq * nsub) + g] for hd in heads]

        def far(j, carry, rows=rows):
            off = pl.multiple_of(j * tk, tk)
            return tuple(update(rows, off, tk, carry[hh], hh) for hh in range(2))

        carries.append(lax.fori_loop(jnp.minimum(firsts[0], firsts[1]), win, far, (init, init)))
        wins.append(win)
    lower_tri = (lax.broadcasted_iota(jnp.int32, (ts, ts), 1) <= lax.broadcasted_iota(jnp.int32, (ts, ts), 0))
    for u in range(nsub):
        rows = slice(u * ts, (u + 1) * ts)
        off = pl.multiple_of(wins[u] * tk, tk)
        if u == 0:
            mask = dict(row_pos=i * tq + lax.broadcasted_iota(jnp.int32, (ts, 1), 0))
        else:
            mask = dict(own_mask=lower_tri)
        outs = []
        for hh in range(2):
            m, l, acc = update(rows, off, lead + ts, carries[u][hh], hh, **mask)
            outs.append(acc / l)
        o_ref[0, rows, :] = jnp.where(first_head, outs[0], outs[1]).astype(o_ref.dtype)


def _first_live_chunk(frow, qg, kg, tq, tk):
    b, nh, lp = frow.shape
    qk_bound = 1.02 * HEAD_DIM * jnp.max(jnp.abs(qg)) * jnp.max(jnp.abs(kg))
    f_first = frow[:, :, 0::tq]
    f_last = frow[:, :, tk - 1::tk]
    dead = (f_first[:, :, :, None] - f_last[:, :, None, :] + 2.0 * qk_bound) < -EXP_UNDERFLOW
    n_dead = jnp.sum(dead.astype(jnp.int32), axis=-1)
    n_full = (jnp.arange(lp // tq, dtype=jnp.int32) * tq) // tk
    return jnp.minimum(n_dead, n_full[None, None, :]).reshape(-1)


def _attention(z3, fcol, frow, qg, kg, tq, ts, tk, lead):
    b, lp, _ = z3.shape
    npairs = ATT_HEADS // 2
    qb, kb, vb = Z_Q // LANES, Z_K // LANES, Z_V // LANES
    assert tq % ts == 0 and ts % tk == 0 and lead % tk == 0 and lp >= lead + ts and lp % KNORM_ROWS == 0
    first = _first_live_chunk(frow, qg, kg, ts, tk)
    grid_spec = pltpu.PrefetchScalarGridSpec(
        num_scalar_prefetch=1,
        grid=(b, npairs, lp // tq),
        in_specs=[pl.BlockSpec((1, tq, LANES), lambda bi, p, i, lo: (bi, i, qb + p)),
                  pl.BlockSpec((1, lp, LANES), lambda bi, p, i, lo: (bi, 0, kb + p)),
                  pl.BlockSpec((1, lp, LANES), lambda bi, p, i, lo: (bi, 0, vb + p)),
                  pl.BlockSpec((1, tq, LANES), lambda bi, p, i, lo: (bi, i, 0)),
                  pl.BlockSpec((1, ATT_HEADS, lp), lambda bi, p, i, lo: (bi, 0, 0)),
                  pl.BlockSpec((1, LANES), lambda bi, p, i, lo: (0, 0)),
                  pl.BlockSpec((1, LANES), lambda bi, p, i, lo: (0, 0))],
        out_specs=pl.BlockSpec((1, tq, LANES), lambda bi, p, i, lo: (bi, i, p)),
        scratch_shapes=[pltpu.VMEM((lp, LANES), BF16)],
    )
    return pl.pallas_call(
        functools.partial(_attn_kernel, tq=tq, ts=ts, tk=tk, lead=lead),
        grid_spec=grid_spec,
        out_shape=jax.ShapeDtypeStruct((b, lp, ATT_W), BF16),
        compiler_params=_cparams(("parallel", "parallel", "arbitrary")),
        name="fox_attention",
    )(first, z3, z3, z3, fcol, frow, qg * LOG2E, kg)


def _convpool_kernel(a_ref, g_ref, p_ref, ah_ref, gh_ref, ph_ref, cw_ref, cb_ref, lg_ref, lb_ref,
                     pw_ref, ps_ref, u_ref, pm_ref, ext_ref, pext_ref, *, ts):
    i = pl.program_id(1)
    has_prev = i > 0

    def stage(ref, halo_rows, rows):
        ref[0, 0:HALO, :] = halo_rows
        ref[0, HALO:HALO + ts, :] = rows
        base = ref[0]
        for r in range(1, SUBLANES):
            ref[r, r:HALO + ts, :] = base[0:HALO + ts - r, :]

    def behind(ref, back):
        start = HALO - back // SUBLANES * SUBLANES
        return ref[back % SUBLANES, start:start + ts, :]

    u = a_ref[0].astype(F32) * _sigmoid(g_ref[0].astype(F32))
    uh = ah_ref[0].astype(F32) * _sigmoid(gh_ref[0].astype(F32))
    stage(ext_ref, jnp.where(has_prev, uh, 0.0), u)
    acc = jnp.zeros((ts, CONV_W), F32) + cb_ref[...]
    for j in range(CONV_K):
        acc = acc + cw_ref[j:j + 1, :] * behind(ext_ref, CONV_K - 1 - j)
    mu = jnp.mean(acc, axis=-1, keepdims=True)
    cen = acc - mu
    var = jnp.mean(cen * cen, axis=-1, keepdims=True)
    y = cen * lax.rsqrt(var + 1e-5) * lg_ref[...] + lb_ref[...]
    u_ref[0] = _silu(y).astype(u_ref.dtype)

    x = p_ref[0].astype(F32)
    stage(pext_ref, jnp.where(has_prev, ph_ref[0].astype(F32), 0.0), x)
    pos1 = (i * ts + 1 + lax.broadcasted_iota(jnp.int32, (ts, 1), 0)).astype(F32)
    lane = lax.broadcasted_iota(jnp.int32, (1, POOL_W), 1)
    group_w = POOL_W // len(POOL_WINDOWS)
    run = x
    pooled = jnp.zeros((ts, POOL_W), F32)
    for k in range(1, max(POOL_WINDOWS)):
        run = run + behind(pext_ref, k)
        if (k + 1) in POOL_WINDOWS:
            gi = POOL_WINDOWS.index(k + 1)
            mean = run / jnp.minimum(pos1, float(k + 1))
            in_group = (lane >= gi * group_w) & (lane < (gi + 1) * group_w)
            pooled = jnp.where(in_group, mean, pooled)
    pm = (pooled - x).astype(BF16)
    lin = jnp.dot(pm, pw_ref[...], preferred_element_type=F32) * ps_ref[...]
    pm_ref[0] = lin.astype(pm_ref.dtype)


def _convpool(z3, cw, cb, lg, lb, pw, ps, ts):
    b, lp, _ = z3.shape
    ab, gb, pb = Z_A // CONV_W, Z_G // CONV_W, Z_P // POOL_W
    hpt = ts // HALO
    main = lambda blk: pl.BlockSpec((1, ts, CONV_W), lambda bi, i: (bi, i, blk))
    halo = lambda blk: pl.BlockSpec((1, HALO, CONV_W), lambda bi, i: (bi, jnp.maximum(i * hpt - 1, 0), blk))
    const = lambda shape: pl.BlockSpec(shape, lambda bi, i: (0, 0))
    out = pl.BlockSpec((1, ts, CONV_W), lambda bi, i: (bi, i, 0))
    return pl.pallas_call(
        functools.partial(_convpool_kernel, ts=ts),
        grid=(b, lp // ts),
        in_specs=[main(ab), main(gb), main(pb), halo(ab), halo(gb), halo(pb),
                  const((HALO, CONV_W)), const((1, CONV_W)), const((1, CONV_W)), const((1, CONV_W)),
                  const((POOL_W, POOL_W)), const((1, POOL_W))],
        out_specs=[out, out],
        out_shape=[jax.ShapeDtypeStruct((b, lp, CONV_W), BF16), jax.ShapeDtypeStruct((b, lp, POOL_W), BF16)],
        scratch_shapes=[pltpu.VMEM((SUBLANES, HALO + ts, CONV_W), F32),
                        pltpu.VMEM((SUBLANES, HALO + ts, POOL_W), F32)],
        compiler_params=_cparams(("parallel", "arbitrary")),
        name="conv_pool",
    )(z3, z3, z3, z3, z3, z3, cw, cb, lg, lb, pw, ps)


def _merge_kernel(att_ref, u_ref, pm_ref, g0_ref, g1_ref, g2_ref, h_ref, wa_ref, wc_ref, wp_ref, wo_ref,
                  ng_ref, h_out_ref, hn_out_ref):
    ya = jnp.dot(att_ref[...], wa_ref[...], preferred_element_type=F32)
    yc = jnp.dot(u_ref[...], wc_ref[...], preferred_element_type=F32)
    yp = jnp.dot(pm_ref[...], wp_ref[...], preferred_element_type=F32)
    m = (_sigmoid(g0_ref[...].astype(F32)) * ya + _sigmoid(g1_ref[...].astype(F32)) * yc
         + _sigmoid(g2_ref[...].astype(F32)) * yp)
    h = h_ref[...] + jnp.dot(m.astype(BF16), wo_ref[...], preferred_element_type=F32)
    h_out_ref[...] = h
    hn_out_ref[...] = _rms(h, ng_ref[...]).astype(hn_out_ref.dtype)


def _merge(att, u, pm, z, h, wa, wc, wp, wo, ng, tm):
    n, d = h.shape
    row = lambda w, blk=0: pl.BlockSpec((tm, w), lambda i: (i, blk))
    const = lambda shape: pl.BlockSpec(shape, lambda i: (0, 0))
    return pl.pallas_call(
        _merge_kernel,
        grid=(n // tm,),
        in_specs=[row(ATT_W), row(CONV_W), row(POOL_W), row(d, 0), row(d, 1), row(d, 2), row(d),
                  const(wa.shape), const(wc.shape), const(wp.shape), const(wo.shape), const((1, d))],
        out_specs=[row(d), row(d)],
        out_shape=[jax.ShapeDtypeStruct((n, d), F32), jax.ShapeDtypeStruct((n, d), BF16)],
        compiler_params=_cparams(("parallel",)),
        name="merge_outproj",
    )(att, u, pm, z, z, z, h, wa, wc, wp, wo, ng)


def _ffn_kernel(x_ref, h_ref, wg_ref, wu_ref, wd_ref, ng_ref, h_out_ref, hn_out_ref, *, tf):
    x = x_ref[...]
    h = h_ref[...]
    nc = wg_ref.shape[1] // tf

    def gate_up(c):
        cols = slice(c * tf, (c + 1) * tf)
        return (jnp.dot(x, wg_ref[:, cols], preferred_element_type=F32),
                jnp.dot(x, wu_ref[:, cols], preferred_element_type=F32))

    gt, up = gate_up(0)
    for c in range(nc):
        act = (_silu(gt) * up).astype(BF16)
        if c + 1 < nc:
            gt, up = gate_up(c + 1)
        h = h + jnp.dot(act, wd_ref[c * tf:(c + 1) * tf, :], preferred_element_type=F32)
    h_out_ref[...] = h
    hn_out_ref[...] = _rms(h, ng_ref[...]).astype(hn_out_ref.dtype)


def _ffn(hn, h, wg, wu, wd, ng, tm, tf):
    n, d = h.shape
    row = pl.BlockSpec((tm, d), lambda i: (i, 0))
    const = lambda shape: pl.BlockSpec(shape, lambda i: (0, 0), pipeline_mode=pl.Buffered(1))
    return pl.pallas_call(
        functools.partial(_ffn_kernel, tf=tf),
        grid=(n // tm,),
        in_specs=[row, row, const(wg.shape), const(wu.shape), const(wd.shape), const((1, d))],
        out_specs=[row, row],
        out_shape=[jax.ShapeDtypeStruct((n, d), F32), jax.ShapeDtypeStruct((n, d), BF16)],
        compiler_params=_cparams(("parallel",)),
        name="dense_swiglu",
    )(hn, h, wg, wu, wd, ng)


def _router_kernel(h_ref, ng_ref, wr_ref, br_ref, comb_ref, pos_ref, cnt_ref, *, sb):
    tm = h_ref.shape[0]
    hn = _rms(h_ref[...], ng_ref[...])
    h_hi = hn.astype(BF16)
    h_lo = (hn - h_hi.astype(F32)).astype(BF16)
    hw = jnp.dot(h_hi, wr_ref[...], preferred_element_type=F32)
    lw = jnp.dot(h_lo, wr_ref[:, :LANES], preferred_element_type=F32)
    logits = hw[:, :LANES] + (hw[:, LANES:] + lw) + br_ref[...]
    lane = lax.broadcasted_iota(jnp.int32, (1, LANES), 1).astype(F32)
    lg = jnp.where(lane < N_EXPERTS, logits, -jnp.inf)
    m1 = jnp.max(lg, axis=-1, keepdims=True)
    i1 = jnp.min(jnp.where(lg == m1, lane, float(LANES)), axis=-1, keepdims=True)
    sel1 = lane == i1
    lg2 = jnp.where(sel1, -jnp.inf, lg)
    m2 = jnp.max(lg2, axis=-1, keepdims=True)
    i2 = jnp.min(jnp.where(lg2 == m2, lane, float(LANES)), axis=-1, keepdims=True)
    sel2 = lane == i2
    e = jnp.exp(m2 - m1)
    g1 = 1.0 / (1.0 + e)
    comb_ref[...] = jnp.where(sel1, g1, 0.0) + jnp.where(sel2, e * g1, 0.0)
    sel = jnp.where(sel1 | sel2, 1.0, 0.0)
    r = lax.broadcasted_iota(jnp.int32, (sb, sb), 0)
    c = lax.broadcasted_iota(jnp.int32, (sb, sb), 1)
    tril = jnp.where(c <= r, 1.0, 0.0).astype(BF16)
    carry = jnp.zeros((1, LANES), F32)
    for s in range(tm // sb):
        blk = slice(s * sb, (s + 1) * sb)
        incl = jnp.dot(tril, sel[blk].astype(BF16), preferred_element_type=F32) + carry
        pos_ref[blk, :] = jnp.where(sel[blk] > 0.0, incl - 1.0, -1.0)
        carry = incl[sb - 1:sb, :]
    cnt_ref[0] = jnp.broadcast_to(carry, (8, LANES)).astype(jnp.int32)


def _router(h, ng, wr, br, tm, sb):
    n, d = h.shape
    nt = n // tm
    return pl.pallas_call(
        functools.partial(_router_kernel, sb=sb),
        grid=(nt,),
        in_specs=[pl.BlockSpec((tm, d), lambda i: (i, 0)),
                  pl.BlockSpec((1, d), lambda i: (0, 0)),
                  pl.BlockSpec((d, 2 * LANES), lambda i: (0, 0)),
                  pl.BlockSpec((1, LANES), lambda i: (0, 0))],
        out_specs=[pl.BlockSpec((tm, LANES), lambda i: (i, 0)),
                   pl.BlockSpec((tm, LANES), lambda i: (i, 0)),
                   pl.BlockSpec((1, 8, LANES), lambda i: (i, 0, 0))],
        out_shape=[jax.ShapeDtypeStruct((n, LANES), F32), jax.ShapeDtypeStruct((n, LANES), F32),
                   jax.ShapeDtypeStruct((nt, 8, LANES), jnp.int32)],
        compiler_params=_cparams(("parallel",)),
        name="router_top2",
    )(h, ng, wr, br)


def _moe_kernel(cnt_ref, x_ref, post_ref, pos_ref, comb_ref, wg_ref, wu_ref, wd_ref,
                out_ref, xs_ref, ys_ref, *, sizes, sub):
    i = pl.program_id(0)
    e = pl.program_id(1)
    f = pl.program_id(2)
    nf = pl.num_programs(2)
    cnt = cnt_ref[i * N_EXPERTS + e]
    big = sizes[-1]

    def for_each_chunk(body):
        lo = 0
        for r in sizes:
            @pl.when((cnt > lo) & (cnt <= r))
            def _(r=r):
                body(0, r)
            lo = r

        @pl.when(cnt > big)
        def _():
            def step(c, carry):
                body(pl.multiple_of(c * big, big), big)
                return carry
            lax.fori_loop(0, (cnt + big - 1) // big, step, 0)

    @pl.when((e == 0) & (f == 0))
    def _():
        out_ref[...] = jnp.zeros_like(out_ref)

    @pl.when(f == 0)
    def _():
        prow = post_ref[0, pl.ds(e, 1), :]

        def gather(off, r):
            tgt = (off + lax.broadcasted_iota(jnp.int32, (r, 1), 0)).astype(F32)
            onehot = jnp.where(prow == tgt, 1.0, 0.0).astype(BF16)
            xs_ref[pl.ds(off, r), :] = jnp.dot(onehot, x_ref[...], preferred_element_type=F32).astype(BF16)
            ys_ref[pl.ds(off, r), :] = jnp.zeros((r, D_MODEL), F32)
        for_each_chunk(gather)

    def expert(off, r):
        xc = xs_ref[pl.ds(off, r), :]
        y = ys_ref[pl.ds(off, r), :]
        tf = wg_ref.shape[2]
        edges = list(range(0, tf, sub)) + [tf]
        spans = [slice(a, b) for a, b in zip(edges[:-1], edges[1:])]

        def gate_up(cols):
            return (jnp.dot(xc, wg_ref[0, :, cols], preferred_element_type=F32),
                    jnp.dot(xc, wu_ref[0, :, cols], preferred_element_type=F32))

        gt, up = gate_up(spans[0])
        for s, cols in enumerate(spans):
            act = (_silu(gt) * up).astype(BF16)
            if s + 1 < len(spans):
                gt, up = gate_up(spans[s + 1])
            y = y + jnp.dot(act, wd_ref[0, cols, :], preferred_element_type=F32)
        ys_ref[pl.ds(off, r), :] = y
    for_each_chunk(expert)

    @pl.when(f == nf - 1)
    def _():
        lane = lax.broadcasted_iota(jnp.int32, (1, LANES), 1)
        pcol = jnp.sum(jnp.where(lane == e, pos_ref[...], 0.0), axis=-1, keepdims=True)
        gcol = jnp.sum(jnp.where(lane == e, comb_ref[...], 0.0), axis=-1, keepdims=True)

        def scatter(off, r):
            tgt = (off + lax.broadcasted_iota(jnp.int32, (1, r), 1)).astype(F32)
            onehot_t = jnp.where(pcol == tgt, 1.0, 0.0).astype(BF16)
            y = ys_ref[pl.ds(off, r), :].astype(BF16)
            out_ref[...] += gcol * jnp.dot(onehot_t, y, preferred_element_type=F32)
        for_each_chunk(scatter)


def _moe(cnt, hn, post, pos, comb, wg, wu, wd, tm, tf, sizes, sub):
    n, d = hn.shape
    nf = wg.shape[2] // tf
    cap = -(-tm // sizes[-1]) * sizes[-1]
    once = pl.Buffered(1)
    row = lambda w: pl.BlockSpec((tm, w), lambda i, e, f, cnt: (i, 0), pipeline_mode=once)
    grid_spec = pltpu.PrefetchScalarGridSpec(
        num_scalar_prefetch=1,
        grid=(n // tm, N_EXPERTS, nf),
        in_specs=[row(d),
                  pl.BlockSpec((1, POST_ROWS, tm), lambda i, e, f, cnt: (i, 0, 0), pipeline_mode=once),
                  row(LANES), row(LANES),
                  pl.BlockSpec((1, d, tf), lambda i, e, f, cnt: (e, 0, f)),
                  pl.BlockSpec((1, d, tf), lambda i, e, f, cnt: (e, 0, f)),
                  pl.BlockSpec((1, tf, d), lambda i, e, f, cnt: (e, f, 0))],
        out_specs=pl.BlockSpec((tm, d), lambda i, e, f, cnt: (i, 0)),
        scratch_shapes=[pltpu.VMEM((cap, d), BF16), pltpu.VMEM((cap, d), F32)],
    )
    return pl.pallas_call(
        functools.partial(_moe_kernel, sizes=sizes, sub=sub),
        grid_spec=grid_spec,
        out_shape=jax.ShapeDtypeStruct((n, d), F32),
        compiler_params=pltpu.CompilerParams(dimension_semantics=("parallel", "arbitrary", "arbitrary"),
                                             vmem_limit_bytes=MOE_VMEM_LIMIT),
        name="expert_swiglu",
    )(cnt, hn, post, pos, comb, wg, wu, wd)


def _add_norm_kernel(h_ref, d_ref, g_ref, h_out_ref, hn_out_ref):
    h = h_ref[...] + d_ref[...]
    h_out_ref[...] = h
    hn_out_ref[...] = _rms(h, g_ref[...]).astype(hn_out_ref.dtype)


def _add_norm(h, delta, g, tm):
    n, d = h.shape
    row = pl.BlockSpec((tm, d), lambda i: (i, 0))
    return pl.pallas_call(
        _add_norm_kernel,
        grid=(n // tm,),
        in_specs=[row, row, pl.BlockSpec((1, d), lambda i: (0, 0))],
        out_specs=[row, row],
        out_shape=[jax.ShapeDtypeStruct((n, d), F32), jax.ShapeDtypeStruct((n, d), BF16)],
        compiler_params=_cparams(("parallel",)),
        name="residual_norm",
    )(h, delta, g)


def _add_drop_meta_kernel(h_ref, d_ref, ht_ref, dt_ref, o_ref):
    body = h_ref[0] + d_ref[0]
    tail = ht_ref[0] + dt_ref[0]
    o_ref[0] = jnp.concatenate([body[N_META:], tail], axis=0)


def _add_drop_meta(h3, delta3, seq):
    b, lp, d = h3.shape
    per = OUT_ROWS // N_META
    main = pl.BlockSpec((1, OUT_ROWS, d), lambda bi, j: (bi, j, 0))
    tail = pl.BlockSpec((1, N_META, d), lambda bi, j: (bi, (j + 1) * per, 0))
    return pl.pallas_call(
        _add_drop_meta_kernel,
        grid=(b, seq // OUT_ROWS),
        in_specs=[main, main, tail, tail],
        out_specs=main,
        out_shape=jax.ShapeDtypeStruct((b, seq, d), F32),
        compiler_params=_cparams(("parallel", "parallel")),
        name="residual_drop_meta",
    )(h3, delta3, h3, delta3)


def _pad_lanes(a, width=LANES):
    return jnp.pad(a, ((0, 0), (0, width - a.shape[-1])))


def kernel(x, meta, norm_mix, w_in, b_in, q_norm, k_norm, w_attn_o, conv_w, conv_b, conv_ln_g, conv_ln_b,
           w_conv_o, pool_w, pool_scale, w_pool_o, w_out, norm_ffn, w_ff_gate, w_ff_up, w_ff_down, w_router,
           b_router, w_e_gate, w_e_up, w_e_down):
    bsz, seq, d = x.shape
    depth = w_in.shape[0]
    length = seq + N_META
    lp = -(-length // SEQ_ALIGN) * SEQ_ALIGN
    n = bsz * lp

    tm = _pick(n, (1536, 768, 512, 256))
    tm_mid = _pick(n, (768, 512, 256))
    tm_ffn = _pick(n, (768, 512, 256))
    tm_moe = _pick(n, (1408, 768, 512, 256))
    sb_router = _pick(tm_moe, (704, 768, 512, 256))
    tn = _pick(Z_W, (1792, 768, 256))
    ts = _pick(lp, (768, 512, 256))
    ts_f = ts
    tq = _pick(lp, (2816, 768, 256))
    ts_att, tk, att_lead = (256, 256, 256)
    tf_dense = _pick(w_ff_gate.shape[2], (256,))
    tf_moe = _pick(w_e_gate.shape[3], (1792, 512, 256))
    sub_moe = _pick(tf_moe, (256,))
    quarter = tm_moe * TOP_K // N_EXPERTS
    lo_size = max(quarter - MOE_SIZE_BELOW, MOE_SIZE_STEP) // MOE_SIZE_STEP * MOE_SIZE_STEP
    moe_sizes = tuple(range(lo_size, quarter + MOE_SIZE_ABOVE + 1, MOE_SIZE_STEP))

    h = jnp.concatenate([jnp.broadcast_to(meta[None].astype(x.dtype), (bsz, N_META, d)), x], axis=1)
    h = jnp.pad(h, ((0, 0), (0, lp - length), (0, 0))).reshape(n, d)
    hn = _norm(h, norm_mix[0][None], tm)

    att_scale = HEAD_DIM ** -0.5
    gate_lo = 3 * ATT_W + ATT_HEADS + 2 * CONV_W + POOL_W
    f_lo = 3 * ATT_W
    for l in range(depth):
        wl = w_in[l]
        w_main = jnp.concatenate([wl[:, gate_lo:], wl[:, :f_lo], wl[:, f_lo + ATT_HEADS:gate_lo]], axis=1).astype(BF16)
        bl = b_in[l]
        b_main = jnp.concatenate([bl[gate_lo:], bl[:f_lo], bl[f_lo + ATT_HEADS:gate_lo]])[None]
        w_f = _pad_lanes(wl[:, f_lo:f_lo + ATT_HEADS])
        w_f_hi = w_f.astype(BF16)
        w_f = jnp.concatenate([w_f_hi, (w_f - w_f_hi.astype(F32)).astype(BF16)], axis=1)
        b_f = _pad_lanes(bl[None, f_lo:f_lo + ATT_HEADS])

        z = _inproj(hn, w_main, b_main, tm, tn)
        z3 = z.reshape(bsz, lp, Z_W)
        fcol = _forget(hn.reshape(bsz, lp, d), w_f, b_f, ts_f)
        frow = jnp.transpose(fcol[:, :, :ATT_HEADS], (0, 2, 1))
        qg = jnp.tile(q_norm[l], 2)[None] * att_scale
        kg = jnp.tile(k_norm[l], 2)[None]
        att = _attention(z3, fcol, frow, qg, kg, tq, ts_att, tk, att_lead)

        cw = jnp.pad(conv_w[l], ((0, HALO - CONV_K), (0, 0)))
        pw = jax.scipy.linalg.block_diag(*[pool_w[l, g] for g in range(pool_w.shape[1])]).astype(BF16)
        u, pm = _convpool(z3, cw, conv_b[l][None], conv_ln_g[l][None], conv_ln_b[l][None], pw,
                          pool_scale[l][None], ts)

        h, hn = _merge(att.reshape(n, ATT_W), u.reshape(n, CONV_W), pm.reshape(n, POOL_W), z, h,
                       w_attn_o[l].astype(BF16), w_conv_o[l].astype(BF16), w_pool_o[l].astype(BF16),
                       w_out[l].astype(BF16), norm_ffn[l][None], tm_mid)

        ng_next = norm_mix[min(l + 1, depth - 1)][None]
        i = l // 2
        if l % 2 == 0:
            h, hn = _ffn(hn, h, w_ff_gate[i].astype(BF16), w_ff_up[i].astype(BF16), w_ff_down[i].astype(BF16),
                         ng_next, tm_ffn, tf_dense)
        else:
            w_r = _pad_lanes(w_router[i])
            w_r_hi = w_r.astype(BF16)
            w_r = jnp.concatenate([w_r_hi, (w_r - w_r_hi.astype(F32)).astype(BF16)], axis=1)
            comb, pos, cnt = _router(h, norm_ffn[l][None], w_r, _pad_lanes(b_router[i][None]), tm_moe, sb_router)
            post = jnp.transpose(pos[:, :POST_ROWS].reshape(n // tm_moe, tm_moe, POST_ROWS), (0, 2, 1))
            cnt_flat = cnt[:, 0, :N_EXPERTS].reshape(-1)
            delta = _moe(cnt_flat, hn, post, pos, comb, _to_bf16(w_e_gate, i), _to_bf16(w_e_up, i),
                         _to_bf16(w_e_down, i), tm_moe, tf_moe, moe_sizes, sub_moe)
            if l == depth - 1 and seq % OUT_ROWS == 0 and lp >= seq + OUT_ROWS:
                return _add_drop_meta(h.reshape(bsz, lp, d), delta.reshape(bsz, lp, d), seq)
            h, hn = _add_norm(h, delta, ng_next, tm)

    return h.reshape(bsz, lp, d)[:, N_META:length]
```

```python
import functools

import jax
import jax.numpy as jnp
from jax import lax
from jax.experimental import pallas as pl
from jax.experimental.pallas import tpu as pltpu

F32 = jnp.float32
BF16 = jnp.bfloat16

D_MODEL = 1024
N_META = 16
HEAD_DIM = 64
ATT_W = 512
ATT_HEADS = 8
CONV_W = 256
CONV_K = 31
POOL_W = 256
POOL_WINDOWS = (2, 4, 8, 16)
N_EXPERTS = 8
LANES = 128
SUBLANES = 8
HALO = 32
NEG_INF = -1e30
LOG2E = 1.4426950408889634
OUT_ROWS = 256
KNORM_ROWS = 256
EXP_UNDERFLOW = 110.0
SEQ_ALIGN = 256
VMEM_LIMIT = 56 * 1024 * 1024
MOE_VMEM_LIMIT = 60 * 1024 * 1024
TOP_K = 2
MOE_SIZE_STEP = 64
MOE_SIZE_BELOW = 96
MOE_SIZE_ABOVE = 160
POST_ROWS = 32

Z_Q = 3072
Z_K = 3584
Z_V = 4096
Z_A = 4608
Z_G = 4864
Z_P = 5120
Z_W = 5376


def _pick(n, candidates):
    for c in candidates:
        if n % c == 0:
            return c
    raise ValueError(f"no tile for {n} in {candidates}")


def _cparams(sem):
    return pltpu.CompilerParams(dimension_semantics=sem, vmem_limit_bytes=VMEM_LIMIT)


def _rms(x, g, eps=1e-6):
    return x * lax.rsqrt(jnp.mean(x * x, axis=-1, keepdims=True) + eps) * g


def _sigmoid(x):
    return 0.5 * jnp.tanh(0.5 * x) + 0.5


def _silu(x):
    return x * _sigmoid(x)


def _cast_kernel(w_ref, o_ref):
    o_ref[...] = w_ref[0].astype(o_ref.dtype)


def _to_bf16(w, layer):
    _, s, r, c = w.shape
    rows = _pick(r, (512, 256, 128, 8))
    return pl.pallas_call(
        _cast_kernel,
        grid=(s, r // rows),
        in_specs=[pl.BlockSpec((1, 1, rows, c), lambda i, j: (layer, i, j, 0))],
        out_specs=pl.BlockSpec((1, rows, c), lambda i, j: (i, j, 0)),
        out_shape=jax.ShapeDtypeStruct((s, r, c), BF16),
        compiler_params=_cparams(("parallel", "parallel")),
        name="weights_to_bf16",
    )(w)


def _norm_kernel(h_ref, g_ref, o_ref):
    o_ref[...] = _rms(h_ref[...], g_ref[...]).astype(o_ref.dtype)


def _norm(h, g, tm):
    n, d = h.shape
    return pl.pallas_call(
        _norm_kernel,
        grid=(n // tm,),
        in_specs=[pl.BlockSpec((tm, d), lambda i: (i, 0)), pl.BlockSpec((1, d), lambda i: (0, 0))],
        out_specs=pl.BlockSpec((tm, d), lambda i: (i, 0)),
        out_shape=jax.ShapeDtypeStruct((n, d), BF16),
        compiler_params=_cparams(("parallel",)),
        name="rmsnorm",
    )(h, g)


def _inproj_kernel(x_ref, w_ref, b_ref, o_ref):
    acc = jnp.dot(x_ref[...], w_ref[...], preferred_element_type=F32)
    o_ref[...] = (acc + b_ref[...]).astype(o_ref.dtype)


def _inproj(hn, w, b, tm, tn):
    n, d = hn.shape
    zw = w.shape[1]
    return pl.pallas_call(
        _inproj_kernel,
        grid=(zw // tn, n // tm),
        in_specs=[pl.BlockSpec((tm, d), lambda j, i: (i, 0)),
                  pl.BlockSpec((d, tn), lambda j, i: (0, j)),
                  pl.BlockSpec((1, tn), lambda j, i: (0, j))],
        out_specs=pl.BlockSpec((tm, tn), lambda j, i: (i, j)),
        out_shape=jax.ShapeDtypeStruct((n, zw), BF16),
        compiler_params=_cparams(("parallel", "parallel")),
        name="inproj",
    )(hn, w, b)


def _forget_kernel(x_ref, w_ref, b_ref, o_ref, carry_ref):
    @pl.when(pl.program_id(1) == 0)
    def _():
        carry_ref[...] = jnp.zeros_like(carry_ref)

    fw = jnp.dot(x_ref[0], w_ref[...], preferred_element_type=F32)
    f = fw[:, :LANES] + fw[:, LANES:] + b_ref[...]
    ls = jnp.minimum(f, 0.0) - jnp.log(1.0 + jnp.exp(-jnp.abs(f)))
    hi = ls.astype(BF16)
    r1 = ls - hi.astype(F32)
    mid = r1.astype(BF16)
    lo = (r1 - mid.astype(F32)).astype(BF16)
    t = ls.shape[0]
    r = lax.broadcasted_iota(jnp.int32, (t, t), 0)
    c = lax.broadcasted_iota(jnp.int32, (t, t), 1)
    tril = jnp.where(c <= r, 1.0, 0.0).astype(BF16)
    parts = jnp.dot(tril, jnp.concatenate([hi, mid, lo], axis=-1), preferred_element_type=F32)
    cs = (parts[:, :LANES] + parts[:, LANES:2 * LANES]) + parts[:, 2 * LANES:] + carry_ref[...]
    o_ref[0] = cs
    carry_ref[...] = cs[t - 1:t, :]


def _forget(hn3, w_f, b_f, ts):
    b, lp, d = hn3.shape
    return pl.pallas_call(
        _forget_kernel,
        grid=(b, lp // ts),
        in_specs=[pl.BlockSpec((1, ts, d), lambda bi, t: (bi, t, 0)),
                  pl.BlockSpec((d, 2 * LANES), lambda bi, t: (0, 0)),
                  pl.BlockSpec((1, LANES), lambda bi, t: (0, 0))],
        out_specs=pl.BlockSpec((1, ts, LANES), lambda bi, t: (bi, t, 0)),
        out_shape=jax.ShapeDtypeStruct((b, lp, LANES), F32),
        scratch_shapes=[pltpu.VMEM((1, LANES), F32)],
        compiler_params=_cparams(("parallel", "arbitrary")),
        name="forget_cumsum",
    )(hn3, w_f, b_f)


def _pair_rms(x, first_head, gain, eps=1e-6):
    sq = x * x
    s0 = jnp.sum(jnp.where(first_head, sq, 0.0), axis=-1, keepdims=True)
    s1 = jnp.sum(jnp.where(first_head, 0.0, sq), axis=-1, keepdims=True)
    ms = jnp.where(first_head, s0, s1) * (1.0 / HEAD_DIM)
    return x * lax.rsqrt(ms + eps) * gain


def _attn_kernel(lo_ref, q_ref, k_ref, v_ref, fc_ref, fr_ref, qg_ref, kg_ref, o_ref, kn_ref, *,
                 tq, ts, tk, lead):
    bi = pl.program_id(0)
    p = pl.program_id(1)
    i = pl.program_id(2)
    nq = pl.num_programs(2)
    lp = k_ref.shape[1]
    lane = lax.broadcasted_iota(jnp.int32, (1, LANES), 1)
    first_head = lane < HEAD_DIM

    @pl.when(i == 0)
    def _():
        def body(c, carry):
            off = pl.multiple_of(c * KNORM_ROWS, KNORM_ROWS)
            kk = k_ref[0, pl.ds(off, KNORM_ROWS), :].astype(F32)
            kn_ref[pl.ds(off, KNORM_ROWS), :] = _pair_rms(kk, first_head, kg_ref[...]).astype(BF16)
            return carry
        lax.fori_loop(0, lp // KNORM_ROWS, body, 0)

    qn = _pair_rms(q_ref[0].astype(F32), first_head, qg_ref[...])
    fc = fc_ref[0]

    heads = [2 * p, 2 * p + 1]
    qhs = [jnp.where(first_head, qn, 0.0).astype(BF16), jnp.where(first_head, 0.0, qn).astype(BF16)]
    fts = [jnp.sum(jnp.where(lane == hd, fc, 0.0), axis=-1, keepdims=True) * LOG2E for hd in heads]

    def update(rows, off, width, carry, hh, row_pos=None, own_mask=None):
        m, l, acc = carry
        ks = kn_ref[pl.ds(off, width), :]
        s = lax.dot_general(qhs[hh][rows], ks, (((1,), (1,)), ((), ())), preferred_element_type=F32)
        f_all = fr_ref[0, :, pl.ds(off, width)]
        head_row = lax.broadcasted_iota(jnp.int32, (ATT_HEADS, 1), 0) == heads[hh]
        fs = jnp.sum(jnp.where(head_row, f_all, 0.0), axis=0, keepdims=True) * LOG2E
        s = s + (fts[hh][rows] - fs)
        if own_mask is not None:
            s = jnp.concatenate([s[:, :lead], jnp.where(own_mask, s[:, lead:], NEG_INF)], axis=1)
        elif row_pos is not None:
            col_pos = off + lax.broadcasted_iota(jnp.int32, (1, width), 1)
            s = jnp.where(col_pos <= row_pos, s, NEG_INF)
        m_new = jnp.maximum(m, jnp.max(s, axis=-1, keepdims=True))
        alpha = jnp.exp2(m - m_new)
        pm = jnp.exp2(s - m_new)
        l = alpha * l + jnp.sum(pm, axis=-1, keepdims=True)
        vs = v_ref[0, pl.ds(off, width), :]
        acc = alpha * acc + jnp.dot(pm.astype(BF16), vs, preferred_element_type=F32)
        return m_new, l, acc

    nsub = tq // ts
    init = (jnp.full((ts, 1), NEG_INF, F32), jnp.zeros((ts, 1), F32), jnp.zeros((ts, LANES), F32))
    carries, wins = [], []
    for u in range(nsub):
        rows = slice(u * ts, (u + 1) * ts)
        g = i * nsub + u
        win = jnp.maximum(g * (ts // tk) - lead // tk, 0)
        firsts = [lo_ref[(bi * ATT_HEADS + hd) * (nq * nsub) + g] for hd in heads]

        def far(j, carry, rows=rows):
            off = pl.multiple_of(j * tk, tk)
            return tuple(update(rows, off, tk, carry[hh], hh) for hh in range(2))

        carries.append(lax.fori_loop(jnp.minimum(firsts[0], firsts[1]), win, far, (init, init)))
        wins.append(win)
    lower_tri = (lax.broadcasted_iota(jnp.int32, (ts, ts), 1) <= lax.broadcasted_iota(jnp.int32, (ts, ts), 0))
    for u in range(nsub):
        rows = slice(u * ts, (u + 1) * ts)
        off = pl.multiple_of(wins[u] * tk, tk)
        if u == 0:
            mask = dict(row_pos=i * tq + lax.broadcasted_iota(jnp.int32, (ts, 1), 0))
        else:
            mask = dict(own_mask=lower_tri)
        outs = []
        for hh in range(2):
            m, l, acc = update(rows, off, lead + ts, carries[u][hh], hh, **mask)
            outs.append(acc / l)
        o_ref[0, rows, :] = jnp.where(first_head, outs[0], outs[1]).astype(o_ref.dtype)


def _first_live_chunk(frow, qg, kg, tq, tk):
    b, nh, lp = frow.shape
    qk_bound = 1.02 * HEAD_DIM * jnp.max(jnp.abs(qg)) * jnp.max(jnp.abs(kg))
    f_first = frow[:, :, 0::tq]
    f_last = frow[:, :, tk - 1::tk]
    dead = (f_first[:, :, :, None] - f_last[:, :, None, :] + 2.0 * qk_bound) < -EXP_UNDERFLOW
    n_dead = jnp.sum(dead.astype(jnp.int32), axis=-1)
    n_full = (jnp.arange(lp // tq, dtype=jnp.int32) * tq) // tk
    return jnp.minimum(n_dead, n_full[None, None, :]).reshape(-1)


def _attention(z3, fcol, frow, qg, kg, tq, ts, tk, lead):
    b, lp, _ = z3.shape
    npairs = ATT_HEADS // 2
    qb, kb, vb = Z_Q // LANES, Z_K // LANES, Z_V // LANES
    assert tq % ts == 0 and ts % tk == 0 and lead % tk == 0 and lp >= lead + ts and lp % KNORM_ROWS == 0
    first = _first_live_chunk(frow, qg, kg, ts, tk)
    grid_spec = pltpu.PrefetchScalarGridSpec(
        num_scalar_prefetch=1,
        grid=(b, npairs, lp // tq),
        in_specs=[pl.BlockSpec((1, tq, LANES), lambda bi, p, i, lo: (bi, i, qb + p)),
                  pl.BlockSpec((1, lp, LANES), lambda bi, p, i, lo: (bi, 0, kb + p)),
                  pl.BlockSpec((1, lp, LANES), lambda bi, p, i, lo: (bi, 0, vb + p)),
                  pl.BlockSpec((1, tq, LANES), lambda bi, p, i, lo: (bi, i, 0)),
                  pl.BlockSpec((1, ATT_HEADS, lp), lambda bi, p, i, lo: (bi, 0, 0)),
                  pl.BlockSpec((1, LANES), lambda bi, p, i, lo: (0, 0)),
                  pl.BlockSpec((1, LANES), lambda bi, p, i, lo: (0, 0))],
        out_specs=pl.BlockSpec((1, tq, LANES), lambda bi, p, i, lo: (bi, i, p)),
        scratch_shapes=[pltpu.VMEM((lp, LANES), BF16)],
    )
    return pl.pallas_call(
        functools.partial(_attn_kernel, tq=tq, ts=ts, tk=tk, lead=lead),
        grid_spec=grid_spec,
        out_shape=jax.ShapeDtypeStruct((b, lp, ATT_W), BF16),
        compiler_params=_cparams(("parallel", "parallel", "arbitrary")),
        name="fox_attention",
    )(first, z3, z3, z3, fcol, frow, qg * LOG2E, kg)


def _convpool_kernel(a_ref, g_ref, p_ref, ah_ref, gh_ref, ph_ref, cw_ref, cb_ref, lg_ref, lb_ref,
                     pw_ref, ps_ref, u_ref, pm_ref, ext_ref, pext_ref, *, ts):
    i = pl.program_id(1)
    has_prev = i > 0

    def stage(ref, halo_rows, rows):
        ref[0, 0:HALO, :] = halo_rows
        ref[0, HALO:HALO + ts, :] = rows
        base = ref[0]
        for r in range(1, SUBLANES):
            ref[r, r:HALO + ts, :] = base[0:HALO + ts - r, :]

    def behind(ref, back):
        start = HALO - back // SUBLANES * SUBLANES
        return ref[back % SUBLANES, start:start + ts, :]

    u = a_ref[0].astype(F32) * _sigmoid(g_ref[0].astype(F32))
    uh = ah_ref[0].astype(F32) * _sigmoid(gh_ref[0].astype(F32))
    stage(ext_ref, jnp.where(has_prev, uh, 0.0), u)
    acc = jnp.zeros((ts, CONV_W), F32) + cb_ref[...]
    for j in range(CONV_K):
        acc = acc + cw_ref[j:j + 1, :] * behind(ext_ref, CONV_K - 1 - j)
    mu = jnp.mean(acc, axis=-1, keepdims=True)
    cen = acc - mu
    var = jnp.mean(cen * cen, axis=-1, keepdims=True)
    y = cen * lax.rsqrt(var + 1e-5) * lg_ref[...] + lb_ref[...]
    u_ref[0] = _silu(y).astype(u_ref.dtype)

    x = p_ref[0].astype(F32)
    stage(pext_ref, jnp.where(has_prev, ph_ref[0].astype(F32), 0.0), x)
    pos1 = (i * ts + 1 + lax.broadcasted_iota(jnp.int32, (ts, 1), 0)).astype(F32)
    lane = lax.broadcasted_iota(jnp.int32, (1, POOL_W), 1)
    group_w = POOL_W // len(POOL_WINDOWS)
    run = x
    pooled = jnp.zeros((ts, POOL_W), F32)
    for k in range(1, max(POOL_WINDOWS)):
        run = run + behind(pext_ref, k)
        if (k + 1) in POOL_WINDOWS:
            gi = POOL_WINDOWS.index(k + 1)
            mean = run / jnp.minimum(pos1, float(k + 1))
            in_group = (lane >= gi * group_w) & (lane < (gi + 1) * group_w)
            pooled = jnp.where(in_group, mean, pooled)
    pm = (pooled - x).astype(BF16)
    lin = jnp.dot(pm, pw_ref[...], preferred_element_type=F32) * ps_ref[...]
    pm_ref[0] = lin.astype(pm_ref.dtype)


def _convpool(z3, cw, cb, lg, lb, pw, ps, ts):
    b, lp, _ = z3.shape
    ab, gb, pb = Z_A // CONV_W, Z_G // CONV_W, Z_P // POOL_W
    hpt = ts // HALO
    main = lambda blk: pl.BlockSpec((1, ts, CONV_W), lambda bi, i: (bi, i, blk))
    halo = lambda blk: pl.BlockSpec((1, HALO, CONV_W), lambda bi, i: (bi, jnp.maximum(i * hpt - 1, 0), blk))
    const = lambda shape: pl.BlockSpec(shape, lambda bi, i: (0, 0))
    out = pl.BlockSpec((1, ts, CONV_W), lambda bi, i: (bi, i, 0))
    return pl.pallas_call(
        functools.partial(_convpool_kernel, ts=ts),
        grid=(b, lp // ts),
        in_specs=[main(ab), main(gb), main(pb), halo(ab), halo(gb), halo(pb),
                  const((HALO, CONV_W)), const((1, CONV_W)), const((1, CONV_W)), const((1, CONV_W)),
                  const((POOL_W, POOL_W)), const((1, POOL_W))],
        out_specs=[out, out],
        out_shape=[jax.ShapeDtypeStruct((b, lp, CONV_W), BF16), jax.ShapeDtypeStruct((b, lp, POOL_W), BF16)],
        scratch_shapes=[pltpu.VMEM((SUBLANES, HALO + ts, CONV_W), F32),
                        pltpu.VMEM((SUBLANES, HALO + ts, POOL_W), F32)],
        compiler_params=_cparams(("parallel", "arbitrary")),
        name="conv_pool",
    )(z3, z3, z3, z3, z3, z3, cw, cb, lg, lb, pw, ps)


def _merge_kernel(att_ref, u_ref, pm_ref, g0_ref, g1_ref, g2_ref, h_ref, wa_ref, wc_ref, wp_ref, wo_ref,
                  ng_ref, h_out_ref, hn_out_ref):
    ya = jnp.dot(att_ref[...], wa_ref[...], preferred_element_type=F32)
    yc = jnp.dot(u_ref[...], wc_ref[...], preferred_element_type=F32)
    yp = jnp.dot(pm_ref[...], wp_ref[...], preferred_element_type=F32)
    m = (_sigmoid(g0_ref[...].astype(F32)) * ya + _sigmoid(g1_ref[...].astype(F32)) * yc
         + _sigmoid(g2_ref[...].astype(F32)) * yp)
    h = h_ref[...] + jnp.dot(m.astype(BF16), wo_ref[...], preferred_element_type=F32)
    h_out_ref[...] = h
    hn_out_ref[...] = _rms(h, ng_ref[...]).astype(hn_out_ref.dtype)


def _merge(att, u, pm, z, h, wa, wc, wp, wo, ng, tm):
    n, d = h.shape
    row = lambda w, blk=0: pl.BlockSpec((tm, w), lambda i: (i, blk))
    const = lambda shape: pl.BlockSpec(shape, lambda i: (0, 0))
    return pl.pallas_call(
        _merge_kernel,
        grid=(n // tm,),
        in_specs=[row(ATT_W), row(CONV_W), row(POOL_W), row(d, 0), row(d, 1), row(d, 2), row(d),
                  const(wa.shape), const(wc.shape), const(wp.shape), const(wo.shape), const((1, d))],
        out_specs=[row(d), row(d)],
        out_shape=[jax.ShapeDtypeStruct((n, d), F32), jax.ShapeDtypeStruct((n, d), BF16)],
        compiler_params=_cparams(("parallel",)),
        name="merge_outproj",
    )(att, u, pm, z, z, z, h, wa, wc, wp, wo, ng)


def _ffn_kernel(x_ref, h_ref, wg_ref, wu_ref, wd_ref, ng_ref, h_out_ref, hn_out_ref, *, tf):
    x = x_ref[...]
    h = h_ref[...]
    nc = wg_ref.shape[1] // tf

    def gate_up(c):
        cols = slice(c * tf, (c + 1) * tf)
        return (jnp.dot(x, wg_ref[:, cols], preferred_element_type=F32),
                jnp.dot(x, wu_ref[:, cols], preferred_element_type=F32))

    gt, up = gate_up(0)
    for c in range(nc):
        act = (_silu(gt) * up).astype(BF16)
        if c + 1 < nc:
            gt, up = gate_up(c + 1)
        h = h + jnp.dot(act, wd_ref[c * tf:(c + 1) * tf, :], preferred_element_type=F32)
    h_out_ref[...] = h
    hn_out_ref[...] = _rms(h, ng_ref[...]).astype(hn_out_ref.dtype)


def _ffn(hn, h, wg, wu, wd, ng, tm, tf):
    n, d = h.shape
    row = pl.BlockSpec((tm, d), lambda i: (i, 0))
    const = lambda shape: pl.BlockSpec(shape, lambda i: (0, 0), pipeline_mode=pl.Buffered(1))
    return pl.pallas_call(
        functools.partial(_ffn_kernel, tf=tf),
        grid=(n // tm,),
        in_specs=[row, row, const(wg.shape), const(wu.shape), const(wd.shape), const((1, d))],
        out_specs=[row, row],
        out_shape=[jax.ShapeDtypeStruct((n, d), F32), jax.ShapeDtypeStruct((n, d), BF16)],
        compiler_params=_cparams(("parallel",)),
        name="dense_swiglu",
    )(hn, h, wg, wu, wd, ng)


def _router_kernel(h_ref, ng_ref, wr_ref, br_ref, comb_ref, pos_ref, cnt_ref, *, sb):
    tm = h_ref.shape[0]
    hn = _rms(h_ref[...], ng_ref[...])
    h_hi = hn.astype(BF16)
    h_lo = (hn - h_hi.astype(F32)).astype(BF16)
    hw = jnp.dot(h_hi, wr_ref[...], preferred_element_type=F32)
    lw = jnp.dot(h_lo, wr_ref[:, :LANES], preferred_element_type=F32)
    logits = hw[:, :LANES] + (hw[:, LANES:] + lw) + br_ref[...]
    lane = lax.broadcasted_iota(jnp.int32, (1, LANES), 1).astype(F32)
    lg = jnp.where(lane < N_EXPERTS, logits, -jnp.inf)
    m1 = jnp.max(lg, axis=-1, keepdims=True)
    i1 = jnp.min(jnp.where(lg == m1, lane, float(LANES)), axis=-1, keepdims=True)
    sel1 = lane == i1
    lg2 = jnp.where(sel1, -jnp.inf, lg)
    m2 = jnp.max(lg2, axis=-1, keepdims=True)
    i2 = jnp.min(jnp.where(lg2 == m2, lane, float(LANES)), axis=-1, keepdims=True)
    sel2 = lane == i2
    e = jnp.exp(m2 - m1)
    g1 = 1.0 / (1.0 + e)
    comb_ref[...] = jnp.where(sel1, g1, 0.0) + jnp.where(sel2, e * g1, 0.0)
    sel = jnp.where(sel1 | sel2, 1.0, 0.0)
    r = lax.broadcasted_iota(jnp.int32, (sb, sb), 0)
    c = lax.broadcasted_iota(jnp.int32, (sb, sb), 1)
    tril = jnp.where(c <= r, 1.0, 0.0).astype(BF16)
    carry = jnp.zeros((1, LANES), F32)
    for s in range(tm // sb):
        blk = slice(s * sb, (s + 1) * sb)
        incl = jnp.dot(tril, sel[blk].astype(BF16), preferred_element_type=F32) + carry
        pos_ref[blk, :] = jnp.where(sel[blk] > 0.0, incl - 1.0, -1.0)
        carry = incl[sb - 1:sb, :]
    cnt_ref[0] = jnp.broadcast_to(carry, (8, LANES)).astype(jnp.int32)


def _router(h, ng, wr, br, tm, sb):
    n, d = h.shape
    nt = n // tm
    return pl.pallas_call(
        functools.partial(_router_kernel, sb=sb),
        grid=(nt,),
        in_specs=[pl.BlockSpec((tm, d), lambda i: (i, 0)),
                  pl.BlockSpec((1, d), lambda i: (0, 0)),
                  pl.BlockSpec((d, 2 * LANES), lambda i: (0, 0)),
                  pl.BlockSpec((1, LANES), lambda i: (0, 0))],
        out_specs=[pl.BlockSpec((tm, LANES), lambda i: (i, 0)),
                   pl.BlockSpec((tm, LANES), lambda i: (i, 0)),
                   pl.BlockSpec((1, 8, LANES), lambda i: (i, 0, 0))],
        out_shape=[jax.ShapeDtypeStruct((n, LANES), F32), jax.ShapeDtypeStruct((n, LANES), F32),
                   jax.ShapeDtypeStruct((nt, 8, LANES), jnp.int32)],
        compiler_params=_cparams(("parallel",)),
        name="router_top2",
    )(h, ng, wr, br)


def _moe_kernel(cnt_ref, x_ref, post_ref, pos_ref, comb_ref, wg_ref, wu_ref, wd_ref,
                out_ref, xs_ref, ys_ref, *, sizes, sub):
    i = pl.program_id(0)
    e = pl.program_id(1)
    f = pl.program_id(2)
    nf = pl.num_programs(2)
    cnt = cnt_ref[i * N_EXPERTS + e]
    big = sizes[-1]

    def for_each_chunk(body):
        lo = 0
        for r in sizes:
            @pl.when((cnt > lo) & (cnt <= r))
            def _(r=r):
                body(0, r)
            lo = r

        @pl.when(cnt > big)
        def _():
            def step(c, carry):
                body(pl.multiple_of(c * big, big), big)
                return carry
            lax.fori_loop(0, (cnt + big - 1) // big, step, 0)

    @pl.when((e == 0) & (f == 0))
    def _():
        out_ref[...] = jnp.zeros_like(out_ref)

    @pl.when(f == 0)
    def _():
        prow = post_ref[0, pl.ds(e, 1), :]

        def gather(off, r):
            tgt = (off + lax.broadcasted_iota(jnp.int32, (r, 1), 0)).astype(F32)
            onehot = jnp.where(prow == tgt, 1.0, 0.0).astype(BF16)
            xs_ref[pl.ds(off, r), :] = jnp.dot(onehot, x_ref[...], preferred_element_type=F32).astype(BF16)
            ys_ref[pl.ds(off, r), :] = jnp.zeros((r, D_MODEL), F32)
        for_each_chunk(gather)

    def expert(off, r):
        xc = xs_ref[pl.ds(off, r), :]
        y = ys_ref[pl.ds(off, r), :]
        tf = wg_ref.shape[2]
        edges = list(range(0, tf, sub)) + [tf]
        spans = [slice(a, b) for a, b in zip(edges[:-1], edges[1:])]

        def gate_up(cols):
            return (jnp.dot(xc, wg_ref[0, :, cols], preferred_element_type=F32),
                    jnp.dot(xc, wu_ref[0, :, cols], preferred_element_type=F32))

        gt, up = gate_up(spans[0])
        for s, cols in enumerate(spans):
            act = (_silu(gt) * up).astype(BF16)
            if s + 1 < len(spans):
                gt, up = gate_up(spans[s + 1])
            y = y + jnp.dot(act, wd_ref[0, cols, :], preferred_element_type=F32)
        ys_ref[pl.ds(off, r), :] = y
    for_each_chunk(expert)

    @pl.when(f == nf - 1)
    def _():
        lane = lax.broadcasted_iota(jnp.int32, (1, LANES), 1)
        pcol = jnp.sum(jnp.where(lane == e, pos_ref[...], 0.0), axis=-1, keepdims=True)
        gcol = jnp.sum(jnp.where(lane == e, comb_ref[...], 0.0), axis=-1, keepdims=True)

        def scatter(off, r):
            tgt = (off + lax.broadcasted_iota(jnp.int32, (1, r), 1)).astype(F32)
            onehot_t = jnp.where(pcol == tgt, 1.0, 0.0).astype(BF16)
            y = ys_ref[pl.ds(off, r), :].astype(BF16)
            out_ref[...] += gcol * jnp.dot(onehot_t, y, preferred_element_type=F32)
        for_each_chunk(scatter)


def _moe(cnt, hn, post, pos, comb, wg, wu, wd, tm, tf, sizes, sub):
    n, d = hn.shape
    nf = wg.shape[2] // tf
    cap = -(-tm // sizes[-1]) * sizes[-1]
    once = pl.Buffered(1)
    row = lambda w: pl.BlockSpec((tm, w), lambda i, e, f, cnt: (i, 0), pipeline_mode=once)
    grid_spec = pltpu.PrefetchScalarGridSpec(
        num_scalar_prefetch=1,
        grid=(n // tm, N_EXPERTS, nf),
        in_specs=[row(d),
                  pl.BlockSpec((1, POST_ROWS, tm), lambda i, e, f, cnt: (i, 0, 0), pipeline_mode=once),
                  row(LANES), row(LANES),
                  pl.BlockSpec((1, d, tf), lambda i, e, f, cnt: (e, 0, f)),
                  pl.BlockSpec((1, d, tf), lambda i, e, f, cnt: (e, 0, f)),
                  pl.BlockSpec((1, tf, d), lambda i, e, f, cnt: (e, f, 0))],
        out_specs=pl.BlockSpec((tm, d), lambda i, e, f, cnt: (i, 0)),
        scratch_shapes=[pltpu.VMEM((cap, d), BF16), pltpu.VMEM((cap, d), F32)],
    )
    return pl.pallas_call(
        functools.partial(_moe_kernel, sizes=sizes, sub=sub),
        grid_spec=grid_spec,
        out_shape=jax.ShapeDtypeStruct((n, d), F32),
        compiler_params=pltpu.CompilerParams(dimension_semantics=("parallel", "arbitrary", "arbitrary"),
                                             vmem_limit_bytes=MOE_VMEM_LIMIT),
        name="expert_swiglu",
    )(cnt, hn, post, pos, comb, wg, wu, wd)


def _add_norm_kernel(h_ref, d_ref, g_ref, h_out_ref, hn_out_ref):
    h = h_ref[...] + d_ref[...]
    h_out_ref[...] = h
    hn_out_ref[...] = _rms(h, g_ref[...]).astype(hn_out_ref.dtype)


def _add_norm(h, delta, g, tm):
    n, d = h.shape
    row = pl.BlockSpec((tm, d), lambda i: (i, 0))
    return pl.pallas_call(
        _add_norm_kernel,
        grid=(n // tm,),
        in_specs=[row, row, pl.BlockSpec((1, d), lambda i: (0, 0))],
        out_specs=[row, row],
        out_shape=[jax.ShapeDtypeStruct((n, d), F32), jax.ShapeDtypeStruct((n, d), BF16)],
        compiler_params=_cparams(("parallel",)),
        name="residual_norm",
    )(h, delta, g)


def _add_drop_meta_kernel(h_ref, d_ref, ht_ref, dt_ref, o_ref):
    body = h_ref[0] + d_ref[0]
    tail = ht_ref[0] + dt_ref[0]
    o_ref[0] = jnp.concatenate([body[N_META:], tail], axis=0)


def _add_drop_meta(h3, delta3, seq):
    b, lp, d = h3.shape
    per = OUT_ROWS // N_META
    main = pl.BlockSpec((1, OUT_ROWS, d), lambda bi, j: (bi, j, 0))
    tail = pl.BlockSpec((1, N_META, d), lambda bi, j: (bi, (j + 1) * per, 0))
    return pl.pallas_call(
        _add_drop_meta_kernel,
        grid=(b, seq // OUT_ROWS),
        in_specs=[main, main, tail, tail],
        out_specs=main,
        out_shape=jax.ShapeDtypeStruct((b, seq, d), F32),
        compiler_params=_cparams(("parallel", "parallel")),
        name="residual_drop_meta",
    )(h3, delta3, h3, delta3)


def _pad_lanes(a, width=LANES):
    return jnp.pad(a, ((0, 0), (0, width - a.shape[-1])))


def kernel(x, meta, norm_mix, w_in, b_in, q_norm, k_norm, w_attn_o, conv_w, conv_b, conv_ln_g, conv_ln_b,
           w_conv_o, pool_w, pool_scale, w_pool_o, w_out, norm_ffn, w_ff_gate, w_ff_up, w_ff_down, w_router,
           b_router, w_e_gate, w_e_up, w_e_down):
    bsz, seq, d = x.shape
    depth = w_in.shape[0]
    length = seq + N_META
    lp = -(-length // SEQ_ALIGN) * SEQ_ALIGN
    n = bsz * lp

    tm = _pick(n, (1536, 768, 512, 256))
    tm_mid = _pick(n, (768, 512, 256))
    tm_ffn = _pick(n, (768, 512, 256))
    tm_moe = _pick(n, (1408, 768, 512, 256))
    sb_router = _pick(tm_moe, (704, 768, 512, 256))
    tn = _pick(Z_W, (1792, 768, 256))
    ts = _pick(lp, (768, 512, 256))
    ts_f = ts
    tq = _pick(lp, (2816, 768, 256))
    ts_att, tk, att_lead = (256, 256, 256)
    tf_dense = _pick(w_ff_gate.shape[2], (256,))
    tf_moe = _pick(w_e_gate.shape[3], (1792, 512, 256))
    sub_moe = _pick(tf_moe, (256,))
    quarter = tm_moe * TOP_K // N_EXPERTS
    lo_size = max(quarter - MOE_SIZE_BELOW, MOE_SIZE_STEP) // MOE_SIZE_STEP * MOE_SIZE_STEP
    moe_sizes = tuple(range(lo_size, quarter + MOE_SIZE_ABOVE + 1, MOE_SIZE_STEP))

    h = jnp.concatenate([jnp.broadcast_to(meta[None].astype(x.dtype), (bsz, N_META, d)), x], axis=1)
    h = jnp.pad(h, ((0, 0), (0, lp - length), (0, 0))).reshape(n, d)
    hn = _norm(h, norm_mix[0][None], tm)

    att_scale = HEAD_DIM ** -0.5
    gate_lo = 3 * ATT_W + ATT_HEADS + 2 * CONV_W + POOL_W
    f_lo = 3 * ATT_W
    for l in range(depth):
        wl = w_in[l]
        w_main = jnp.concatenate([wl[:, gate_lo:], wl[:, :f_lo], wl[:, f_lo + ATT_HEADS:gate_lo]], axis=1).astype(BF16)
        bl = b_in[l]
        b_main = jnp.concatenate([bl[gate_lo:], bl[:f_lo], bl[f_lo + ATT_HEADS:gate_lo]])[None]
        w_f = _pad_lanes(wl[:, f_lo:f_lo + ATT_HEADS])
        w_f_hi = w_f.astype(BF16)
        w_f = jnp.concatenate([w_f_hi, (w_f - w_f_hi.astype(F32)).astype(BF16)], axis=1)
        b_f = _pad_lanes(bl[None, f_lo:f_lo + ATT_HEADS])

        z = _inproj(hn, w_main, b_main, tm, tn)
        z3 = z.reshape(bsz, lp, Z_W)
        fcol = _forget(hn.reshape(bsz, lp, d), w_f, b_f, ts_f)
        frow = jnp.transpose(fcol[:, :, :ATT_HEADS], (0, 2, 1))
        qg = jnp.tile(q_norm[l], 2)[None] * att_scale
        kg = jnp.tile(k_norm[l], 2)[None]
        att = _attention(z3, fcol, frow, qg, kg, tq, ts_att, tk, att_lead)

        cw = jnp.pad(conv_w[l], ((0, HALO - CONV_K), (0, 0)))
        pw = jax.scipy.linalg.block_diag(*[pool_w[l, g] for g in range(pool_w.shape[1])]).astype(BF16)
        u, pm = _convpool(z3, cw, conv_b[l][None], conv_ln_g[l][None], conv_ln_b[l][None], pw,
                          pool_scale[l][None], ts)

        h, hn = _merge(att.reshape(n, ATT_W), u.reshape(n, CONV_W), pm.reshape(n, POOL_W), z, h,
                       w_attn_o[l].astype(BF16), w_conv_o[l].astype(BF16), w_pool_o[l].astype(BF16),
                       w_out[l].astype(BF16), norm_ffn[l][None], tm_mid)

        ng_next = norm_mix[min(l + 1, depth - 1)][None]
        i = l // 2
        if l % 2 == 0:
            h, hn = _ffn(hn, h, w_ff_gate[i].astype(BF16), w_ff_up[i].astype(BF16), w_ff_down[i].astype(BF16),
                         ng_next, tm_ffn, tf_dense)
        else:
            w_r = _pad_lanes(w_router[i])
            w_r_hi = w_r.astype(BF16)
            w_r = jnp.concatenate([w_r_hi, (w_r - w_r_hi.astype(F32)).astype(BF16)], axis=1)
            comb, pos, cnt = _router(h, norm_ffn[l][None], w_r, _pad_lanes(b_router[i][None]), tm_moe, sb_router)
            post = jnp.transpose(pos[:, :POST_ROWS].reshape(n // tm_moe, tm_moe, POST_ROWS), (0, 2, 1))
            cnt_flat = cnt[:, 0, :N_EXPERTS].reshape(-1)
            delta = _moe(cnt_flat, hn, post, pos, comb, _to_bf16(w_e_gate, i), _to_bf16(w_e_up, i),
                         _to_bf16(w_e_down, i), tm_moe, tf_moe, moe_sizes, sub_moe)
            if l == depth - 1 and seq % OUT_ROWS == 0 and lp >= seq + OUT_ROWS:
                return _add_drop_meta(h.reshape(bsz, lp, d), delta.reshape(bsz, lp, d), seq)
            h, hn = _add_norm(h, delta, ng_next, tm)

    return h.reshape(bsz, lp, d)[:, N_META:length]
```
